```python
import jax
import jax.numpy as jnp
from jax import lax
import numpy as np

D_MODEL = 1024
BATCH = 16
SEQ = 4096
DEPTH = 4

GRID_W = 64
CTX_LEN = 256
N_MIXERS = 3
N_ATTN_LAYERS = (DEPTH + 2) // N_MIXERS
N_RWKV_LAYERS = (DEPTH + 1) // N_MIXERS
N_POOL_LAYERS = DEPTH // N_MIXERS
N_MOD = 6
EPS = 1e-6
N_HEADS = 16
N_KV_HEADS = 4
HEAD_DIM = D_MODEL // N_HEADS
GQA_REP = N_HEADS // N_KV_HEADS
Q_WIDTH = N_HEADS * HEAD_DIM
KV_WIDTH = N_KV_HEADS * HEAD_DIM
QKV_WIDTH = Q_WIDTH + 2 * KV_WIDTH
ROPE_THETA = 10000.0
Q_BLOCK = 128
RWKV_HEAD = 64
RWKV_HEADS = D_MODEL // RWKV_HEAD
DECAY_LORA = 64
ICLR_LORA = 64
GATE_LORA = 160
GN_EPS = RWKV_HEAD * 1e-5
N_DIRS = 2
POOL_WINDOWS = (2, 4, 8, 16)
POOL_GROUP = D_MODEL // len(POOL_WINDOWS)
D_FF = 4 * D_MODEL

kernel_name = "hybrid_attn_rwkv7_pool_dit"


def _rmsnorm(x, g):
    xf = x.astype(jnp.float32)
    y = xf * lax.rsqrt(jnp.mean(xf * xf, axis=-1, keepdims=True) + EPS)
    return (y * g.astype(jnp.float32)).astype(x.dtype)


def _axial_rope(n_tokens):
    rows = n_tokens // GRID_W
    n_freq = HEAD_DIM // 4
    inv = ROPE_THETA ** (-jnp.arange(n_freq, dtype=jnp.float32) / n_freq)
    ang_r = jnp.arange(rows, dtype=jnp.float32)[:, None] * inv
    ang_c = jnp.arange(GRID_W, dtype=jnp.float32)[:, None] * inv
    ang = jnp.concatenate([
        jnp.broadcast_to(ang_r[:, None, :], (rows, GRID_W, n_freq)),
        jnp.broadcast_to(ang_c[None, :, :], (rows, GRID_W, n_freq))], axis=-1).reshape(rows * GRID_W, 2 * n_freq)
    return jnp.cos(ang), jnp.sin(ang)


def _rope(x, cos, sin):
    half = HEAD_DIM // 2
    xf = x.astype(jnp.float32)
    x1, x2 = xf[..., :half], xf[..., half:]
    c = cos[None, :, None, :]
    s = sin[None, :, None, :]
    return jnp.concatenate([x1 * c - x2 * s, x1 * s + x2 * c], axis=-1).astype(x.dtype)


def _attn_project(h, w_qkv, q_gain, k_gain):
    B, T, _ = h.shape
    qkv = h @ w_qkv
    q = qkv[..., :Q_WIDTH].reshape(B, T, N_HEADS, HEAD_DIM)
    k = qkv[..., Q_WIDTH:Q_WIDTH + KV_WIDTH].reshape(B, T, N_KV_HEADS, HEAD_DIM)
    v = qkv[..., Q_WIDTH + KV_WIDTH:].reshape(B, T, N_KV_HEADS, HEAD_DIM)
    return _rmsnorm(q, q_gain), _rmsnorm(k, k_gain), v


def attention_mixer(h_lat, h_ctx, w_qkv, q_gain, k_gain, w_o, with_ctx_out):
    B, S, _ = h_lat.shape
    L = h_ctx.shape[1]
    scale = HEAD_DIM ** -0.5
    cos, sin = _axial_rope(S)
    q_l, k_l, v_l = _attn_project(h_lat, w_qkv, q_gain, k_gain)
    q_l = _rope(q_l, cos, sin)
    k_l = _rope(k_l, cos, sin)
    q_c, k_c, v_c = _attn_project(h_ctx, w_qkv, q_gain, k_gain)
    n_blk = S // Q_BLOCK
    q_blocks = jnp.moveaxis(q_l.reshape(B, n_blk, Q_BLOCK, N_KV_HEADS, GQA_REP, HEAD_DIM), 1, 0)

    def attend_block(qb):
        s = jnp.concatenate([
            jnp.einsum("bqgrd,bsgd->bgrqs", qb, k_l),
            jnp.einsum("bqgrd,bcgd->bgrqc", qb, k_c)], axis=-1).astype(jnp.float32) * scale
        p = jax.nn.softmax(s, axis=-1).astype(v_l.dtype)
        return (jnp.einsum("bgrqs,bsgd->bqgrd", p[..., :S], v_l)
                + jnp.einsum("bgrqc,bcgd->bqgrd", p[..., S:], v_c))

    o = lax.map(attend_block, q_blocks)
    out_l = jnp.moveaxis(o, 0, 1).reshape(B, S, Q_WIDTH) @ w_o
    out_c = None
    if with_ctx_out:
        qc = q_c.reshape(B, L, N_KV_HEADS, GQA_REP, HEAD_DIM)
        s = jnp.einsum("bqgrd,bcgd->bgrqc", qc, k_c).astype(jnp.float32) * scale
        p = jax.nn.softmax(s, axis=-1).astype(v_c.dtype)
        out_c = jnp.einsum("bgrqc,bcgd->bqgrd", p, v_c).reshape(B, L, Q_WIDTH) @ w_o
    return out_l, out_c


def _centred_shift(h):
    zero = jnp.zeros_like(h[:, :1])
    prev = jnp.concatenate([zero, h[:, :-1]], axis=1)
    nxt = jnp.concatenate([h[:, 1:], zero], axis=1)
    return 0.5 * (prev + nxt) - h


def _heads(t):
    return t.reshape(t.shape[0], t.shape[1], RWKV_HEADS, RWKV_HEAD)


def _rwkv_prepare(h, mu, w_rkv, w0, w1, w2, a0, a1, a2, g1, g2, k_k, k_a):
    xx = _centred_shift(h)
    xr, xw, xk, xv, xa, xg = [h + xx * mu[m] for m in range(6)]
    r = xr @ w_rkv[0]
    k = xk @ w_rkv[1]
    v = xv @ w_rkv[2]
    g = jax.nn.sigmoid(xg @ g1) @ g2
    kk = _heads((k * k_k).astype(jnp.float32))
    kk = kk / jnp.maximum(jnp.sqrt(jnp.sum(kk * kk, axis=-1, keepdims=True)), 1e-12)
    decays, keys, iclrs = [], [], []
    for d in range(N_DIRS):
        w_log = -jax.nn.softplus(-(w0[d] + jnp.tanh(xw @ w1[d]) @ w2[d])) - 0.5
        decays.append(_heads(jnp.exp(-jnp.exp(w_log.astype(jnp.float32)))))
        a = jax.nn.sigmoid(a0[d] + (xa @ a1[d]) @ a2[d])
        keys.append(_heads(k * (1.0 + (a - 1.0) * k_a)))
        iclrs.append(_heads(a))
    return _heads(r), _heads(v), kk, g, decays, keys, iclrs


def _wkv_scan(state0, r, decay, k, v, kk, a, reverse):
    xs = tuple(jnp.moveaxis(t.astype(jnp.float32), 1, 0) for t in (r, decay, k, v, kk, a))

    def step(state, inp):
        r_t, w_t, k_t, v_t, kk_t, a_t = inp
        sa = jnp.einsum("bhvk,bhk->bhv", state, kk_t)
        state = (state * w_t[:, :, None, :]
                 - sa[..., None] * (kk_t * a_t)[:, :, None, :]
                 + v_t[..., None] * k_t[:, :, None, :])
        return state, jnp.einsum("bhvk,bhk->bhv", state, r_t)

    state, ys = lax.scan(step, state0, xs, reverse=reverse)
    return state, jnp.moveaxis(ys, 0, 1)


def _rwkv_finish(y, r, v, keys, r_k, g, ln_g, ln_b, w_o):
    B, T = y.shape[:2]
    mean = jnp.mean(y, axis=-1, keepdims=True)
    var = jnp.mean(jnp.square(y - mean), axis=-1, keepdims=True)
    y = ((y - mean) * lax.rsqrt(var + GN_EPS)).reshape(B, T, D_MODEL)
    y = y * ln_g.astype(jnp.float32) + ln_b.astype(jnp.float32)
    k_sum = (keys[0] + keys[1]).astype(jnp.float32)
    rk = r_k.reshape(RWKV_HEADS, RWKV_HEAD).astype(jnp.float32)
    bonus = jnp.sum(r.astype(jnp.float32) * k_sum * rk, axis=-1, keepdims=True) * v.astype(jnp.float32)
    y = y + bonus.reshape(B, T, D_MODEL)
    return (y.astype(g.dtype) * g) @ w_o


def rwkv_mixer(h_lat, h_ctx, mu, w_rkv, w0, w1, w2, a0, a1, a2, g1, g2, k_k, k_a, r_k, ln_g, ln_b, w_o,
               with_ctx_out):
    prep = (mu, w_rkv, w0, w1, w2, a0, a1, a2, g1, g2, k_k, k_a)
    r_l, v_l, kk_l, g_l, dec_l, key_l, a_l = _rwkv_prepare(h_lat, *prep)
    r_c, v_c, kk_c, g_c, dec_c, key_c, a_c = _rwkv_prepare(h_ctx, *prep)
    B = h_lat.shape[0]
    zero = jnp.zeros((B, RWKV_HEADS, RWKV_HEAD, RWKV_HEAD), jnp.float32)
    ys_l, ys_c = [], []
    for d in range(N_DIRS):
        rev = d == 1
        s_ctx, yc = _wkv_scan(zero, r_c, dec_c[d], key_c[d], v_c, kk_c, a_c[d], rev)
        _, yl = _wkv_scan(s_ctx, r_l, dec_l[d], key_l[d], v_l, kk_l, a_l[d], rev)
        ys_l.append(yl)
        ys_c.append(yc)
    out_l = _rwkv_finish(ys_l[0] + ys_l[1], r_l, v_l, key_l, r_k, g_l, ln_g, ln_b, w_o)
    out_c = None
    if with_ctx_out:
        out_c = _rwkv_finish(ys_c[0] + ys_c[1], r_c, v_c, key_c, r_k, g_c, ln_g, ln_b, w_o)
    return out_l, out_c


def pool_mixer(h, w_group, scale):
    B, T, _ = h.shape
    hf = h.astype(jnp.float32)
    cs = jnp.concatenate([jnp.zeros((B, 1, D_MODEL), jnp.float32), jnp.cumsum(hf, axis=1)], axis=1)
    t = jnp.arange(T)
    outs = []
    for gi, win in enumerate(POOL_WINDOWS):
        sl = slice(gi * POOL_GROUP, (gi + 1) * POOL_GROUP)
        lo = jnp.clip(t - win // 2, 0, T)
        hi = jnp.clip(t + win // 2, 0, T)
        cs_g = cs[..., sl]
        sums = jnp.take(cs_g, hi, axis=1) - jnp.take(cs_g, lo, axis=1)
        cnt = (hi - lo).astype(jnp.float32)[None, :, None]
        pooled = (sums / cnt - hf[..., sl]).astype(h.dtype)
        outs.append(jnp.einsum("btc,cd->btd", pooled, w_group[gi]))
    return jnp.concatenate(outs, axis=-1) * scale


def _sq_relu_mlp(h, w_in, w_out):
    u = jax.nn.relu(h @ w_in)
    return (u * u) @ w_out


def setup_inputs(seed: int = 0) -> dict:
    key = jax.random.key(seed)
    ks = iter(jax.random.split(key, 40))

    def nrm(shape, s):
        return jax.random.normal(next(ks), shape, jnp.float32) * s

    def unif(shape, lo, hi):
        return jax.random.uniform(next(ks), shape, jnp.float32, lo, hi)

    D = D_MODEL
    return {
        "x": nrm((BATCH, SEQ, D), 1.0),
        "c": nrm((BATCH, D), 1.0),
        "ctx": nrm((BATCH, CTX_LEN, D), 1.0),
        "c_ctx": nrm((D,), 1.0),
        "w_mod": nrm((DEPTH, D, N_MOD * D), 0.5 * D ** -0.5),
        "b_mod": nrm((DEPTH, N_MOD * D), 0.02),
        "norm1_g": 1.0 + nrm((DEPTH, D), 0.02),
        "norm2_g": 1.0 + nrm((DEPTH, D), 0.02),
        "mlp_w_in": nrm((DEPTH, D, D_FF), D ** -0.5),
        "mlp_w_out": nrm((DEPTH, D_FF, D), D_FF ** -0.5),
        "attn_w_qkv": nrm((N_ATTN_LAYERS, D, QKV_WIDTH), D ** -0.5),
        "attn_q_gain": 1.0 + nrm((N_ATTN_LAYERS, HEAD_DIM), 0.02),
        "attn_k_gain": 1.0 + nrm((N_ATTN_LAYERS, HEAD_DIM), 0.02),
        "attn_w_o": nrm((N_ATTN_LAYERS, Q_WIDTH, D), Q_WIDTH ** -0.5),
        "rwkv_mu": unif((N_RWKV_LAYERS, 6, D), 0.0, 1.0),
        "rwkv_w_rkv": nrm((N_RWKV_LAYERS, 3, D, D), D ** -0.5),
        "rwkv_w0": unif((N_RWKV_LAYERS, N_DIRS, D), -5.0, 0.0),
        "rwkv_w1": nrm((N_RWKV_LAYERS, N_DIRS, D, DECAY_LORA), D ** -0.5),
        "rwkv_w2": nrm((N_RWKV_LAYERS, N_DIRS, DECAY_LORA, D), 0.1 * DECAY_LORA ** -0.5),
        "rwkv_a0": nrm((N_RWKV_LAYERS, N_DIRS, D), 0.5),
        "rwkv_a1": nrm((N_RWKV_LAYERS, N_DIRS, D, ICLR_LORA), D ** -0.5),
        "rwkv_a2": nrm((N_RWKV_LAYERS, N_DIRS, ICLR_LORA, D), 0.5 * ICLR_LORA ** -0.5),
        "rwkv_g1": nrm((N_RWKV_LAYERS, D, GATE_LORA), D ** -0.5),
        "rwkv_g2": nrm((N_RWKV_LAYERS, GATE_LORA, D), GATE_LORA ** -0.5),
        "rwkv_k_k": 0.85 + nrm((N_RWKV_LAYERS, D), 0.05),
        "rwkv_k_a": 1.0 + nrm((N_RWKV_LAYERS, D), 0.05),
        "rwkv_r_k": nrm((N_RWKV_LAYERS, D), 0.1),
        "rwkv_ln_g": 1.0 + nrm((N_RWKV_LAYERS, D), 0.02),
        "rwkv_ln_b": nrm((N_RWKV_LAYERS, D), 0.02),
        "rwkv_w_o": nrm((N_RWKV_LAYERS, D, D), D ** -0.5),
        "pool_w": nrm((N_POOL_LAYERS, len(POOL_WINDOWS), POOL_GROUP, POOL_GROUP), POOL_GROUP ** -0.5),
        "pool_scale": unif((N_POOL_LAYERS, D), 0.5, 1.5),
    }


def reference(x, c, ctx, c_ctx, w_mod, b_mod, norm1_g, norm2_g, mlp_w_in, mlp_w_out,
              attn_w_qkv, attn_q_gain, attn_k_gain, attn_w_o,
              rwkv_mu, rwkv_w_rkv, rwkv_w0, rwkv_w1, rwkv_w2, rwkv_a0, rwkv_a1, rwkv_a2,
              rwkv_g1, rwkv_g2, rwkv_k_k, rwkv_k_a, rwkv_r_k, rwkv_ln_g, rwkv_ln_b, rwkv_w_o,
              pool_w, pool_scale):
    B = x.shape[0]
    silu_c = jax.nn.silu(c)
    silu_cc = jax.nn.silu(c_ctx)
    for i in range(DEPTH):
        last = i == DEPTH - 1
        j = i // N_MIXERS
        mod_l = (silu_c @ w_mod[i] + b_mod[i]).reshape(B, N_MOD, 1, D_MODEL)
        mod_c = (silu_cc @ w_mod[i] + b_mod[i]).reshape(N_MOD, 1, 1, D_MODEL)
        h_l = _rmsnorm(x, norm1_g[i]) * (1.0 + mod_l[:, 1]) + mod_l[:, 0]
        h_c = _rmsnorm(ctx, norm1_g[i]) * (1.0 + mod_c[1]) + mod_c[0]
        kind = i % N_MIXERS
        if kind == 0:
            y_l, y_c = attention_mixer(h_l, h_c, attn_w_qkv[j], attn_q_gain[j], attn_k_gain[j], attn_w_o[j],
                                       not last)
        elif kind == 1:
            y_l, y_c = rwkv_mixer(h_l, h_c, rwkv_mu[j], rwkv_w_rkv[j], rwkv_w0[j], rwkv_w1[j], rwkv_w2[j],
                                  rwkv_a0[j], rwkv_a1[j], rwkv_a2[j], rwkv_g1[j], rwkv_g2[j], rwkv_k_k[j],
                                  rwkv_k_a[j], rwkv_r_k[j], rwkv_ln_g[j], rwkv_ln_b[j], rwkv_w_o[j], not last)
        else:
            y_l = pool_mixer(h_l, pool_w[j], pool_scale[j])
            y_c = None if last else pool_mixer(h_c, pool_w[j], pool_scale[j])
        x = x + mod_l[:, 2] * y_l
        h2 = _rmsnorm(x, norm2_g[i]) * (1.0 + mod_l[:, 4]) + mod_l[:, 3]
        x = x + mod_l[:, 5] * _sq_relu_mlp(h2, mlp_w_in[i], mlp_w_out[i])
        if not last:
            ctx = ctx + mod_c[2] * y_c
            h2c = _rmsnorm(ctx, norm2_g[i]) * (1.0 + mod_c[4]) + mod_c[3]
            ctx = ctx + mod_c[5] * _sq_relu_mlp(h2c, mlp_w_in[i], mlp_w_out[i])
    return x
```

```python
import functools

import jax
import jax.numpy as jnp
import numpy as np
from jax import lax
from jax.experimental import pallas as pl
from jax.experimental.pallas import tpu as pltpu

F32 = jnp.float32
BF16 = jnp.bfloat16

D_MODEL = 1024
GRID_W = 64
N_MIXERS = 3
N_MOD = 6
EPS = 1e-6
N_HEADS = 16
N_KV_HEADS = 4
HEAD_DIM = 64
GQA_REP = N_HEADS // N_KV_HEADS
Q_WIDTH = N_HEADS * HEAD_DIM
KV_WIDTH = N_KV_HEADS * HEAD_DIM
QKV_WIDTH = Q_WIDTH + 2 * KV_WIDTH
ROPE_THETA = 10000.0
RWKV_HEAD = 64
DECAY_LORA = 64
ICLR_LORA = 64
GATE_LORA = 160
GN_EPS = RWKV_HEAD * 1e-5
POOL_WINDOWS = (2, 4, 8, 16)
POOL_GROUP = D_MODEL // len(POOL_WINDOWS)
D_FF = 4 * D_MODEL

LANES = 128
SUBLANES = 8
PAIRS = D_MODEL // LANES
CHUNK = 64
INV_BLOCK = 16
VMEM_LIMIT = 56 * 1024 * 1024

HIGHEST = lax.Precision.HIGHEST


def _params(*sem):
    return pltpu.CompilerParams(dimension_semantics=sem, vmem_limit_bytes=VMEM_LIMIT)


def _const_spec(shape):
    zeros = (0,) * len(shape)
    return pl.BlockSpec(shape, lambda *_: zeros, pipeline_mode=pl.Buffered(1))


def _row_spec(tm, width=D_MODEL):
    return pl.BlockSpec((1, tm, width), lambda b, i: (b, i, 0))


def _mod_spec():
    return pl.BlockSpec((1, N_MOD, D_MODEL), lambda b, i: (b, 0, 0))


def _norm_mod(x, g, mod, k):
    y = x * lax.rsqrt(jnp.mean(x * x, axis=-1, keepdims=True) + EPS) * g
    return y * (1.0 + mod[k + 1:k + 2]) + mod[k:k + 1]


def _seg_sum(x):
    lo = lax.broadcasted_iota(jnp.int32, (1, LANES), 1) < HEAD_DIM
    s_lo = jnp.sum(jnp.where(lo, x, 0.0), axis=-1, keepdims=True)
    s_hi = jnp.sum(jnp.where(lo, 0.0, x), axis=-1, keepdims=True)
    return jnp.where(lo, s_lo, s_hi)


def _seg_sum_wide(x):
    return jnp.concatenate([_seg_sum(x[:, p * LANES:(p + 1) * LANES]) for p in range(x.shape[1] // LANES)], axis=1)


def _mod_kernel(c_ref, w_ref, b_ref, o_ref):
    c = c_ref[...]
    s = c * jax.nn.sigmoid(c)
    o_ref[0] = jnp.dot(s, w_ref[0], precision=HIGHEST, preferred_element_type=F32) + b_ref[0]


def _modulation(cvec, w_mod, b_mod):
    depth = w_mod.shape[0]
    rows = cvec.shape[0]
    tn = 1536
    return pl.pallas_call(
        _mod_kernel,
        grid=(depth, N_MOD * D_MODEL // tn),
        in_specs=[pl.BlockSpec((rows, D_MODEL), lambda l, j: (0, 0)),
                  pl.BlockSpec((1, D_MODEL, tn), lambda l, j: (l, 0, j)),
                  pl.BlockSpec((1, 1, tn), lambda l, j: (l, 0, j))],
        out_specs=pl.BlockSpec((1, rows, tn), lambda l, j: (l, 0, j)),
        out_shape=jax.ShapeDtypeStruct((depth, rows, N_MOD * D_MODEL), F32),
        compiler_params=_params("parallel", "parallel"),
    )(cvec, w_mod, b_mod.reshape(depth, 1, N_MOD * D_MODEL))


def _qkv_kernel(x_ref, g_ref, mod_ref, w_ref, qg_ref, kg_ref, cos_ref, sin_ref, q_ref, k_ref, v_ref, *, use_rope):
    h = _norm_mod(x_ref[0], g_ref[...], mod_ref[0], 0).astype(BF16)
    acc = jnp.dot(h, w_ref[...], preferred_element_type=F32)
    lane = lax.broadcasted_iota(jnp.int32, (1, LANES), 1)
    first_half = (lane % HEAD_DIM) < (HEAD_DIM // 2)

    def head_pair(xp, gain, scale):
        y = xp * lax.rsqrt(_seg_sum(xp * xp) * (1.0 / HEAD_DIM) + EPS) * gain
        if use_rope:
            rot = jnp.where(first_half, pltpu.roll(y, LANES - HEAD_DIM // 2, 1), pltpu.roll(y, HEAD_DIM // 2, 1))
            y = y * cos_ref[...] + rot * sin_ref[...]
        return (y * scale).astype(BF16)

    for p in range(Q_WIDTH // LANES):
        y = head_pair(acc[:, p * LANES:(p + 1) * LANES], qg_ref[...], HEAD_DIM ** -0.5)
        q_ref[0, 2 * p] = y[:, :HEAD_DIM]
        q_ref[0, 2 * p + 1] = y[:, HEAD_DIM:]
    for p in range(KV_WIDTH // LANES):
        y = head_pair(acc[:, Q_WIDTH + p * LANES:Q_WIDTH + (p + 1) * LANES], kg_ref[...], 1.0)
        k_ref[0, 2 * p] = y[:, :HEAD_DIM]
        k_ref[0, 2 * p + 1] = y[:, HEAD_DIM:]
    for hh in range(N_KV_HEADS):
        c0 = Q_WIDTH + KV_WIDTH + hh * HEAD_DIM
        v_ref[0, hh] = acc[:, c0:c0 + HEAD_DIM].astype(BF16)


def _qkv_project(x, g, mod, w_bf, qg, kg, cos_t, sin_t, use_rope):
    B, T, _ = x.shape
    tm = min(T, 512)
    tab_spec = pl.BlockSpec((tm, LANES), lambda b, i: (i, 0))
    head_spec = lambda n: pl.BlockSpec((1, n, tm, HEAD_DIM), lambda b, i: (b, 0, i, 0))
    return pl.pallas_call(
        functools.partial(_qkv_kernel, use_rope=use_rope),
        grid=(B, T // tm),
        in_specs=[_row_spec(tm), _const_spec((1, D_MODEL)), _mod_spec(), _const_spec((D_MODEL, QKV_WIDTH)),
                  _const_spec((1, LANES)), _const_spec((1, LANES)), tab_spec, tab_spec],
        out_specs=[head_spec(N_HEADS), head_spec(N_KV_HEADS), head_spec(N_KV_HEADS)],
        out_shape=[jax.ShapeDtypeStruct((B, N_HEADS, T, HEAD_DIM), BF16),
                   jax.ShapeDtypeStruct((B, N_KV_HEADS, T, HEAD_DIM), BF16),
                   jax.ShapeDtypeStruct((B, N_KV_HEADS, T, HEAD_DIM), BF16)],
        compiler_params=_params("parallel", "parallel"),
    )(x, g, mod, w_bf, qg, kg, cos_t, sin_t)


def _attn_kernel(*refs, n_src, tq, kv_chunk):
    q_ref = refs[0]
    kv_refs = refs[1:1 + 2 * n_src]
    o_ref = refs[1 + 2 * n_src]
    s_scr = refs[2 + 2 * n_src]
    q = q_ref[0].reshape(GQA_REP * tq, HEAD_DIM)
    pieces = []
    col = 0
    for s in range(n_src):
        rows_total = kv_refs[2 * s].shape[2]
        ck = min(kv_chunk, rows_total)
        for r0 in range(0, rows_total, ck):
            pieces.append((kv_refs[2 * s], kv_refs[2 * s + 1], r0, ck, col))
            col += ck
    m = None
    for k_ref, _, r0, ck, c0 in pieces:
        s_blk = lax.dot_general(q, k_ref[0, 0, r0:r0 + ck, :], (((1,), (1,)), ((), ())), preferred_element_type=F32)
        s_scr[:, c0:c0 + ck] = s_blk
        bm = jnp.max(s_blk, axis=-1, keepdims=True)
        m = bm if m is None else jnp.maximum(m, bm)
    l = jnp.zeros((GQA_REP * tq, 1), F32)
    acc = jnp.zeros((GQA_REP * tq, HEAD_DIM), F32)
    for _, v_ref, r0, ck, c0 in pieces:
        p = jnp.exp(s_scr[:, c0:c0 + ck] - m)
        l = l + jnp.sum(p, axis=-1, keepdims=True)
        acc = acc + jnp.dot(p.astype(BF16), v_ref[0, 0, r0:r0 + ck, :], preferred_element_type=F32)
    o = acc / l
    o_ref[0] = jnp.concatenate([o[r * tq:(r + 1) * tq] for r in range(GQA_REP)], axis=1).astype(BF16)


def _attention(q, kv_sources):
    B, _, T, _ = q.shape
    tq = min(T, 128)
    total = sum(k.shape[2] for k, _ in kv_sources)
    in_specs = [pl.BlockSpec((1, GQA_REP, tq, HEAD_DIM), lambda b, g, i: (b, g, i, 0))]
    args = [q]
    for k, v in kv_sources:
        spec = pl.BlockSpec((1, 1, k.shape[2], HEAD_DIM), lambda b, g, i: (b, g, 0, 0))
        in_specs += [spec, spec]
        args += [k, v]
    return pl.pallas_call(
        functools.partial(_attn_kernel, n_src=len(kv_sources), tq=tq, kv_chunk=1024),
        grid=(B, N_KV_HEADS, T // tq),
        in_specs=in_specs,
        out_specs=pl.BlockSpec((1, tq, GQA_REP * HEAD_DIM), lambda b, g, i: (b, i, g)),
        out_shape=jax.ShapeDtypeStruct((B, T, Q_WIDTH), BF16),
        scratch_shapes=[pltpu.VMEM((GQA_REP * tq, total), F32)],
        compiler_params=_params("parallel", "parallel", "arbitrary"),
    )(*args)


def _proj_residual_kernel(x_ref, y_ref, mod_ref, w_ref, o_ref):
    out = jnp.dot(y_ref[0], w_ref[...], preferred_element_type=F32)
    o_ref[0] = x_ref[0] + mod_ref[0][2:3] * out


def _proj_residual(x, y, mod, w_bf):
    B, T, _ = x.shape
    tm = min(T, 512)
    return pl.pallas_call(
        _proj_residual_kernel,
        grid=(B, T // tm),
        in_specs=[_row_spec(tm), _row_spec(tm), _mod_spec(), _const_spec((D_MODEL, D_MODEL))],
        out_specs=_row_spec(tm),
        out_shape=jax.ShapeDtypeStruct(x.shape, F32),
        compiler_params=_params("parallel", "parallel"),
    )(x, y, mod, w_bf)


def _mlp_kernel(x_ref, g_ref, mod_ref, win_ref, wout_ref, o_ref, *, ff_chunk):
    x = x_ref[0]
    mod = mod_ref[0]
    h = _norm_mod(x, g_ref[...], mod, 3).astype(BF16)
    acc = jnp.zeros(x.shape, F32)
    for f0 in range(0, D_FF, ff_chunk):
        u = jnp.maximum(jnp.dot(h, win_ref[:, f0:f0 + ff_chunk], preferred_element_type=F32), 0.0)
        acc = acc + jnp.dot((u * u).astype(BF16), wout_ref[f0:f0 + ff_chunk, :], preferred_element_type=F32)
    o_ref[0] = x + mod[5:6] * acc


def _mlp(x, g, mod, win_bf, wout_bf):
    B, T, _ = x.shape
    tm = min(T, 512)
    return pl.pallas_call(
        functools.partial(_mlp_kernel, ff_chunk=1024),
        grid=(B, T // tm),
        in_specs=[_row_spec(tm), _const_spec((1, D_MODEL)), _mod_spec(),
                  _const_spec((D_MODEL, D_FF)), _const_spec((D_FF, D_MODEL))],
        out_specs=_row_spec(tm),
        out_shape=jax.ShapeDtypeStruct(x.shape, F32),
        compiler_params=_params("parallel", "parallel"),
    )(x, g, mod, win_bf, wout_bf)


DECAY_SCALE = float(np.exp(-0.5))


def _rwkv_prep_kernel(x_ref, xp_ref, xn_ref, g_ref, mod_ref, mu_ref, wrkv_ref, w1_ref, w2_ref, w0_ref, a1_ref, a2_ref,
                      a0_ref, g1_ref, g2_ref, kk_ref, ka_ref,
                      r_o, v_o, kk_o, g_o, lw0_o, lw1_o, key0_o, key1_o, a0_o, a1_o, *, tm):
    i = pl.program_id(1)
    last = pl.num_programs(1) - 1
    g = g_ref[...]
    mod = mod_ref[0]
    h = _norm_mod(x_ref[0], g, mod, 0)
    hp = _norm_mod(xp_ref[0][SUBLANES - 1:SUBLANES], g, mod, 0) * jnp.where(i == 0, 0.0, 1.0)
    hn = _norm_mod(xn_ref[0][0:1], g, mod, 0) * jnp.where(i == last, 0.0, 1.0)
    row = lax.broadcasted_iota(jnp.int32, (tm, 1), 0)
    prev = jnp.where(row == 0, hp, pltpu.roll(h, 1, 0))
    nxt = jnp.where(row == tm - 1, hn, pltpu.roll(h, tm - 1, 0))
    xx = 0.5 * (prev + nxt) - h
    mix = lambda m: (h + xx * mu_ref[m:m + 1]).astype(BF16)
    dot = functools.partial(jnp.dot, preferred_element_type=F32)
    r = dot(mix(0), wrkv_ref[0])
    k = dot(mix(2), wrkv_ref[1])
    v = dot(mix(3), wrkv_ref[2])
    r_o[0] = r
    v_o[0] = v
    g_o[0] = dot(jax.nn.sigmoid(dot(mix(5), g1_ref[...])).astype(BF16), g2_ref[...])
    kk = k * kk_ref[...]
    kk_o[0] = kk / jnp.maximum(jnp.sqrt(_seg_sum_wide(kk * kk)), 1e-12)
    tw = jnp.tanh(dot(mix(1), w1_ref[...])).astype(BF16)
    ta = dot(mix(4), a1_ref[...]).astype(BF16)
    for d, (lw_o, key_o, a_o) in enumerate(((lw0_o, key0_o, a0_o), (lw1_o, key1_o, a1_o))):
        z = w0_ref[d:d + 1] + dot(tw[:, d * DECAY_LORA:(d + 1) * DECAY_LORA], w2_ref[d])
        lw_o[0] = -DECAY_SCALE * jax.nn.sigmoid(z)
        a = jax.nn.sigmoid(a0_ref[d:d + 1] + dot(ta[:, d * ICLR_LORA:(d + 1) * ICLR_LORA], a2_ref[d]))
        a_o[0] = a
        key_o[0] = k * (1.0 + (a - 1.0) * ka_ref[...])


def _rwkv_prepare(x, g, mod, p):
    B, T, _ = x.shape
    tm = min(T, 256)
    nb = tm // SUBLANES
    n_blk8 = T // SUBLANES
    prev_spec = pl.BlockSpec((1, SUBLANES, D_MODEL), lambda b, i: (b, jnp.maximum(i * nb - 1, 0), 0))
    next_spec = pl.BlockSpec((1, SUBLANES, D_MODEL), lambda b, i: (b, jnp.minimum((i + 1) * nb, n_blk8 - 1), 0))
    out = jax.ShapeDtypeStruct(x.shape, F32)
    return pl.pallas_call(
        functools.partial(_rwkv_prep_kernel, tm=tm),
        grid=(B, T // tm),
        in_specs=[_row_spec(tm), prev_spec, next_spec, _const_spec((1, D_MODEL)), _mod_spec(),
                  _const_spec((6, D_MODEL)), _const_spec((3, D_MODEL, D_MODEL)),
                  _const_spec((D_MODEL, 2 * DECAY_LORA)), _const_spec((2, DECAY_LORA, D_MODEL)),
                  _const_spec((2, D_MODEL)),
                  _const_spec((D_MODEL, 2 * ICLR_LORA)), _const_spec((2, ICLR_LORA, D_MODEL)),
                  _const_spec((2, D_MODEL)),
                  _const_spec((D_MODEL, GATE_LORA)), _const_spec((GATE_LORA, D_MODEL)),
                  _const_spec((1, D_MODEL)), _const_spec((1, D_MODEL))],
        out_specs=[_row_spec(tm)] * 10,
        out_shape=[out] * 10,
        compiler_params=_params("parallel", "parallel"),
    )(x, x, x, g, mod, p["mu"], p["w_rkv"], p["w1"], p["w2"], p["w0"], p["a1"], p["a2"], p["a0"], p["g1"], p["g2"],
      p["k_k"], p["k_a"])


def _wkv_kernel(r_ref, lw_ref, k_ref, v_ref, kk_ref, a_ref, s0_ref, y_ref, s_ref, state, *, n_chunks, reverse):
    C = CHUNK
    S2 = 2 * C
    hp = functools.partial(jnp.dot, precision=HIGHEST, preferred_element_type=F32)
    nt = lambda a, b: lax.dot_general(a, b, (((1,), (1,)), ((), ())), precision=HIGHEST, preferred_element_type=F32)
    tn = lambda a, b: lax.dot_general(a, b, (((0,), (0,)), ((), ())), precision=HIGHEST, preferred_element_type=F32)

    @pl.when(pl.program_id(2) == 0)
    def _():
        state[...] = s0_ref[0, 0]

    ti = lax.broadcasted_iota(jnp.int32, (C, C), 0)
    tj = lax.broadcasted_iota(jnp.int32, (C, C), 1)
    tri = jnp.where((ti <= tj) if reverse else (ti >= tj), 1.0, 0.0)
    si = lax.broadcasted_iota(jnp.int32, (S2, S2), 0)
    sj = lax.broadcasted_iota(jnp.int32, (S2, S2), 1)
    strict = (si < sj) if reverse else (si > sj)
    incl = (si <= sj) if reverse else (si >= sj)
    blk = (si // INV_BLOCK) == (sj // INV_BLOCK)
    eye = si == sj
    ident = jnp.where(eye, 1.0, 0.0)
    head0 = lax.broadcasted_iota(jnp.int32, (1, LANES), 1) < RWKV_HEAD
    stack = lambda z: jnp.concatenate([jnp.where(head0, z, 0.0), jnp.where(head0, 0.0, z)], axis=0)
    unstack = lambda z: z[:C] + z[C:]

    pre = []
    for c in range(n_chunks):
        rows = slice(c * C, (c + 1) * C)
        r, lw, k, v, kk, a = (ref[0, rows, :] for ref in (r_ref, lw_ref, k_ref, v_ref, kk_ref, a_ref))
        cum = hp(tri, lw)
        total = cum[0:1] if reverse else cum[C - 1:C]
        g_in = jnp.exp(-cum)
        g_to_end = jnp.exp(total - cum)
        b = kk * a
        a_s = stack(kk * jnp.exp(cum - lw))
        b_s = stack(b * g_in)
        k_s = stack(k * g_in)
        r_s = stack(r * jnp.exp(cum))
        v_s = stack(v)
        big = nt(jnp.concatenate([a_s, r_s], axis=0), jnp.concatenate([b_s, k_s], axis=0))
        l_ab = jnp.where(strict, big[:S2, :S2], 0.0)
        l_ak = jnp.where(strict, big[:S2, S2:], 0.0)
        p_rb = jnp.where(incl, big[S2:, :S2], 0.0)
        p_rk = jnp.where(incl, big[S2:, S2:], 0.0)
        l_d = jnp.where(blk, l_ab, 0.0)
        l_o = l_ab - l_d
        l2 = hp(l_d, l_d)
        l4 = hp(l2, l2)
        l8 = hp(l4, l4)
        p1 = ident - l_d + l2 - hp(l_d, l2)
        p2 = p1 + hp(p1, l4)
        t_d = p2 + hp(p2, l8)
        m1 = hp(t_d, l_o)
        m2 = hp(m1, m1)
        t_inv = hp(ident - m1 + m2 - hp(m1, m2), t_d)
        w = hp(t_inv, jnp.concatenate([a_s, hp(l_ak, v_s)], axis=1))
        pw = hp(p_rb, w)
        q1 = unstack(r_s - pw[:, :S2])
        y2 = unstack(hp(p_rk, v_s) - pw[:, S2:])
        bw = tn(stack(b * g_to_end), w)
        g_mat = jnp.where(eye, jnp.exp(total), 0.0) - bw[:, :S2]
        h_mat = tn(stack(k * g_to_end), v_s) - bw[:, S2:]
        pre.append((q1, y2, g_mat, h_mat))

    s = state[...]
    order = range(n_chunks - 1, -1, -1) if reverse else range(n_chunks)
    for c in order:
        q1, y2, g_mat, h_mat = pre[c]
        y_ref[0, c * C:(c + 1) * C, :] = hp(q1, s) + y2
        s = hp(g_mat, s) + h_mat
    state[...] = s

    @pl.when(pl.program_id(2) == pl.num_programs(2) - 1)
    def _():
        s_ref[0, 0] = s


def _wkv(r, lw, k, v, kk, a, s0, reverse):
    B, T, _ = r.shape
    rows = min(T, 4 * CHUNK)
    n_steps = T // rows
    tmap = (lambda b, p, j: (b, n_steps - 1 - j, p)) if reverse else (lambda b, p, j: (b, j, p))
    seq_spec = pl.BlockSpec((1, rows, LANES), tmap)
    st_spec = pl.BlockSpec((1, 1, LANES, LANES), lambda b, p, j: (b, p, 0, 0))
    return pl.pallas_call(
        functools.partial(_wkv_kernel, n_chunks=rows // CHUNK, reverse=reverse),
        grid=(B, PAIRS, n_steps),
        in_specs=[seq_spec] * 6 + [st_spec],
        out_specs=[seq_spec, st_spec],
        out_shape=[jax.ShapeDtypeStruct(r.shape, F32), jax.ShapeDtypeStruct((B, PAIRS, LANES, LANES), F32)],
        scratch_shapes=[pltpu.VMEM((LANES, LANES), F32)],
        compiler_params=_params("parallel", "parallel", "arbitrary"),
    )(r, lw, k, v, kk, a, s0)


def _rwkv_finish_kernel(x_ref, y0_ref, y1_ref, r_ref, v_ref, k0_ref, k1_ref, g_ref, mod_ref, rk_ref, lng_ref, lnb_ref,
                        wo_ref, o_ref):
    y = y0_ref[0] + y1_ref[0]
    inv_n = 1.0 / RWKV_HEAD
    mean = _seg_sum_wide(y) * inv_n
    yc = y - mean
    var = _seg_sum_wide(yc * yc) * inv_n
    yn = yc * lax.rsqrt(var + GN_EPS) * lng_ref[...] + lnb_ref[...]
    bonus = _seg_sum_wide(r_ref[0] * (k0_ref[0] + k1_ref[0]) * rk_ref[...]) * v_ref[0]
    out = jnp.dot(((yn + bonus) * g_ref[0]).astype(BF16), wo_ref[...], preferred_element_type=F32)
    o_ref[0] = x_ref[0] + mod_ref[0][2:3] * out


def _rwkv_finish(x, y0, y1, r, v, k0, k1, gate, mod, p):
    B, T, _ = x.shape
    tm = min(T, 256)
    vec = _const_spec((1, D_MODEL))
    return pl.pallas_call(
        _rwkv_finish_kernel,
        grid=(B, T // tm),
        in_specs=[_row_spec(tm)] * 8 + [_mod_spec(), vec, vec, vec, _const_spec((D_MODEL, D_MODEL))],
        out_specs=_row_spec(tm),
        out_shape=jax.ShapeDtypeStruct(x.shape, F32),
        compiler_params=_params("parallel", "parallel"),
    )(x, y0, y1, r, v, k0, k1, gate, mod, p["r_k"], p["ln_g"], p["ln_b"], p["w_o"])


HALO = SUBLANES


def _pool_kernel(x_ref, xp_ref, xn_ref, g_ref, mod_ref, w_ref, sc_ref, o_ref, *, tm, seq_len):
    i = pl.program_id(1)
    last = pl.num_programs(1) - 1
    g = g_ref[...]
    mod = mod_ref[0]
    x = x_ref[0]
    h = _norm_mod(x, g, mod, 0)
    hp = _norm_mod(xp_ref[0], g, mod, 0) * jnp.where(i == 0, 0.0, 1.0)
    hn = _norm_mod(xn_ref[0], g, mod, 0) * jnp.where(i == last, 0.0, 1.0)
    ext = jnp.concatenate([hp, h, hn], axis=0)
    n_ext = tm + 2 * HALO
    t = i * tm + lax.broadcasted_iota(jnp.int32, (tm, 1), 0)
    outs = []
    for gi, win in enumerate(POOL_WINDOWS):
        e = ext[:, gi * POOL_GROUP:(gi + 1) * POOL_GROUP]
        acc = e + pltpu.roll(e, 1, 0)
        step = 1
        while 2 * step < win:
            acc = pltpu.roll(acc, step, 0) + pltpu.roll(acc, n_ext - step, 0)
            step *= 2
        half = win // 2
        cnt = (jnp.minimum(t + half, seq_len) - jnp.maximum(t - half, 0)).astype(F32)
        pooled = acc[HALO:HALO + tm] / cnt - e[HALO:HALO + tm]
        outs.append(jnp.dot(pooled.astype(BF16), w_ref[gi], preferred_element_type=F32))
    y = jnp.concatenate(outs, axis=1) * sc_ref[...]
    o_ref[0] = x + mod[2:3] * y


def _pool_mixer(x, g, mod, w_bf, scale):
    B, T, _ = x.shape
    tm = min(T, 256)
    nb = tm // HALO
    n_blk = T // HALO
    prev_spec = pl.BlockSpec((1, HALO, D_MODEL), lambda b, i: (b, jnp.maximum(i * nb - 1, 0), 0))
    next_spec = pl.BlockSpec((1, HALO, D_MODEL), lambda b, i: (b, jnp.minimum((i + 1) * nb, n_blk - 1), 0))
    n_grp = len(POOL_WINDOWS)
    return pl.pallas_call(
        functools.partial(_pool_kernel, tm=tm, seq_len=T),
        grid=(B, T // tm),
        in_specs=[_row_spec(tm), prev_spec, next_spec, _const_spec((1, D_MODEL)), _mod_spec(),
                  _const_spec((n_grp, POOL_GROUP, POOL_GROUP)), _const_spec((1, D_MODEL))],
        out_specs=_row_spec(tm),
        out_shape=jax.ShapeDtypeStruct(x.shape, F32),
        compiler_params=_params("parallel", "parallel"),
    )(x, x, x, g, mod, w_bf, scale)


def _rope_tables(n_tokens):
    rows = n_tokens // GRID_W
    n_freq = HEAD_DIM // 4
    inv = ROPE_THETA ** (-jnp.arange(n_freq, dtype=F32) / n_freq)
    ang_r = jnp.arange(rows, dtype=F32)[:, None] * inv
    ang_c = jnp.arange(GRID_W, dtype=F32)[:, None] * inv
    ang = jnp.concatenate([
        jnp.broadcast_to(ang_r[:, None, :], (rows, GRID_W, n_freq)),
        jnp.broadcast_to(ang_c[None, :, :], (rows, GRID_W, n_freq))], axis=-1).reshape(rows * GRID_W, 2 * n_freq)
    cos, sin = jnp.cos(ang), jnp.sin(ang)
    return jnp.tile(cos, (1, 4)), jnp.tile(jnp.concatenate([-sin, sin], axis=-1), (1, 2))


def kernel(x, c, ctx, c_ctx, w_mod, b_mod, norm1_g, norm2_g, mlp_w_in, mlp_w_out, attn_w_qkv, attn_q_gain, attn_k_gain, attn_w_o, rwkv_mu, rwkv_w_rkv, rwkv_w0, rwkv_w1, rwkv_w2, rwkv_a0, rwkv_a1, rwkv_a2, rwkv_g1, rwkv_g2, rwkv_k_k, rwkv_k_a, rwkv_r_k, rwkv_ln_g, rwkv_ln_b, rwkv_w_o, pool_w, pool_scale):
    B, S, _ = x.shape
    L = ctx.shape[1]
    depth = w_mod.shape[0]
    assert x.shape[2] == D_MODEL and S % (4 * CHUNK) == 0 and L % CHUNK == 0 and S % GRID_W == 0

    n_rows = -(-(B + 1) // SUBLANES) * SUBLANES
    cvec = jnp.concatenate([c, c_ctx[None], jnp.zeros((n_rows - B - 1, D_MODEL), F32)], axis=0)
    mod_all = _modulation(cvec, w_mod, b_mod).reshape(depth, n_rows, N_MOD, D_MODEL)
    cos_t, sin_t = _rope_tables(S)
    zero_tab = jnp.zeros((L, LANES), F32)
    row = lambda a: a.reshape(1, -1)

    for i in range(depth):
        last = i == depth - 1
        j = i // N_MIXERS
        mod_l = mod_all[i, :B]
        mod_c = jnp.broadcast_to(mod_all[i, B][None], (B, N_MOD, D_MODEL))
        g1 = row(norm1_g[i])
        kind = i % N_MIXERS
        if kind == 0:
            w_qkv = attn_w_qkv[j].astype(BF16)
            w_o = attn_w_o[j].astype(BF16)
            qg = jnp.tile(row(attn_q_gain[j]), (1, 2))
            kg = jnp.tile(row(attn_k_gain[j]), (1, 2))
            q_l, k_l, v_l = _qkv_project(x, g1, mod_l, w_qkv, qg, kg, cos_t, sin_t, True)
            q_c, k_c, v_c = _qkv_project(ctx, g1, mod_c, w_qkv, qg, kg, zero_tab, zero_tab, False)
            o_l = _attention(q_l, [(k_l, v_l), (k_c, v_c)])
            x = _proj_residual(x, o_l, mod_l, w_o)
            if not last:
                o_c = _attention(q_c, [(k_c, v_c)])
                ctx = _proj_residual(ctx, o_c, mod_c, w_o)
        elif kind == 1:
            p = {
                "mu": rwkv_mu[j], "w_rkv": rwkv_w_rkv[j].astype(BF16),
                "w1": jnp.concatenate([rwkv_w1[j, 0], rwkv_w1[j, 1]], axis=1).astype(BF16),
                "w2": rwkv_w2[j].astype(BF16), "w0": rwkv_w0[j],
                "a1": jnp.concatenate([rwkv_a1[j, 0], rwkv_a1[j, 1]], axis=1).astype(BF16),
                "a2": rwkv_a2[j].astype(BF16), "a0": rwkv_a0[j],
                "g1": rwkv_g1[j].astype(BF16), "g2": rwkv_g2[j].astype(BF16),
                "k_k": row(rwkv_k_k[j]), "k_a": row(rwkv_k_a[j]), "r_k": row(rwkv_r_k[j]),
                "ln_g": row(rwkv_ln_g[j]), "ln_b": row(rwkv_ln_b[j]), "w_o": rwkv_w_o[j].astype(BF16),
            }
            r_l, v_l, kk_l, g_l, lw0_l, lw1_l, key0_l, key1_l, a0_l, a1_l = _rwkv_prepare(x, g1, mod_l, p)
            r_c, v_c, kk_c, g_c, lw0_c, lw1_c, key0_c, key1_c, a0_c, a1_c = _rwkv_prepare(ctx, g1, mod_c, p)
            zero_state = jnp.zeros((B, PAIRS, LANES, LANES), F32)
            ys_l, ys_c = [], []
            for rev, (lw_l, key_l, a_l, lw_c, key_c, a_c) in enumerate(
                    ((lw0_l, key0_l, a0_l, lw0_c, key0_c, a0_c), (lw1_l, key1_l, a1_l, lw1_c, key1_c, a1_c))):
                y_c, s_ctx = _wkv(r_c, lw_c, key_c, v_c, kk_c, a_c, zero_state, bool(rev))
                y_l, _ = _wkv(r_l, lw_l, key_l, v_l, kk_l, a_l, s_ctx, bool(rev))
                ys_l.append(y_l)
                ys_c.append(y_c)
            x = _rwkv_finish(x, ys_l[0], ys_l[1], r_l, v_l, key0_l, key1_l, g_l, mod_l, p)
            if not last:
                ctx = _rwkv_finish(ctx, ys_c[0], ys_c[1], r_c, v_c, key0_c, key1_c, g_c, mod_c, p)
        else:
            w_p = pool_w[j].astype(BF16)
            sc = row(pool_scale[j])
            x = _pool_mixer(x, g1, mod_l, w_p, sc)
            if not last:
                ctx = _pool_mixer(ctx, g1, mod_c, w_p, sc)
        g2 = row(norm2_g[i])
        w_in = mlp_w_in[i].astype(BF16)
        w_out = mlp_w_out[i].astype(BF16)
        x = _mlp(x, g2, mod_l, w_in, w_out)
        if not last:
            ctx = _mlp(ctx, g2, mod_c, w_in, w_out)
    return x
```

```python
import functools

import jax
import jax.numpy as jnp
import numpy as np
from jax import lax
from jax.experimental import pallas as pl
from jax.experimental.pallas import tpu as pltpu

F32 = jnp.float32
BF16 = jnp.bfloat16

D_MODEL = 1024
GRID_W = 64
N_MIXERS = 3
N_MOD = 6
EPS = 1e-6
N_HEADS = 16
N_KV_HEADS = 4
HEAD_DIM = 64
GQA_REP = N_HEADS // N_KV_HEADS
Q_WIDTH = N_HEADS * HEAD_DIM
KV_WIDTH = N_KV_HEADS * HEAD_DIM
QKV_WIDTH = Q_WIDTH + 2 * KV_WIDTH
ROPE_THETA = 10000.0
RWKV_HEAD = 64
DECAY_LORA = 64
ICLR_LORA = 64
GATE_LORA = 160
GN_EPS = RWKV_HEAD * 1e-5
POOL_WINDOWS = (2, 4, 8, 16)
POOL_GROUP = D_MODEL // len(POOL_WINDOWS)
D_FF = 4 * D_MODEL

LANES = 128
SUBLANES = 8
PAIRS = D_MODEL // LANES
CHUNK = 64
INV_BLOCK = 16
VMEM_LIMIT = 56 * 1024 * 1024

HIGHEST = lax.Precision.HIGHEST


def _params(*sem):
    return pltpu.CompilerParams(dimension_semantics=sem, vmem_limit_bytes=VMEM_LIMIT)


def _const_spec(shape):
    zeros = (0,) * len(shape)
    return pl.BlockSpec(shape, lambda *_: zeros, pipeline_mode=pl.Buffered(1))


def _row_spec(tm, width=D_MODEL):
    return pl.BlockSpec((1, tm, width), lambda b, i: (b, i, 0))


def _mod_spec():
    return pl.BlockSpec((1, N_MOD, D_MODEL), lambda b, i: (b, 0, 0))


def _norm_mod(x, g, mod, k):
    y = x * lax.rsqrt(jnp.mean(x * x, axis=-1, keepdims=True) + EPS) * g
    return y * (1.0 + mod[k + 1:k + 2]) + mod[k:k + 1]


def _seg_sum(x):
    lo = lax.broadcasted_iota(jnp.int32, (1, LANES), 1) < HEAD_DIM
    s_lo = jnp.sum(jnp.where(lo, x, 0.0), axis=-1, keepdims=True)
    s_hi = jnp.sum(jnp.where(lo, 0.0, x), axis=-1, keepdims=True)
    return jnp.where(lo, s_lo, s_hi)


def _seg_sum_wide(x):
    return jnp.concatenate([_seg_sum(x[:, p * LANES:(p + 1) * LANES]) for p in range(x.shape[1] // LANES)], axis=1)


def _mod_kernel(c_ref, w_ref, b_ref, o_ref):
    c = c_ref[...]
    s = c * jax.nn.sigmoid(c)
    o_ref[0] = jnp.dot(s, w_ref[0], precision=HIGHEST, preferred_element_type=F32) + b_ref[0]


def _modulation(cvec, w_mod, b_mod):
    depth = w_mod.shape[0]
    rows = cvec.shape[0]
    tn = 1536
    return pl.pallas_call(
        _mod_kernel,
        grid=(depth, N_MOD * D_MODEL // tn),
        in_specs=[pl.BlockSpec((rows, D_MODEL), lambda l, j: (0, 0)),
                  pl.BlockSpec((1, D_MODEL, tn), lambda l, j: (l, 0, j)),
                  pl.BlockSpec((1, 1, tn), lambda l, j: (l, 0, j))],
        out_specs=pl.BlockSpec((1, rows, tn), lambda l, j: (l, 0, j)),
        out_shape=jax.ShapeDtypeStruct((depth, rows, N_MOD * D_MODEL), F32),
        compiler_params=_params("parallel", "parallel"),
    )(cvec, w_mod, b_mod.reshape(depth, 1, N_MOD * D_MODEL))


Q_SCALE = float(HEAD_DIM ** -0.5 * np.log2(np.e))


def _qkv_kernel(x_ref, g_ref, mod_ref, w_ref, qg_ref, kg_ref, cos_ref, sin_ref, q_ref, k_ref, v_ref, *, use_rope):
    h = _norm_mod(x_ref[0], g_ref[...], mod_ref[0], 0).astype(BF16)
    acc = jnp.dot(h, w_ref[...], preferred_element_type=F32)
    lane = lax.broadcasted_iota(jnp.int32, (1, LANES), 1)
    first_half = (lane % HEAD_DIM) < (HEAD_DIM // 2)

    def head_pair(xp, gain, scale):
        y = xp * lax.rsqrt(_seg_sum(xp * xp) * (1.0 / HEAD_DIM) + EPS) * gain
        if use_rope:
            rot = jnp.where(first_half, pltpu.roll(y, LANES - HEAD_DIM // 2, 1), pltpu.roll(y, HEAD_DIM // 2, 1))
            y = y * cos_ref[...] + rot * sin_ref[...]
        return (y * scale).astype(BF16)

    for p in range(Q_WIDTH // LANES):
        y = head_pair(acc[:, p * LANES:(p + 1) * LANES], qg_ref[...], Q_SCALE)
        q_ref[0, 2 * p] = y[:, :HEAD_DIM]
        q_ref[0, 2 * p + 1] = y[:, HEAD_DIM:]
    for p in range(KV_WIDTH // LANES):
        y = head_pair(acc[:, Q_WIDTH + p * LANES:Q_WIDTH + (p + 1) * LANES], kg_ref[...], 1.0)
        k_ref[0, 2 * p] = y[:, :HEAD_DIM]
        k_ref[0, 2 * p + 1] = y[:, HEAD_DIM:]
    lower = lane < HEAD_DIM
    for p in range(KV_WIDTH // LANES):
        c0 = Q_WIDTH + KV_WIDTH + p * LANES
        vp = acc[:, c0:c0 + LANES]
        v_ref[0, 2 * p] = jnp.where(lower, vp, 1.0).astype(BF16)
        v_ref[0, 2 * p + 1] = jnp.where(lower, pltpu.roll(vp, HEAD_DIM, 1), 1.0).astype(BF16)


def _qkv_project(x, g, mod, w_bf, qg, kg, cos_t, sin_t, use_rope):
    B, T, _ = x.shape
    tm = min(T, 512)
    tab_spec = pl.BlockSpec((tm, LANES), lambda b, i: (i, 0))
    head_spec = lambda n: pl.BlockSpec((1, n, tm, HEAD_DIM), lambda b, i: (b, 0, i, 0))
    return pl.pallas_call(
        functools.partial(_qkv_kernel, use_rope=use_rope),
        grid=(B, T // tm),
        in_specs=[_row_spec(tm), _const_spec((1, D_MODEL)), _mod_spec(), _const_spec((D_MODEL, QKV_WIDTH)),
                  _const_spec((1, LANES)), _const_spec((1, LANES)), tab_spec, tab_spec],
        out_specs=[head_spec(N_HEADS), head_spec(N_KV_HEADS),
                   pl.BlockSpec((1, N_KV_HEADS, tm, LANES), lambda b, i: (b, 0, i, 0))],
        out_shape=[jax.ShapeDtypeStruct((B, N_HEADS, T, HEAD_DIM), BF16),
                   jax.ShapeDtypeStruct((B, N_KV_HEADS, T, HEAD_DIM), BF16),
                   jax.ShapeDtypeStruct((B, N_KV_HEADS, T, LANES), BF16)],
        compiler_params=_params("parallel", "parallel"),
    )(x, g, mod, w_bf, qg, kg, cos_t, sin_t)


def _attn_kernel(*refs, n_src, tq, qk_chunk, pv_chunk):
    q_ref = refs[0]
    kv_refs = refs[1:1 + 2 * n_src]
    o_ref = refs[1 + 2 * n_src]
    s_scr = refs[2 + 2 * n_src]
    q = q_ref[0].reshape(GQA_REP * tq, HEAD_DIM)

    def pieces(chunk):
        out, col = [], 0
        for s in range(n_src):
            rows_total = kv_refs[2 * s].shape[2]
            ck = min(chunk, rows_total)
            for r0 in range(0, rows_total, ck):
                out.append((kv_refs[2 * s], kv_refs[2 * s + 1], r0, ck, col))
                col += ck
        return out

    m = None
    for k_ref, _, r0, ck, c0 in pieces(qk_chunk):
        s_blk = lax.dot_general(q, k_ref[0, 0, r0:r0 + ck, :], (((1,), (1,)), ((), ())), preferred_element_type=F32)
        s_scr[:, c0:c0 + ck] = s_blk
        bm = jnp.max(s_blk, axis=-1, keepdims=True)
        m = bm if m is None else jnp.maximum(m, bm)
    heads = [slice(h * tq, (h + 1) * tq) for h in range(GQA_REP)]
    m_h = [m[rows] for rows in heads]
    acc = [jnp.zeros((tq, LANES), F32) for _ in heads]
    for _, v_ref, r0, ck, c0 in pieces(pv_chunk):
        v_blk = v_ref[0, 0, r0:r0 + ck, :]
        for h, rows in enumerate(heads):
            p = jnp.exp2(s_scr[rows, c0:c0 + ck] - m_h[h]).astype(BF16)
            acc[h] = acc[h] + jnp.dot(p, v_blk, preferred_element_type=F32)
    o_ref[0] = jnp.concatenate([a[:, :HEAD_DIM] / a[:, HEAD_DIM:HEAD_DIM + 1] for a in acc], axis=1).astype(BF16)


def _attention(q, kv_sources):
    B, _, T, _ = q.shape
    tq = min(T, 128)
    total = sum(k.shape[2] for k, _ in kv_sources)
    in_specs = [pl.BlockSpec((1, GQA_REP, tq, HEAD_DIM), lambda b, g, i: (b, g, i, 0))]
    args = [q]
    for k, v in kv_sources:
        in_specs += [pl.BlockSpec((1, 1, k.shape[2], HEAD_DIM), lambda b, g, i: (b, g, 0, 0)),
                     pl.BlockSpec((1, 1, k.shape[2], LANES), lambda b, g, i: (b, g, 0, 0))]
        args += [k, v]
    return pl.pallas_call(
        functools.partial(_attn_kernel, n_src=len(kv_sources), tq=tq, qk_chunk=1024, pv_chunk=256),
        grid=(B, N_KV_HEADS, T // tq),
        in_specs=in_specs,
        out_specs=pl.BlockSpec((1, tq, GQA_REP * HEAD_DIM), lambda b, g, i: (b, i, g)),
        out_shape=jax.ShapeDtypeStruct((B, T, Q_WIDTH), BF16),
        scratch_shapes=[pltpu.VMEM((GQA_REP * tq, total), F32)],
        compiler_params=_params("parallel", "parallel", "arbitrary"),
    )(*args)


def _proj_residual_kernel(x_ref, y_ref, mod_ref, w_ref, o_ref):
    out = jnp.dot(y_ref[0], w_ref[...], preferred_element_type=F32)
    o_ref[0] = x_ref[0] + mod_ref[0][2:3] * out


def _proj_residual(x, y, mod, w_bf):
    B, T, _ = x.shape
    tm = min(T, 512)
    return pl.pallas_call(
        _proj_residual_kernel,
        grid=(B, T // tm),
        in_specs=[_row_spec(tm), _row_spec(tm), _mod_spec(), _const_spec((D_MODEL, D_MODEL))],
        out_specs=_row_spec(tm),
        out_shape=jax.ShapeDtypeStruct(x.shape, F32),
        compiler_params=_params("parallel", "parallel"),
    )(x, y, mod, w_bf)


def _mlp_kernel(x_ref, g_ref, mod_ref, win_ref, wout_ref, o_ref, *, ff_chunk):
    x = x_ref[0]
    mod = mod_ref[0]
    h = _norm_mod(x, g_ref[...], mod, 3).astype(BF16)
    acc = jnp.zeros(x.shape, F32)
    for f0 in range(0, D_FF, ff_chunk):
        u = jnp.maximum(jnp.dot(h, win_ref[:, f0:f0 + ff_chunk], preferred_element_type=F32), 0.0)
        acc = acc + jnp.dot((u * u).astype(BF16), wout_ref[f0:f0 + ff_chunk, :], preferred_element_type=F32)
    o_ref[0] = x + mod[5:6] * acc


def _mlp(x, g, mod, win_bf, wout_bf):
    B, T, _ = x.shape
    tm = min(T, 512)
    return pl.pallas_call(
        functools.partial(_mlp_kernel, ff_chunk=1024),
        grid=(B, T // tm),
        in_specs=[_row_spec(tm), _const_spec((1, D_MODEL)), _mod_spec(),
                  _const_spec((D_MODEL, D_FF)), _const_spec((D_FF, D_MODEL))],
        out_specs=_row_spec(tm),
        out_shape=jax.ShapeDtypeStruct(x.shape, F32),
        compiler_params=_params("parallel", "parallel"),
    )(x, g, mod, win_bf, wout_bf)


DECAY_SCALE = float(np.exp(-0.5))


def _rwkv_prep_kernel(x_ref, xp_ref, xn_ref, g_ref, mod_ref, mu_ref, wrkv_ref, w1_ref, w2_ref, w0_ref, a1_ref, a2_ref,
                      a0_ref, g1_ref, g2_ref, kk_ref, ka_ref,
                      r_o, v_o, kk_o, g_o, lw0_o, lw1_o, key0_o, key1_o, a0_o, a1_o, *, tm):
    i = pl.program_id(1)
    last = pl.num_programs(1) - 1
    g = g_ref[...]
    mod = mod_ref[0]
    h = _norm_mod(x_ref[0], g, mod, 0)
    hp = _norm_mod(xp_ref[0][SUBLANES - 1:SUBLANES], g, mod, 0) * jnp.where(i == 0, 0.0, 1.0)
    hn = _norm_mod(xn_ref[0][0:1], g, mod, 0) * jnp.where(i == last, 0.0, 1.0)
    row = lax.broadcasted_iota(jnp.int32, (tm, 1), 0)
    prev = jnp.where(row == 0, hp, pltpu.roll(h, 1, 0))
    nxt = jnp.where(row == tm - 1, hn, pltpu.roll(h, tm - 1, 0))
    xx = 0.5 * (prev + nxt) - h
    mix = lambda m: (h + xx * mu_ref[m:m + 1]).astype(BF16)
    dot = functools.partial(jnp.dot, preferred_element_type=F32)
    r = dot(mix(0), wrkv_ref[0])
    k = dot(mix(2), wrkv_ref[1])
    v = dot(mix(3), wrkv_ref[2])
    r_o[0] = r
    v_o[0] = v
    g_o[0] = dot(jax.nn.sigmoid(dot(mix(5), g1_ref[...])).astype(BF16), g2_ref[...])
    kk = k * kk_ref[...]
    kk_o[0] = kk / jnp.maximum(jnp.sqrt(_seg_sum_wide(kk * kk)), 1e-12)
    tw = jnp.tanh(dot(mix(1), w1_ref[...])).astype(BF16)
    ta = dot(mix(4), a1_ref[...]).astype(BF16)
    for d, (lw_o, key_o, a_o) in enumerate(((lw0_o, key0_o, a0_o), (lw1_o, key1_o, a1_o))):
        z = w0_ref[d:d + 1] + dot(tw[:, d * DECAY_LORA:(d + 1) * DECAY_LORA], w2_ref[d])
        lw_o[0] = -DECAY_SCALE * jax.nn.sigmoid(z)
        a = jax.nn.sigmoid(a0_ref[d:d + 1] + dot(ta[:, d * ICLR_LORA:(d + 1) * ICLR_LORA], a2_ref[d]))
        a_o[0] = a
        key_o[0] = k * (1.0 + (a - 1.0) * ka_ref[...])


def _rwkv_prepare(x, g, mod, p):
    B, T, _ = x.shape
    tm = min(T, 256)
    nb = tm // SUBLANES
    n_blk8 = T // SUBLANES
    prev_spec = pl.BlockSpec((1, SUBLANES, D_MODEL), lambda b, i: (b, jnp.maximum(i * nb - 1, 0), 0))
    next_spec = pl.BlockSpec((1, SUBLANES, D_MODEL), lambda b, i: (b, jnp.minimum((i + 1) * nb, n_blk8 - 1), 0))
    out = jax.ShapeDtypeStruct(x.shape, F32)
    return pl.pallas_call(
        functools.partial(_rwkv_prep_kernel, tm=tm),
        grid=(B, T // tm),
        in_specs=[_row_spec(tm), prev_spec, next_spec, _const_spec((1, D_MODEL)), _mod_spec(),
                  _const_spec((6, D_MODEL)), _const_spec((3, D_MODEL, D_MODEL)),
                  _const_spec((D_MODEL, 2 * DECAY_LORA)), _const_spec((2, DECAY_LORA, D_MODEL)),
                  _const_spec((2, D_MODEL)),
                  _const_spec((D_MODEL, 2 * ICLR_LORA)), _const_spec((2, ICLR_LORA, D_MODEL)),
                  _const_spec((2, D_MODEL)),
                  _const_spec((D_MODEL, GATE_LORA)), _const_spec((GATE_LORA, D_MODEL)),
                  _const_spec((1, D_MODEL)), _const_spec((1, D_MODEL))],
        out_specs=[_row_spec(tm)] * 10,
        out_shape=[out] * 10,
        compiler_params=_params("parallel", "parallel"),
    )(x, x, x, g, mod, p["mu"], p["w_rkv"], p["w1"], p["w2"], p["w0"], p["a1"], p["a2"], p["a0"], p["g1"], p["g2"],
      p["k_k"], p["k_a"])


def _wkv_kernel(r_ref, lw_ref, k_ref, v_ref, kk_ref, a_ref, s0_ref, y_ref, s_ref, state, *, n_chunks, n_pairs,
                reverse):
    C = CHUNK
    S2 = 2 * C
    def mm(dims):
        return lambda a, b: lax.dot_general(a.astype(BF16), b.astype(BF16), dims, preferred_element_type=F32)

    hp = mm((((1,), (0,)), ((), ())))
    nt = mm((((1,), (1,)), ((), ())))
    tn = mm((((0,), (0,)), ((), ())))

    @pl.when(pl.program_id(2) == 0)
    def _():
        state[...] = s0_ref[0]

    ti = lax.broadcasted_iota(jnp.int32, (C, C), 0)
    tj = lax.broadcasted_iota(jnp.int32, (C, C), 1)
    tri = jnp.where((ti <= tj) if reverse else (ti >= tj), 1.0, 0.0).astype(BF16)
    si = lax.broadcasted_iota(jnp.int32, (S2, S2), 0)
    sj = lax.broadcasted_iota(jnp.int32, (S2, S2), 1)
    strict = (si < sj) if reverse else (si > sj)
    incl = (si <= sj) if reverse else (si >= sj)
    blk = (si // INV_BLOCK) == (sj // INV_BLOCK)
    eye = si == sj
    ident = jnp.where(eye, 1.0, 0.0)
    head0 = lax.broadcasted_iota(jnp.int32, (1, LANES), 1) < RWKV_HEAD
    stack = lambda z: jnp.concatenate([jnp.where(head0, z, 0.0), jnp.where(head0, 0.0, z)], axis=0)
    unstack = lambda z: z[:C] + z[C:]

    each = lambda f, *ls: [f(*xs) for xs in zip(*ls)]
    units = [(slice(c * C, (c + 1) * C), slice(p * LANES, (p + 1) * LANES))
             for p in range(n_pairs) for c in range(n_chunks)]
    load = lambda ref: [ref[0, rows, cols] for rows, cols in units]
    r, lw, k, v, kk, a = (load(ref) for ref in (r_ref, lw_ref, k_ref, v_ref, kk_ref, a_ref))

    def cumsum(x):
        x1 = x.astype(BF16)
        x2 = (x - x1.astype(F32)).astype(BF16)
        x3 = (x - x1.astype(F32) - x2.astype(F32)).astype(BF16)
        out = jnp.dot(tri, jnp.concatenate([x1, x2, x3], axis=1), preferred_element_type=F32)
        return out[:, :LANES] + out[:, LANES:2 * LANES] + out[:, 2 * LANES:]

    cum = each(cumsum, lw)
    total = each(lambda z: z[0:1] if reverse else z[C - 1:C], cum)
    g_in = each(lambda z: jnp.exp(-z), cum)
    g_to_end = each(lambda t, z: jnp.exp(t - z), total, cum)
    b = each(lambda x, y: x * y, kk, a)
    a_s = each(lambda x, z, l: stack(x * jnp.exp(z - l)), kk, cum, lw)
    b_s = each(lambda x, y: stack(x * y), b, g_in)
    k_s = each(lambda x, y: stack(x * y), k, g_in)
    r_s = each(lambda x, z: stack(x * jnp.exp(z)), r, cum)
    v_s = each(stack, v)
    bh_s = each(lambda x, y: stack(x * y), b, g_to_end)
    kh_s = each(lambda x, y: stack(x * y), k, g_to_end)
    big = each(lambda a_, r_, b_, k_: nt(jnp.concatenate([a_, r_], axis=0), jnp.concatenate([b_, k_], axis=0)),
               a_s, r_s, b_s, k_s)
    l_ab = each(lambda z: jnp.where(strict, z[:S2, :S2], 0.0), big)
    l_ak = each(lambda z: jnp.where(strict, z[:S2, S2:], 0.0), big)
    p_rb = each(lambda z: jnp.where(incl, z[S2:, :S2], 0.0), big)
    p_rk = each(lambda z: jnp.where(incl, z[S2:, S2:], 0.0), big)
    l_d = each(lambda z: jnp.where(blk, z, 0.0), l_ab)
    l_o = each(lambda x, y: x - y, l_ab, l_d)
    l2 = each(hp, l_d, l_d)
    lv = each(hp, l_ak, v_s)
    prv = each(hp, p_rk, v_s)
    khv = each(tn, kh_s, v_s)
    l4 = each(hp, l2, l2)
    p1 = each(lambda d, s2: ident - d + s2 - hp(d, s2), l_d, l2)
    l8 = each(hp, l4, l4)
    p2 = each(lambda p, s4: p + hp(p, s4), p1, l4)
    t_d = each(lambda p, s8: p + hp(p, s8), p2, l8)
    m1 = each(hp, t_d, l_o)
    m2 = each(hp, m1, m1)
    n1 = each(lambda x, y: ident - x + y - hp(x, y), m1, m2)
    t_inv = each(hp, n1, t_d)
    w = each(lambda t, x, y: hp(t, jnp.concatenate([x, y], axis=1)), t_inv, a_s, lv)
    pw = each(hp, p_rb, w)
    bw = each(tn, bh_s, w)
    q1 = each(lambda x, y: unstack(x - y[:, :S2]), r_s, pw)
    y2 = each(lambda x, y: unstack(x - y[:, S2:]), prv, pw)
    g_mat = each(lambda t, y: jnp.where(eye, jnp.exp(t), 0.0) - y[:, :S2], total, bw)
    h_mat = each(lambda x, y: x - y[:, S2:], khv, bw)
    pre = list(zip(q1, y2, g_mat, h_mat))

    s = [state[p] for p in range(n_pairs)]
    order = range(n_chunks - 1, -1, -1) if reverse else range(n_chunks)
    for c in order:
        for p in range(n_pairs):
            q1, y2, g_mat, h_mat = pre[p * n_chunks + c]
            y_ref[0, c * C:(c + 1) * C, p * LANES:(p + 1) * LANES] = hp(q1, s[p]) + y2
            s[p] = hp(g_mat, s[p]) + h_mat
    for p in range(n_pairs):
        state[p] = s[p]

    @pl.when(pl.program_id(2) == pl.num_programs(2) - 1)
    def _():
        for p in range(n_pairs):
            s_ref[0, p] = s[p]


WKV_PAIRS = 2
WKV_CHUNKS = 4


def _wkv(r, lw, k, v, kk, a, s0, reverse):
    B, T, _ = r.shape
    rows = min(T, WKV_CHUNKS * CHUNK)
    n_steps = T // rows
    width = WKV_PAIRS * LANES
    tmap = (lambda b, p, j: (b, n_steps - 1 - j, p)) if reverse else (lambda b, p, j: (b, j, p))
    seq_spec = pl.BlockSpec((1, rows, width), tmap)
    st_spec = pl.BlockSpec((1, WKV_PAIRS, LANES, LANES), lambda b, p, j: (b, p, 0, 0))
    return pl.pallas_call(
        functools.partial(_wkv_kernel, n_chunks=rows // CHUNK, n_pairs=WKV_PAIRS, reverse=reverse),
        grid=(B, PAIRS // WKV_PAIRS, n_steps),
        in_specs=[seq_spec] * 6 + [st_spec],
        out_specs=[seq_spec, st_spec],
        out_shape=[jax.ShapeDtypeStruct(r.shape, F32), jax.ShapeDtypeStruct((B, PAIRS, LANES, LANES), F32)],
        scratch_shapes=[pltpu.VMEM((WKV_PAIRS, LANES, LANES), F32)],
        compiler_params=_params("parallel", "parallel", "arbitrary"),
    )(r, lw, k, v, kk, a, s0)


def _rwkv_finish_kernel(x_ref, y0_ref, y1_ref, r_ref, v_ref, k0_ref, k1_ref, g_ref, mod_ref, rk_ref, lng_ref, lnb_ref,
                        wo_ref, o_ref):
    y = y0_ref[0] + y1_ref[0]
    inv_n = 1.0 / RWKV_HEAD
    mean = _seg_sum_wide(y) * inv_n
    yc = y - mean
    var = _seg_sum_wide(yc * yc) * inv_n
    yn = yc * lax.rsqrt(var + GN_EPS) * lng_ref[...] + lnb_ref[...]
    bonus = _seg_sum_wide(r_ref[0] * (k0_ref[0] + k1_ref[0]) * rk_ref[...]) * v_ref[0]
    out = jnp.dot(((yn + bonus) * g_ref[0]).astype(BF16), wo_ref[...], preferred_element_type=F32)
    o_ref[0] = x_ref[0] + mod_ref[0][2:3] * out


def _rwkv_finish(x, y0, y1, r, v, k0, k1, gate, mod, p):
    B, T, _ = x.shape
    tm = min(T, 256)
    vec = _const_spec((1, D_MODEL))
    return pl.pallas_call(
        _rwkv_finish_kernel,
        grid=(B, T // tm),
        in_specs=[_row_spec(tm)] * 8 + [_mod_spec(), vec, vec, vec, _const_spec((D_MODEL, D_MODEL))],
        out_specs=_row_spec(tm),
        out_shape=jax.ShapeDtypeStruct(x.shape, F32),
        compiler_params=_params("parallel", "parallel"),
    )(x, y0, y1, r, v, k0, k1, gate, mod, p["r_k"], p["ln_g"], p["ln_b"], p["w_o"])


HALO = SUBLANES


def _pool_kernel(x_ref, xp_ref, xn_ref, g_ref, mod_ref, w_ref, sc_ref, o_ref, *, tm, seq_len):
    i = pl.program_id(1)
    last = pl.num_programs(1) - 1
    g = g_ref[...]
    mod = mod_ref[0]
    x = x_ref[0]
    h = _norm_mod(x, g, mod, 0)
    hp = _norm_mod(xp_ref[0], g, mod, 0) * jnp.where(i == 0, 0.0, 1.0)
    hn = _norm_mod(xn_ref[0], g, mod, 0) * jnp.where(i == last, 0.0, 1.0)
    ext = jnp.concatenate([hp, h, hn], axis=0)
    n_ext = tm + 2 * HALO
    t = i * tm + lax.broadcasted_iota(jnp.int32, (tm, 1), 0)
    outs = []
    for gi, win in enumerate(POOL_WINDOWS):
        e = ext[:, gi * POOL_GROUP:(gi + 1) * POOL_GROUP]
        acc = e + pltpu.roll(e, 1, 0)
        step = 1
        while 2 * step < win:
            acc = pltpu.roll(acc, step, 0) + pltpu.roll(acc, n_ext - step, 0)
            step *= 2
        half = win // 2
        cnt = (jnp.minimum(t + half, seq_len) - jnp.maximum(t - half, 0)).astype(F32)
        pooled = acc[HALO:HALO + tm] / cnt - e[HALO:HALO + tm]
        outs.append(jnp.dot(pooled.astype(BF16), w_ref[gi], preferred_element_type=F32))
    y = jnp.concatenate(outs, axis=1) * sc_ref[...]
    o_ref[0] = x + mod[2:3] * y


def _pool_mixer(x, g, mod, w_bf, scale):
    B, T, _ = x.shape
    tm = min(T, 256)
    nb = tm // HALO
    n_blk = T // HALO
    prev_spec = pl.BlockSpec((1, HALO, D_MODEL), lambda b, i: (b, jnp.maximum(i * nb - 1, 0), 0))
    next_spec = pl.BlockSpec((1, HALO, D_MODEL), lambda b, i: (b, jnp.minimum((i + 1) * nb, n_blk - 1), 0))
    n_grp = len(POOL_WINDOWS)
    return pl.pallas_call(
        functools.partial(_pool_kernel, tm=tm, seq_len=T),
        grid=(B, T // tm),
        in_specs=[_row_spec(tm), prev_spec, next_spec, _const_spec((1, D_MODEL)), _mod_spec(),
                  _const_spec((n_grp, POOL_GROUP, POOL_GROUP)), _const_spec((1, D_MODEL))],
        out_specs=_row_spec(tm),
        out_shape=jax.ShapeDtypeStruct(x.shape, F32),
        compiler_params=_params("parallel", "parallel"),
    )(x, x, x, g, mod, w_bf, scale)


def _rope_tables(n_tokens):
    rows = n_tokens // GRID_W
    n_freq = HEAD_DIM // 4
    inv = ROPE_THETA ** (-jnp.arange(n_freq, dtype=F32) / n_freq)
    ang_r = jnp.arange(rows, dtype=F32)[:, None] * inv
    ang_c = jnp.arange(GRID_W, dtype=F32)[:, None] * inv
    ang = jnp.concatenate([
        jnp.broadcast_to(ang_r[:, None, :], (rows, GRID_W, n_freq)),
        jnp.broadcast_to(ang_c[None, :, :], (rows, GRID_W, n_freq))], axis=-1).reshape(rows * GRID_W, 2 * n_freq)
    cos, sin = jnp.cos(ang), jnp.sin(ang)
    return jnp.tile(cos, (1, 4)), jnp.tile(jnp.concatenate([-sin, sin], axis=-1), (1, 2))


def kernel(x, c, ctx, c_ctx, w_mod, b_mod, norm1_g, norm2_g, mlp_w_in, mlp_w_out, attn_w_qkv, attn_q_gain, attn_k_gain, attn_w_o, rwkv_mu, rwkv_w_rkv, rwkv_w0, rwkv_w1, rwkv_w2, rwkv_a0, rwkv_a1, rwkv_a2, rwkv_g1, rwkv_g2, rwkv_k_k, rwkv_k_a, rwkv_r_k, rwkv_ln_g, rwkv_ln_b, rwkv_w_o, pool_w, pool_scale):
    B, S, _ = x.shape
    L = ctx.shape[1]
    depth = w_mod.shape[0]
    assert x.shape[2] == D_MODEL and S % (4 * CHUNK) == 0 and L % CHUNK == 0 and S % GRID_W == 0

    n_rows = -(-(B + 1) // SUBLANES) * SUBLANES
    cvec = jnp.concatenate([c, c_ctx[None], jnp.zeros((n_rows - B - 1, D_MODEL), F32)], axis=0)
    mod_all = _modulation(cvec, w_mod, b_mod).reshape(depth, n_rows, N_MOD, D_MODEL)
    cos_t, sin_t = _rope_tables(S)
    zero_tab = jnp.zeros((L, LANES), F32)
    row = lambda a: a.reshape(1, -1)

    for i in range(depth):
        last = i == depth - 1
        j = i // N_MIXERS
        mod_l = mod_all[i, :B]
        mod_c = jnp.broadcast_to(mod_all[i, B][None], (B, N_MOD, D_MODEL))
        g1 = row(norm1_g[i])
        kind = i % N_MIXERS
        if kind == 0:
            w_qkv = attn_w_qkv[j].astype(BF16)
            w_o = attn_w_o[j].astype(BF16)
            qg = jnp.tile(row(attn_q_gain[j]), (1, 2))
            kg = jnp.tile(row(attn_k_gain[j]), (1, 2))
            q_l, k_l, v_l = _qkv_project(x, g1, mod_l, w_qkv, qg, kg, cos_t, sin_t, True)
            q_c, k_c, v_c = _qkv_project(ctx, g1, mod_c, w_qkv, qg, kg, zero_tab, zero_tab, False)
            o_l = _attention(q_l, [(k_l, v_l), (k_c, v_c)])
            x = _proj_residual(x, o_l, mod_l, w_o)
            if not last:
                o_c = _attention(q_c, [(k_c, v_c)])
                ctx = _proj_residual(ctx, o_c, mod_c, w_o)
        elif kind == 1:
            p = {
                "mu": rwkv_mu[j], "w_rkv": rwkv_w_rkv[j].astype(BF16),
                "w1": jnp.concatenate([rwkv_w1[j, 0], rwkv_w1[j, 1]], axis=1).astype(BF16),
                "w2": rwkv_w2[j].astype(BF16), "w0": rwkv_w0[j],
                "a1": jnp.concatenate([rwkv_a1[j, 0], rwkv_a1[j, 1]], axis=1).astype(BF16),
                "a2": rwkv_a2[j].astype(BF16), "a0": rwkv_a0[j],
                "g1": rwkv_g1[j].astype(BF16), "g2": rwkv_g2[j].astype(BF16),
                "k_k": row(rwkv_k_k[j]), "k_a": row(rwkv_k_a[j]), "r_k": row(rwkv_r_k[j]),
                "ln_g": row(rwkv_ln_g[j]), "ln_b": row(rwkv_ln_b[j]), "w_o": rwkv_w_o[j].astype(BF16),
            }
            r_l, v_l, kk_l, g_l, lw0_l, lw1_l, key0_l, key1_l, a0_l, a1_l = _rwkv_prepare(x, g1, mod_l, p)
            r_c, v_c, kk_c, g_c, lw0_c, lw1_c, key0_c, key1_c, a0_c, a1_c = _rwkv_prepare(ctx, g1, mod_c, p)
            zero_state = jnp.zeros((B, PAIRS, LANES, LANES), F32)
            ys_l, ys_c = [], []
            for rev, (lw_l, key_l, a_l, lw_c, key_c, a_c) in enumerate(
                    ((lw0_l, key0_l, a0_l, lw0_c, key0_c, a0_c), (lw1_l, key1_l, a1_l, lw1_c, key1_c, a1_c))):
                y_c, s_ctx = _wkv(r_c, lw_c, key_c, v_c, kk_c, a_c, zero_state, bool(rev))
                y_l, _ = _wkv(r_l, lw_l, key_l, v_l, kk_l, a_l, s_ctx, bool(rev))
                ys_l.append(y_l)
                ys_c.append(y_c)
            x = _rwkv_finish(x, ys_l[0], ys_l[1], r_l, v_l, key0_l, key1_l, g_l, mod_l, p)
            if not last:
                ctx = _rwkv_finish(ctx, ys_c[0], ys_c[1], r_c, v_c, key0_c, key1_c, g_c, mod_c, p)
        else:
            w_p = pool_w[j].astype(BF16)
            sc = row(pool_scale[j])
            x = _pool_mixer(x, g1, mod_l, w_p, sc)
            if not last:
                ctx = _pool_mixer(ctx, g1, mod_c, w_p, sc)
        g2 = row(norm2_g[i])
        w_in = mlp_w_in[i].astype(BF16)
        w_out = mlp_w_out[i].astype(BF16)
        x = _mlp(x, g2, mod_l, w_in, w_out)
        if not last:
            ctx = _mlp(ctx, g2, mod_c, w_in, w_out)
    return x
```

```python
import functools

import jax
import jax.numpy as jnp
import numpy as np
from jax import lax
from jax.experimental import pallas as pl
from jax.experimental.pallas import tpu as pltpu

F32 = jnp.float32
BF16 = jnp.bfloat16

D_MODEL = 1024
GRID_W = 64
N_MIXERS = 3
N_MOD = 6
EPS = 1e-6
N_HEADS = 16
N_KV_HEADS = 4
HEAD_DIM = 64
GQA_REP = N_HEADS // N_KV_HEADS
Q_WIDTH = N_HEADS * HEAD_DIM
KV_WIDTH = N_KV_HEADS * HEAD_DIM
QKV_WIDTH = Q_WIDTH + 2 * KV_WIDTH
ROPE_THETA = 10000.0
RWKV_HEAD = 64
DECAY_LORA = 64
ICLR_LORA = 64
GATE_LORA = 160
GN_EPS = RWKV_HEAD * 1e-5
POOL_WINDOWS = (2, 4, 8, 16)
POOL_GROUP = D_MODEL // len(POOL_WINDOWS)
D_FF = 4 * D_MODEL

LANES = 128
SUBLANES = 8
PAIRS = D_MODEL // LANES
CHUNK = 64
INV_BLOCK = 16
VMEM_LIMIT = 56 * 1024 * 1024

HIGHEST = lax.Precision.HIGHEST


def _params(*sem):
    return pltpu.CompilerParams(dimension_semantics=sem, vmem_limit_bytes=VMEM_LIMIT)


def _const_spec(shape):
    zeros = (0,) * len(shape)
    return pl.BlockSpec(shape, lambda *_: zeros, pipeline_mode=pl.Buffered(1))


def _row_spec(tm, width=D_MODEL):
    return pl.BlockSpec((1, tm, width), lambda b, i: (b, i, 0))


def _mod_spec():
    return pl.BlockSpec((1, N_MOD, D_MODEL), lambda b, i: (b, 0, 0))


def _norm_mod(x, g, mod, k):
    y = x * lax.rsqrt(jnp.mean(x * x, axis=-1, keepdims=True) + EPS) * g
    return y * (1.0 + mod[k + 1:k + 2]) + mod[k:k + 1]


def _seg_sum(x):
    lo = lax.broadcasted_iota(jnp.int32, (1, LANES), 1) < HEAD_DIM
    s_lo = jnp.sum(jnp.where(lo, x, 0.0), axis=-1, keepdims=True)
    s_hi = jnp.sum(jnp.where(lo, 0.0, x), axis=-1, keepdims=True)
    return jnp.where(lo, s_lo, s_hi)


def _seg_sum_wide(x):
    return jnp.concatenate([_seg_sum(x[:, p * LANES:(p + 1) * LANES]) for p in range(x.shape[1] // LANES)], axis=1)


def _mod_kernel(c_ref, w_ref, b_ref, o_ref):
    c = c_ref[...]
    s = c * jax.nn.sigmoid(c)
    o_ref[0] = jnp.dot(s, w_ref[0], precision=HIGHEST, preferred_element_type=F32) + b_ref[0]


def _modulation(cvec, w_mod, b_mod):
    depth = w_mod.shape[0]
    rows = cvec.shape[0]
    tn = 1536
    return pl.pallas_call(
        _mod_kernel,
        grid=(depth, N_MOD * D_MODEL // tn),
        in_specs=[pl.BlockSpec((rows, D_MODEL), lambda l, j: (0, 0)),
                  pl.BlockSpec((1, D_MODEL, tn), lambda l, j: (l, 0, j)),
                  pl.BlockSpec((1, 1, tn), lambda l, j: (l, 0, j))],
        out_specs=pl.BlockSpec((1, rows, tn), lambda l, j: (l, 0, j)),
        out_shape=jax.ShapeDtypeStruct((depth, rows, N_MOD * D_MODEL), F32),
        compiler_params=_params("parallel", "parallel"),
    )(cvec, w_mod, b_mod.reshape(depth, 1, N_MOD * D_MODEL))


Q_SCALE = float(HEAD_DIM ** -0.5 * np.log2(np.e))


def _qkv_kernel(x_ref, g_ref, mod_ref, w_ref, qg_ref, kg_ref, cos_ref, sin_ref, q_ref, k_ref, v_ref, *, use_rope):
    h = _norm_mod(x_ref[0], g_ref[...], mod_ref[0], 0).astype(BF16)
    acc = jnp.dot(h, w_ref[...], preferred_element_type=F32)
    lane = lax.broadcasted_iota(jnp.int32, (1, LANES), 1)
    first_half = (lane % HEAD_DIM) < (HEAD_DIM // 2)

    def head_pair(xp, gain, scale):
        y = xp * lax.rsqrt(_seg_sum(xp * xp) * (1.0 / HEAD_DIM) + EPS) * gain
        if use_rope:
            rot = jnp.where(first_half, pltpu.roll(y, LANES - HEAD_DIM // 2, 1), pltpu.roll(y, HEAD_DIM // 2, 1))
            y = y * cos_ref[...] + rot * sin_ref[...]
        return (y * scale).astype(BF16)

    for p in range(Q_WIDTH // LANES):
        y = head_pair(acc[:, p * LANES:(p + 1) * LANES], qg_ref[...], Q_SCALE)
        q_ref[0, 2 * p] = y[:, :HEAD_DIM]
        q_ref[0, 2 * p + 1] = y[:, HEAD_DIM:]
    for p in range(KV_WIDTH // LANES):
        y = head_pair(acc[:, Q_WIDTH + p * LANES:Q_WIDTH + (p + 1) * LANES], kg_ref[...], 1.0)
        k_ref[0, 2 * p] = y[:, :HEAD_DIM]
        k_ref[0, 2 * p + 1] = y[:, HEAD_DIM:]
    ones = jnp.ones((HEAD_DIM, acc.shape[0]), F32)
    for p in range(KV_WIDTH // LANES):
        c0 = Q_WIDTH + KV_WIDTH + p * LANES
        vt = acc[:, c0:c0 + LANES].T
        v_ref[0, 2 * p] = jnp.concatenate([vt[:HEAD_DIM], ones], axis=0).astype(BF16)
        v_ref[0, 2 * p + 1] = jnp.concatenate([vt[HEAD_DIM:], ones], axis=0).astype(BF16)


def _qkv_project(x, g, mod, w_bf, qg, kg, cos_t, sin_t, use_rope):
    B, T, _ = x.shape
    tm = min(T, 512)
    tab_spec = pl.BlockSpec((tm, LANES), lambda b, i: (i, 0))
    head_spec = lambda n: pl.BlockSpec((1, n, tm, HEAD_DIM), lambda b, i: (b, 0, i, 0))
    return pl.pallas_call(
        functools.partial(_qkv_kernel, use_rope=use_rope),
        grid=(B, T // tm),
        in_specs=[_row_spec(tm), _const_spec((1, D_MODEL)), _mod_spec(), _const_spec((D_MODEL, QKV_WIDTH)),
                  _const_spec((1, LANES)), _const_spec((1, LANES)), tab_spec, tab_spec],
        out_specs=[head_spec(N_HEADS), head_spec(N_KV_HEADS),
                   pl.BlockSpec((1, N_KV_HEADS, LANES, tm), lambda b, i: (b, 0, 0, i))],
        out_shape=[jax.ShapeDtypeStruct((B, N_HEADS, T, HEAD_DIM), BF16),
                   jax.ShapeDtypeStruct((B, N_KV_HEADS, T, HEAD_DIM), BF16),
                   jax.ShapeDtypeStruct((B, N_KV_HEADS, LANES, T), BF16)],
        compiler_params=_params("parallel", "parallel"),
    )(x, g, mod, w_bf, qg, kg, cos_t, sin_t)


def _attn_kernel(*refs, n_src, tq):
    q_ref = refs[0]
    kv_refs = refs[1:1 + 2 * n_src]
    o_ref = refs[1 + 2 * n_src]
    s_scr, m_scr = refs[2 + 2 * n_src:]
    nq = GQA_REP * tq
    i = pl.program_id(2)
    pieces = []
    row = 0
    for s in range(n_src):
        keys_total = kv_refs[2 * s].shape[2]
        ck = min(ATTN_KV_CHUNK, keys_total)
        for r0 in range(0, keys_total, ck):
            pieces.append((kv_refs[2 * s], kv_refs[2 * s + 1], r0, ck, row))
            row += ck

    def run(with_pass2):
        q = q_ref[0].reshape(nq, HEAD_DIM)
        m8 = None
        if with_pass2:
            m_prev = m_scr[...]
            acc = jnp.zeros((LANES, nq), F32)
        for k_ref, vt_ref, r0, ck, c0 in pieces:
            if with_pass2:
                p = jnp.exp2(s_scr[c0:c0 + ck, :] - m_prev).astype(BF16)
                acc = acc + jnp.dot(vt_ref[0, 0, :, r0:r0 + ck], p, preferred_element_type=F32)
            s_blk = lax.dot_general(k_ref[0, 0, r0:r0 + ck, :], q, (((1,), (1,)), ((), ())),
                                    preferred_element_type=F32)
            s_scr[c0:c0 + ck, :] = s_blk
            bm = jnp.max(s_blk.reshape(ck // SUBLANES, SUBLANES, nq), axis=0)
            m8 = bm if m8 is None else jnp.maximum(m8, bm)
        m_scr[...] = jnp.max(m8, axis=0, keepdims=True)
        if with_pass2:
            o = (acc[:HEAD_DIM] / acc[HEAD_DIM:HEAD_DIM + 1]).T
            o_ref[0] = jnp.concatenate([o[h * tq:(h + 1) * tq] for h in range(GQA_REP)], axis=1).astype(BF16)

    pl.when(i == 0)(lambda: run(False))
    pl.when(i > 0)(lambda: run(True))


ATTN_KV_CHUNK = 1024


def _attention(q, kv_sources):
    B, _, T, _ = q.shape
    tq = min(T, 128)
    n_tiles = T // tq
    total = sum(k.shape[2] for k, _ in kv_sources)
    in_specs = [pl.BlockSpec((1, GQA_REP, tq, HEAD_DIM), lambda b, g, i: (b, g, jnp.minimum(i, n_tiles - 1), 0))]
    args = [q]
    for k, vt in kv_sources:
        in_specs += [pl.BlockSpec((1, 1, k.shape[2], HEAD_DIM), lambda b, g, i: (b, g, 0, 0)),
                     pl.BlockSpec((1, 1, LANES, k.shape[2]), lambda b, g, i: (b, g, 0, 0))]
        args += [k, vt]
    return pl.pallas_call(
        functools.partial(_attn_kernel, n_src=len(kv_sources), tq=tq),
        grid=(B, N_KV_HEADS, n_tiles + 1),
        in_specs=in_specs,
        out_specs=pl.BlockSpec((1, tq, GQA_REP * HEAD_DIM), lambda b, g, i: (b, jnp.maximum(i - 1, 0), g)),
        out_shape=jax.ShapeDtypeStruct((B, T, Q_WIDTH), BF16),
        scratch_shapes=[pltpu.VMEM((total, GQA_REP * tq), F32), pltpu.VMEM((1, GQA_REP * tq), F32)],
        compiler_params=_params("parallel", "parallel", "arbitrary"),
    )(*args)


def _proj_residual_kernel(x_ref, y_ref, mod_ref, w_ref, o_ref):
    out = jnp.dot(y_ref[0], w_ref[...], preferred_element_type=F32)
    o_ref[0] = x_ref[0] + mod_ref[0][2:3] * out


def _proj_residual(x, y, mod, w_bf):
    B, T, _ = x.shape
    tm = min(T, 512)
    return pl.pallas_call(
        _proj_residual_kernel,
        grid=(B, T // tm),
        in_specs=[_row_spec(tm), _row_spec(tm), _mod_spec(), _const_spec((D_MODEL, D_MODEL))],
        out_specs=_row_spec(tm),
        out_shape=jax.ShapeDtypeStruct(x.shape, F32),
        compiler_params=_params("parallel", "parallel"),
    )(x, y, mod, w_bf)


def _mlp_kernel(x_ref, g_ref, mod_ref, win_ref, wout_ref, o_ref, *, ff_chunk):
    x = x_ref[0]
    mod = mod_ref[0]
    h = _norm_mod(x, g_ref[...], mod, 3).astype(BF16)
    acc = jnp.zeros(x.shape, F32)
    for f0 in range(0, D_FF, ff_chunk):
        u = jnp.maximum(jnp.dot(h, win_ref[:, f0:f0 + ff_chunk], preferred_element_type=F32), 0.0)
        acc = acc + jnp.dot((u * u).astype(BF16), wout_ref[f0:f0 + ff_chunk, :], preferred_element_type=F32)
    o_ref[0] = x + mod[5:6] * acc


def _mlp(x, g, mod, win_bf, wout_bf):
    B, T, _ = x.shape
    tm = min(T, 512)
    return pl.pallas_call(
        functools.partial(_mlp_kernel, ff_chunk=1024),
        grid=(B, T // tm),
        in_specs=[_row_spec(tm), _const_spec((1, D_MODEL)), _mod_spec(),
                  _const_spec((D_MODEL, D_FF)), _const_spec((D_FF, D_MODEL))],
        out_specs=_row_spec(tm),
        out_shape=jax.ShapeDtypeStruct(x.shape, F32),
        compiler_params=_params("parallel", "parallel"),
    )(x, g, mod, win_bf, wout_bf)


DECAY_SCALE = float(np.exp(-0.5))


def _rwkv_prep_kernel(x_ref, xp_ref, xn_ref, g_ref, mod_ref, mu_ref, wrkv_ref, w1_ref, w2_ref, w0_ref, a1_ref, a2_ref,
                      a0_ref, g1_ref, g2_ref, kk_ref, ka_ref,
                      r_o, v_o, kk_o, g_o, lw0_o, lw1_o, key0_o, key1_o, a0_o, a1_o, *, tm):
    i = pl.program_id(1)
    last = pl.num_programs(1) - 1
    g = g_ref[...]
    mod = mod_ref[0]
    h = _norm_mod(x_ref[0], g, mod, 0)
    hp = _norm_mod(xp_ref[0][SUBLANES - 1:SUBLANES], g, mod, 0) * jnp.where(i == 0, 0.0, 1.0)
    hn = _norm_mod(xn_ref[0][0:1], g, mod, 0) * jnp.where(i == last, 0.0, 1.0)
    row = lax.broadcasted_iota(jnp.int32, (tm, 1), 0)
    prev = jnp.where(row == 0, hp, pltpu.roll(h, 1, 0))
    nxt = jnp.where(row == tm - 1, hn, pltpu.roll(h, tm - 1, 0))
    xx = 0.5 * (prev + nxt) - h
    mix = lambda m: (h + xx * mu_ref[m:m + 1]).astype(BF16)
    dot = functools.partial(jnp.dot, preferred_element_type=F32)
    r = dot(mix(0), wrkv_ref[0])
    k = dot(mix(2), wrkv_ref[1])
    v = dot(mix(3), wrkv_ref[2])
    r_o[0] = r
    v_o[0] = v
    g_o[0] = dot(jax.nn.sigmoid(dot(mix(5), g1_ref[...])).astype(BF16), g2_ref[...])
    kk = k * kk_ref[...]
    kk_o[0] = kk / jnp.maximum(jnp.sqrt(_seg_sum_wide(kk * kk)), 1e-12)
    tw = jnp.tanh(dot(mix(1), w1_ref[...])).astype(BF16)
    ta = dot(mix(4), a1_ref[...]).astype(BF16)
    for d, (lw_o, key_o, a_o) in enumerate(((lw0_o, key0_o, a0_o), (lw1_o, key1_o, a1_o))):
        z = w0_ref[d:d + 1] + dot(tw[:, d * DECAY_LORA:(d + 1) * DECAY_LORA], w2_ref[d])
        lw_o[0] = -DECAY_SCALE * jax.nn.sigmoid(z)
        a = jax.nn.sigmoid(a0_ref[d:d + 1] + dot(ta[:, d * ICLR_LORA:(d + 1) * ICLR_LORA], a2_ref[d]))
        a_o[0] = a
        key_o[0] = k * (1.0 + (a - 1.0) * ka_ref[...])


def _rwkv_prepare(x, g, mod, p):
    B, T, _ = x.shape
    tm = min(T, 256)
    nb = tm // SUBLANES
    n_blk8 = T // SUBLANES
    prev_spec = pl.BlockSpec((1, SUBLANES, D_MODEL), lambda b, i: (b, jnp.maximum(i * nb - 1, 0), 0))
    next_spec = pl.BlockSpec((1, SUBLANES, D_MODEL), lambda b, i: (b, jnp.minimum((i + 1) * nb, n_blk8 - 1), 0))
    out = jax.ShapeDtypeStruct(x.shape, F32)
    return pl.pallas_call(
        functools.partial(_rwkv_prep_kernel, tm=tm),
        grid=(B, T // tm),
        in_specs=[_row_spec(tm), prev_spec, next_spec, _const_spec((1, D_MODEL)), _mod_spec(),
                  _const_spec((6, D_MODEL)), _const_spec((3, D_MODEL, D_MODEL)),
                  _const_spec((D_MODEL, 2 * DECAY_LORA)), _const_spec((2, DECAY_LORA, D_MODEL)),
                  _const_spec((2, D_MODEL)),
                  _const_spec((D_MODEL, 2 * ICLR_LORA)), _const_spec((2, ICLR_LORA, D_MODEL)),
                  _const_spec((2, D_MODEL)),
                  _const_spec((D_MODEL, GATE_LORA)), _const_spec((GATE_LORA, D_MODEL)),
                  _const_spec((1, D_MODEL)), _const_spec((1, D_MODEL))],
        out_specs=[_row_spec(tm)] * 10,
        out_shape=[out] * 10,
        compiler_params=_params("parallel", "parallel"),
    )(x, x, x, g, mod, p["mu"], p["w_rkv"], p["w1"], p["w2"], p["w0"], p["a1"], p["a2"], p["a0"], p["g1"], p["g2"],
      p["k_k"], p["k_a"])


def _wkv_kernel(r_ref, lw_ref, k_ref, v_ref, kk_ref, a_ref, s0_ref, y_ref, s_ref, state, *, n_chunks, n_pairs,
                reverse):
    C = CHUNK
    S2 = 2 * C
    def mm(dims):
        return lambda a, b: lax.dot_general(a.astype(BF16), b.astype(BF16), dims, preferred_element_type=F32)

    hp = mm((((1,), (0,)), ((), ())))
    nt = mm((((1,), (1,)), ((), ())))
    tn = mm((((0,), (0,)), ((), ())))

    @pl.when(pl.program_id(2) == 0)
    def _():
        state[...] = s0_ref[0]

    ti = lax.broadcasted_iota(jnp.int32, (C, C), 0)
    tj = lax.broadcasted_iota(jnp.int32, (C, C), 1)
    tri = jnp.where((ti <= tj) if reverse else (ti >= tj), 1.0, 0.0).astype(BF16)
    si = lax.broadcasted_iota(jnp.int32, (S2, S2), 0)
    sj = lax.broadcasted_iota(jnp.int32, (S2, S2), 1)
    strict = (si < sj) if reverse else (si > sj)
    incl = (si <= sj) if reverse else (si >= sj)
    blk = (si // INV_BLOCK) == (sj // INV_BLOCK)
    eye = si == sj
    ident = jnp.where(eye, 1.0, 0.0)
    head0 = lax.broadcasted_iota(jnp.int32, (1, LANES), 1) < RWKV_HEAD
    stack = lambda z: jnp.concatenate([jnp.where(head0, z, 0.0), jnp.where(head0, 0.0, z)], axis=0)
    unstack = lambda z: z[:C] + z[C:]

    each = lambda f, *ls: [f(*xs) for xs in zip(*ls)]
    units = [(slice(c * C, (c + 1) * C), slice(p * LANES, (p + 1) * LANES))
             for p in range(n_pairs) for c in range(n_chunks)]
    load = lambda ref: [ref[0, rows, cols] for rows, cols in units]
    r, lw, k, v, kk, a = (load(ref) for ref in (r_ref, lw_ref, k_ref, v_ref, kk_ref, a_ref))

    def cumsum(x):
        x1 = x.astype(BF16)
        x2 = (x - x1.astype(F32)).astype(BF16)
        x3 = (x - x1.astype(F32) - x2.astype(F32)).astype(BF16)
        out = jnp.dot(tri, jnp.concatenate([x1, x2, x3], axis=1), preferred_element_type=F32)
        return out[:, :LANES] + out[:, LANES:2 * LANES] + out[:, 2 * LANES:]

    cum = each(cumsum, lw)
    total = each(lambda z: z[0:1] if reverse else z[C - 1:C], cum)
    g_in = each(lambda z: jnp.exp(-z), cum)
    g_to_end = each(lambda t, z: jnp.exp(t - z), total, cum)
    b = each(lambda x, y: x * y, kk, a)
    a_s = each(lambda x, z, l: stack(x * jnp.exp(z - l)), kk, cum, lw)
    b_s = each(lambda x, y: stack(x * y), b, g_in)
    k_s = each(lambda x, y: stack(x * y), k, g_in)
    r_s = each(lambda x, z: stack(x * jnp.exp(z)), r, cum)
    v_s = each(stack, v)
    bh_s = each(lambda x, y: stack(x * y), b, g_to_end)
    kh_s = each(lambda x, y: stack(x * y), k, g_to_end)
    big = each(lambda a_, r_, b_, k_: nt(jnp.concatenate([a_, r_], axis=0), jnp.concatenate([b_, k_], axis=0)),
               a_s, r_s, b_s, k_s)
    l_ab = each(lambda z: jnp.where(strict, z[:S2, :S2], 0.0), big)
    l_ak = each(lambda z: jnp.where(strict, z[:S2, S2:], 0.0), big)
    p_rb = each(lambda z: jnp.where(incl, z[S2:, :S2], 0.0), big)
    p_rk = each(lambda z: jnp.where(incl, z[S2:, S2:], 0.0), big)
    l_d = each(lambda z: jnp.where(blk, z, 0.0), l_ab)
    l_o = each(lambda x, y: x - y, l_ab, l_d)
    l2 = each(hp, l_d, l_d)
    lv = each(hp, l_ak, v_s)
    prv = each(hp, p_rk, v_s)
    khv = each(tn, kh_s, v_s)
    l4 = each(hp, l2, l2)
    p1 = each(lambda d, s2: ident - d + s2 - hp(d, s2), l_d, l2)
    l8 = each(hp, l4, l4)
    p2 = each(lambda p, s4: p + hp(p, s4), p1, l4)
    t_d = each(lambda p, s8: p + hp(p, s8), p2, l8)
    m1 = each(hp, t_d, l_o)
    m2 = each(hp, m1, m1)
    n1 = each(lambda x, y: ident - x + y - hp(x, y), m1, m2)
    t_inv = each(hp, n1, t_d)
    w = each(lambda t, x, y: hp(t, jnp.concatenate([x, y], axis=1)), t_inv, a_s, lv)
    pw = each(hp, p_rb, w)
    bw = each(tn, bh_s, w)
    q1 = each(lambda x, y: unstack(x - y[:, :S2]), r_s, pw)
    y2 = each(lambda x, y: unstack(x - y[:, S2:]), prv, pw)
    g_mat = each(lambda t, y: jnp.where(eye, jnp.exp(t), 0.0) - y[:, :S2], total, bw)
    h_mat = each(lambda x, y: x - y[:, S2:], khv, bw)
    pre = list(zip(q1, y2, g_mat, h_mat))

    s = [state[p] for p in range(n_pairs)]
    order = range(n_chunks - 1, -1, -1) if reverse else range(n_chunks)
    for c in order:
        for p in range(n_pairs):
            q1, y2, g_mat, h_mat = pre[p * n_chunks + c]
            y_ref[0, c * C:(c + 1) * C, p * LANES:(p + 1) * LANES] = hp(q1, s[p]) + y2
            s[p] = hp(g_mat, s[p]) + h_mat
    for p in range(n_pairs):
        state[p] = s[p]

    @pl.when(pl.program_id(2) == pl.num_programs(2) - 1)
    def _():
        for p in range(n_pairs):
            s_ref[0, p] = s[p]


WKV_PAIRS = 2
WKV_CHUNKS = 4


def _wkv(r, lw, k, v, kk, a, s0, reverse):
    B, T, _ = r.shape
    rows = min(T, WKV_CHUNKS * CHUNK)
    n_steps = T // rows
    width = WKV_PAIRS * LANES
    tmap = (lambda b, p, j: (b, n_steps - 1 - j, p)) if reverse else (lambda b, p, j: (b, j, p))
    seq_spec = pl.BlockSpec((1, rows, width), tmap)
    st_spec = pl.BlockSpec((1, WKV_PAIRS, LANES, LANES), lambda b, p, j: (b, p, 0, 0))
    return pl.pallas_call(
        functools.partial(_wkv_kernel, n_chunks=rows // CHUNK, n_pairs=WKV_PAIRS, reverse=reverse),
        grid=(B, PAIRS // WKV_PAIRS, n_steps),
        in_specs=[seq_spec] * 6 + [st_spec],
        out_specs=[seq_spec, st_spec],
        out_shape=[jax.ShapeDtypeStruct(r.shape, F32), jax.ShapeDtypeStruct((B, PAIRS, LANES, LANES), F32)],
        scratch_shapes=[pltpu.VMEM((WKV_PAIRS, LANES, LANES), F32)],
        compiler_params=_params("parallel", "parallel", "arbitrary"),
    )(r, lw, k, v, kk, a, s0)


def _rwkv_finish_kernel(x_ref, y0_ref, y1_ref, r_ref, v_ref, k0_ref, k1_ref, g_ref, mod_ref, rk_ref, lng_ref, lnb_ref,
                        wo_ref, o_ref):
    y = y0_ref[0] + y1_ref[0]
    inv_n = 1.0 / RWKV_HEAD
    mean = _seg_sum_wide(y) * inv_n
    yc = y - mean
    var = _seg_sum_wide(yc * yc) * inv_n
    yn = yc * lax.rsqrt(var + GN_EPS) * lng_ref[...] + lnb_ref[...]
    bonus = _seg_sum_wide(r_ref[0] * (k0_ref[0] + k1_ref[0]) * rk_ref[...]) * v_ref[0]
    out = jnp.dot(((yn + bonus) * g_ref[0]).astype(BF16), wo_ref[...], preferred_element_type=F32)
    o_ref[0] = x_ref[0] + mod_ref[0][2:3] * out


def _rwkv_finish(x, y0, y1, r, v, k0, k1, gate, mod, p):
    B, T, _ = x.shape
    tm = min(T, 256)
    vec = _const_spec((1, D_MODEL))
    return pl.pallas_call(
        _rwkv_finish_kernel,
        grid=(B, T // tm),
        in_specs=[_row_spec(tm)] * 8 + [_mod_spec(), vec, vec, vec, _const_spec((D_MODEL, D_MODEL))],
        out_specs=_row_spec(tm),
        out_shape=jax.ShapeDtypeStruct(x.shape, F32),
        compiler_params=_params("parallel", "parallel"),
    )(x, y0, y1, r, v, k0, k1, gate, mod, p["r_k"], p["ln_g"], p["ln_b"], p["w_o"])


HALO = SUBLANES


def _pool_kernel(x_ref, xp_ref, xn_ref, g_ref, mod_ref, w_ref, sc_ref, o_ref, *, tm, seq_len):
    i = pl.program_id(1)
    last = pl.num_programs(1) - 1
    g = g_ref[...]
    mod = mod_ref[0]
    x = x_ref[0]
    h = _norm_mod(x, g, mod, 0)
    hp = _norm_mod(xp_ref[0], g, mod, 0) * jnp.where(i == 0, 0.0, 1.0)
    hn = _norm_mod(xn_ref[0], g, mod, 0) * jnp.where(i == last, 0.0, 1.0)
    ext = jnp.concatenate([hp, h, hn], axis=0)
    n_ext = tm + 2 * HALO
    t = i * tm + lax.broadcasted_iota(jnp.int32, (tm, 1), 0)
    outs = []
    for gi, win in enumerate(POOL_WINDOWS):
        e = ext[:, gi * POOL_GROUP:(gi + 1) * POOL_GROUP]
        acc = e + pltpu.roll(e, 1, 0)
        step = 1
        while 2 * step < win:
            acc = pltpu.roll(acc, step, 0) + pltpu.roll(acc, n_ext - step, 0)
            step *= 2
        half = win // 2
        cnt = (jnp.minimum(t + half, seq_len) - jnp.maximum(t - half, 0)).astype(F32)
        pooled = acc[HALO:HALO + tm] / cnt - e[HALO:HALO + tm]
        outs.append(jnp.dot(pooled.astype(BF16), w_ref[gi], preferred_element_type=F32))
    y = jnp.concatenate(outs, axis=1) * sc_ref[...]
    o_ref[0] = x + mod[2:3] * y


def _pool_mixer(x, g, mod, w_bf, scale):
    B, T, _ = x.shape
    tm = min(T, 256)
    nb = tm // HALO
    n_blk = T // HALO
    prev_spec = pl.BlockSpec((1, HALO, D_MODEL), lambda b, i: (b, jnp.maximum(i * nb - 1, 0), 0))
    next_spec = pl.BlockSpec((1, HALO, D_MODEL), lambda b, i: (b, jnp.minimum((i + 1) * nb, n_blk - 1), 0))
    n_grp = len(POOL_WINDOWS)
    return pl.pallas_call(
        functools.partial(_pool_kernel, tm=tm, seq_len=T),
        grid=(B, T // tm),
        in_specs=[_row_spec(tm), prev_spec, next_spec, _const_spec((1, D_MODEL)), _mod_spec(),
                  _const_spec((n_grp, POOL_GROUP, POOL_GROUP)), _const_spec((1, D_MODEL))],
        out_specs=_row_spec(tm),
        out_shape=jax.ShapeDtypeStruct(x.shape, F32),
        compiler_params=_params("parallel", "parallel"),
    )(x, x, x, g, mod, w_bf, scale)


def _rope_tables(n_tokens):
    rows = n_tokens // GRID_W
    n_freq = HEAD_DIM // 4
    inv = ROPE_THETA ** (-jnp.arange(n_freq, dtype=F32) / n_freq)
    ang_r = jnp.arange(rows, dtype=F32)[:, None] * inv
    ang_c = jnp.arange(GRID_W, dtype=F32)[:, None] * inv
    ang = jnp.concatenate([
        jnp.broadcast_to(ang_r[:, None, :], (rows, GRID_W, n_freq)),
        jnp.broadcast_to(ang_c[None, :, :], (rows, GRID_W, n_freq))], axis=-1).reshape(rows * GRID_W, 2 * n_freq)
    cos, sin = jnp.cos(ang), jnp.sin(ang)
    return jnp.tile(cos, (1, 4)), jnp.tile(jnp.concatenate([-sin, sin], axis=-1), (1, 2))


def kernel(x, c, ctx, c_ctx, w_mod, b_mod, norm1_g, norm2_g, mlp_w_in, mlp_w_out, attn_w_qkv, attn_q_gain, attn_k_gain, attn_w_o, rwkv_mu, rwkv_w_rkv, rwkv_w0, rwkv_w1, rwkv_w2, rwkv_a0, rwkv_a1, rwkv_a2, rwkv_g1, rwkv_g2, rwkv_k_k, rwkv_k_a, rwkv_r_k, rwkv_ln_g, rwkv_ln_b, rwkv_w_o, pool_w, pool_scale):
    B, S, _ = x.shape
    L = ctx.shape[1]
    depth = w_mod.shape[0]
    assert x.shape[2] == D_MODEL and S % (4 * CHUNK) == 0 and L % CHUNK == 0 and S % GRID_W == 0

    n_rows = -(-(B + 1) // SUBLANES) * SUBLANES
    cvec = jnp.concatenate([c, c_ctx[None], jnp.zeros((n_rows - B - 1, D_MODEL), F32)], axis=0)
    mod_all = _modulation(cvec, w_mod, b_mod).reshape(depth, n_rows, N_MOD, D_MODEL)
    cos_t, sin_t = _rope_tables(S)
    zero_tab = jnp.zeros((L, LANES), F32)
    row = lambda a: a.reshape(1, -1)

    for i in range(depth):
        last = i == depth - 1
        j = i // N_MIXERS
        mod_l = mod_all[i, :B]
        mod_c = jnp.broadcast_to(mod_all[i, B][None], (B, N_MOD, D_MODEL))
        g1 = row(norm1_g[i])
        kind = i % N_MIXERS
        if kind == 0:
            w_qkv = attn_w_qkv[j].astype(BF16)
            w_o = attn_w_o[j].astype(BF16)
            qg = jnp.tile(row(attn_q_gain[j]), (1, 2))
            kg = jnp.tile(row(attn_k_gain[j]), (1, 2))
            q_l, k_l, v_l = _qkv_project(x, g1, mod_l, w_qkv, qg, kg, cos_t, sin_t, True)
            q_c, k_c, v_c = _qkv_project(ctx, g1, mod_c, w_qkv, qg, kg, zero_tab, zero_tab, False)
            o_l = _attention(q_l, [(k_l, v_l), (k_c, v_c)])
            x = _proj_residual(x, o_l, mod_l, w_o)
            if not last:
                o_c = _attention(q_c, [(k_c, v_c)])
                ctx = _proj_residual(ctx, o_c, mod_c, w_o)
        elif kind == 1:
            p = {
                "mu": rwkv_mu[j], "w_rkv": rwkv_w_rkv[j].astype(BF16),
                "w1": jnp.concatenate([rwkv_w1[j, 0], rwkv_w1[j, 1]], axis=1).astype(BF16),
                "w2": rwkv_w2[j].astype(BF16), "w0": rwkv_w0[j],
                "a1": jnp.concatenate([rwkv_a1[j, 0], rwkv_a1[j, 1]], axis=1).astype(BF16),
                "a2": rwkv_a2[j].astype(BF16), "a0": rwkv_a0[j],
                "g1": rwkv_g1[j].astype(BF16), "g2": rwkv_g2[j].astype(BF16),
                "k_k": row(rwkv_k_k[j]), "k_a": row(rwkv_k_a[j]), "r_k": row(rwkv_r_k[j]),
                "ln_g": row(rwkv_ln_g[j]), "ln_b": row(rwkv_ln_b[j]), "w_o": rwkv_w_o[j].astype(BF16),
            }
            r_l, v_l, kk_l, g_l, lw0_l, lw1_l, key0_l, key1_l, a0_l, a1_l = _rwkv_prepare(x, g1, mod_l, p)
            r_c, v_c, kk_c, g_c, lw0_c, lw1_c, key0_c, key1_c, a0_c, a1_c = _rwkv_prepare(ctx, g1, mod_c, p)
            zero_state = jnp.zeros((B, PAIRS, LANES, LANES), F32)
            ys_l, ys_c = [], []
            for rev, (lw_l, key_l, a_l, lw_c, key_c, a_c) in enumerate(
                    ((lw0_l, key0_l, a0_l, lw0_c, key0_c, a0_c), (lw1_l, key1_l, a1_l, lw1_c, key1_c, a1_c))):
                y_c, s_ctx = _wkv(r_c, lw_c, key_c, v_c, kk_c, a_c, zero_state, bool(rev))
                y_l, _ = _wkv(r_l, lw_l, key_l, v_l, kk_l, a_l, s_ctx, bool(rev))
                ys_l.append(y_l)
                ys_c.append(y_c)
            x = _rwkv_finish(x, ys_l[0], ys_l[1], r_l, v_l, key0_l, key1_l, g_l, mod_l, p)
            if not last:
                ctx = _rwkv_finish(ctx, ys_c[0], ys_c[1], r_c, v_c, key0_c, key1_c, g_c, mod_c, p)
        else:
            w_p = pool_w[j].astype(BF16)
            sc = row(pool_scale[j])
            x = _pool_mixer(x, g1, mod_l, w_p, sc)
            if not last:
                ctx = _pool_mixer(ctx, g1, mod_c, w_p, sc)
        g2 = row(norm2_g[i])
        w_in = mlp_w_in[i].astype(BF16)
        w_out = mlp_w_out[i].astype(BF16)
        x = _mlp(x, g2, mod_l, w_in, w_out)
        if not last:
            ctx = _mlp(ctx, g2, mod_c, w_in, w_out)
    return x
```

```python
import functools

import jax
import jax.numpy as jnp
import numpy as np
from jax import lax
from jax.experimental import pallas as pl
from jax.experimental.pallas import tpu as pltpu

F32 = jnp.float32
BF16 = jnp.bfloat16

D_MODEL = 1024
GRID_W = 64
N_MIXERS = 3
N_MOD = 6
EPS = 1e-6
N_HEADS = 16
N_KV_HEADS = 4
HEAD_DIM = 64
GQA_REP = N_HEADS // N_KV_HEADS
Q_WIDTH = N_HEADS * HEAD_DIM
KV_WIDTH = N_KV_HEADS * HEAD_DIM
QKV_WIDTH = Q_WIDTH + 2 * KV_WIDTH
ROPE_THETA = 10000.0
RWKV_HEAD = 64
DECAY_LORA = 64
ICLR_LORA = 64
GATE_LORA = 160
GN_EPS = RWKV_HEAD * 1e-5
POOL_WINDOWS = (2, 4, 8, 16)
POOL_GROUP = D_MODEL // len(POOL_WINDOWS)
D_FF = 4 * D_MODEL

LANES = 128
SUBLANES = 8
PAIRS = D_MODEL // LANES
CHUNK = 64
INV_BLOCK = 16
VMEM_LIMIT = 56 * 1024 * 1024

HIGHEST = lax.Precision.HIGHEST


def _params(*sem):
    return pltpu.CompilerParams(dimension_semantics=sem, vmem_limit_bytes=VMEM_LIMIT)


def _const_spec(shape):
    zeros = (0,) * len(shape)
    return pl.BlockSpec(shape, lambda *_: zeros, pipeline_mode=pl.Buffered(1))


def _row_spec(tm, width=D_MODEL):
    return pl.BlockSpec((1, tm, width), lambda b, i: (b, i, 0))


def _mod_spec():
    return pl.BlockSpec((1, N_MOD, D_MODEL), lambda b, i: (b, 0, 0))


def _norm_mod(x, g, mod, k):
    y = x * lax.rsqrt(jnp.mean(x * x, axis=-1, keepdims=True) + EPS) * g
    return y * (1.0 + mod[k + 1:k + 2]) + mod[k:k + 1]


def _seg_sum(x):
    lo = lax.broadcasted_iota(jnp.int32, (1, LANES), 1) < HEAD_DIM
    s_lo = jnp.sum(jnp.where(lo, x, 0.0), axis=-1, keepdims=True)
    s_hi = jnp.sum(jnp.where(lo, 0.0, x), axis=-1, keepdims=True)
    return jnp.where(lo, s_lo, s_hi)


def _seg_sum_wide(x):
    return jnp.concatenate([_seg_sum(x[:, p * LANES:(p + 1) * LANES]) for p in range(x.shape[1] // LANES)], axis=1)


def _mod_kernel(c_ref, w_ref, b_ref, o_ref):
    c = c_ref[...]
    s = c * jax.nn.sigmoid(c)
    o_ref[0] = jnp.dot(s, w_ref[0], precision=HIGHEST, preferred_element_type=F32) + b_ref[0]


def _modulation(cvec, w_mod, b_mod):
    depth = w_mod.shape[0]
    rows = cvec.shape[0]
    tn = 1536
    return pl.pallas_call(
        _mod_kernel,
        grid=(depth, N_MOD * D_MODEL // tn),
        in_specs=[pl.BlockSpec((rows, D_MODEL), lambda l, j: (0, 0)),
                  pl.BlockSpec((1, D_MODEL, tn), lambda l, j: (l, 0, j)),
                  pl.BlockSpec((1, 1, tn), lambda l, j: (l, 0, j))],
        out_specs=pl.BlockSpec((1, rows, tn), lambda l, j: (l, 0, j)),
        out_shape=jax.ShapeDtypeStruct((depth, rows, N_MOD * D_MODEL), F32),
        compiler_params=_params("parallel", "parallel"),
    )(cvec, w_mod, b_mod.reshape(depth, 1, N_MOD * D_MODEL))


VT_ROWS = HEAD_DIM + 16
Q_SCALE = float(HEAD_DIM ** -0.5 * np.log2(np.e))


def _qkv_kernel(x_ref, g_ref, mod_ref, w_ref, qg_ref, kg_ref, cos_ref, sin_ref, q_ref, k_ref, v_ref, *, use_rope):
    h = _norm_mod(x_ref[0], g_ref[...], mod_ref[0], 0).astype(BF16)
    acc = jnp.dot(h, w_ref[...], preferred_element_type=F32)
    lane = lax.broadcasted_iota(jnp.int32, (1, LANES), 1)
    first_half = (lane % HEAD_DIM) < (HEAD_DIM // 2)

    def head_pair(xp, gain, scale):
        y = xp * lax.rsqrt(_seg_sum(xp * xp) * (1.0 / HEAD_DIM) + EPS) * gain
        if use_rope:
            rot = jnp.where(first_half, pltpu.roll(y, LANES - HEAD_DIM // 2, 1), pltpu.roll(y, HEAD_DIM // 2, 1))
            y = y * cos_ref[...] + rot * sin_ref[...]
        return (y * scale).astype(BF16)

    for p in range(Q_WIDTH // LANES):
        y = head_pair(acc[:, p * LANES:(p + 1) * LANES], qg_ref[...], Q_SCALE)
        q_ref[0, 2 * p] = y[:, :HEAD_DIM]
        q_ref[0, 2 * p + 1] = y[:, HEAD_DIM:]
    for p in range(KV_WIDTH // LANES):
        y = head_pair(acc[:, Q_WIDTH + p * LANES:Q_WIDTH + (p + 1) * LANES], kg_ref[...], 1.0)
        k_ref[0, 2 * p] = y[:, :HEAD_DIM]
        k_ref[0, 2 * p + 1] = y[:, HEAD_DIM:]
    ones = jnp.ones((VT_ROWS - HEAD_DIM, acc.shape[0]), F32)
    for p in range(KV_WIDTH // LANES):
        c0 = Q_WIDTH + KV_WIDTH + p * LANES
        vt = acc[:, c0:c0 + LANES].T
        v_ref[0, 2 * p] = jnp.concatenate([vt[:HEAD_DIM], ones], axis=0).astype(BF16)
        v_ref[0, 2 * p + 1] = jnp.concatenate([vt[HEAD_DIM:], ones], axis=0).astype(BF16)


def _qkv_project(x, g, mod, w_bf, qg, kg, cos_t, sin_t, use_rope):
    B, T, _ = x.shape
    tm = min(T, 512)
    tab_spec = pl.BlockSpec((tm, LANES), lambda b, i: (i, 0))
    head_spec = lambda n: pl.BlockSpec((1, n, tm, HEAD_DIM), lambda b, i: (b, 0, i, 0))
    return pl.pallas_call(
        functools.partial(_qkv_kernel, use_rope=use_rope),
        grid=(B, T // tm),
        in_specs=[_row_spec(tm), _const_spec((1, D_MODEL)), _mod_spec(), _const_spec((D_MODEL, QKV_WIDTH)),
                  _const_spec((1, LANES)), _const_spec((1, LANES)), tab_spec, tab_spec],
        out_specs=[head_spec(N_HEADS), head_spec(N_KV_HEADS),
                   pl.BlockSpec((1, N_KV_HEADS, VT_ROWS, tm), lambda b, i: (b, 0, 0, i))],
        out_shape=[jax.ShapeDtypeStruct((B, N_HEADS, T, HEAD_DIM), BF16),
                   jax.ShapeDtypeStruct((B, N_KV_HEADS, T, HEAD_DIM), BF16),
                   jax.ShapeDtypeStruct((B, N_KV_HEADS, VT_ROWS, T), BF16)],
        compiler_params=_params("parallel", "parallel"),
    )(x, g, mod, w_bf, qg, kg, cos_t, sin_t)


def _attn_kernel(*refs, n_src, tq):
    q_ref = refs[0]
    kv_refs = refs[1:1 + 2 * n_src]
    o_ref = refs[1 + 2 * n_src]
    s_scr, m_scr = refs[2 + 2 * n_src:]
    nq = GQA_REP * tq
    i = pl.program_id(2)
    pieces = []
    row = 0
    for s in range(n_src):
        keys_total = kv_refs[2 * s].shape[2]
        ck = min(ATTN_KV_CHUNK, keys_total)
        for r0 in range(0, keys_total, ck):
            pieces.append((kv_refs[2 * s], kv_refs[2 * s + 1], r0, ck, row))
            row += ck

    def run(with_pass2):
        q = q_ref[0].reshape(nq, HEAD_DIM)
        m8 = None
        if with_pass2:
            m_prev = m_scr[...]
            acc = jnp.zeros((VT_ROWS, nq), F32)
        for k_ref, vt_ref, r0, ck, c0 in pieces:
            if with_pass2:
                p = jnp.exp2(s_scr[c0:c0 + ck, :] - m_prev).astype(BF16)
                acc = acc + jnp.dot(vt_ref[0, 0, :, r0:r0 + ck], p, preferred_element_type=F32)
            s_blk = lax.dot_general(k_ref[0, 0, r0:r0 + ck, :], q, (((1,), (1,)), ((), ())),
                                    preferred_element_type=F32)
            s_scr[c0:c0 + ck, :] = s_blk
            bm = jnp.max(s_blk.reshape(ck // SUBLANES, SUBLANES, nq), axis=0)
            m8 = bm if m8 is None else jnp.maximum(m8, bm)
        m_scr[...] = jnp.max(m8, axis=0, keepdims=True)
        if with_pass2:
            o = (acc[:HEAD_DIM] / acc[HEAD_DIM:HEAD_DIM + 1]).T
            o_ref[0] = jnp.concatenate([o[h * tq:(h + 1) * tq] for h in range(GQA_REP)], axis=1).astype(BF16)

    pl.when(i == 0)(lambda: run(False))
    pl.when(i > 0)(lambda: run(True))


ATTN_KV_CHUNK = 1024


def _attention(q, kv_sources):
    B, _, T, _ = q.shape
    tq = min(T, 256)
    n_tiles = T // tq
    total = sum(k.shape[2] for k, _ in kv_sources)
    in_specs = [pl.BlockSpec((1, GQA_REP, tq, HEAD_DIM), lambda b, g, i: (b, g, jnp.minimum(i, n_tiles - 1), 0))]
    args = [q]
    for k, vt in kv_sources:
        in_specs += [pl.BlockSpec((1, 1, k.shape[2], HEAD_DIM), lambda b, g, i: (b, g, 0, 0)),
                     pl.BlockSpec((1, 1, VT_ROWS, k.shape[2]), lambda b, g, i: (b, g, 0, 0))]
        args += [k, vt]
    return pl.pallas_call(
        functools.partial(_attn_kernel, n_src=len(kv_sources), tq=tq),
        grid=(B, N_KV_HEADS, n_tiles + 1),
        in_specs=in_specs,
        out_specs=pl.BlockSpec((1, tq, GQA_REP * HEAD_DIM), lambda b, g, i: (b, jnp.maximum(i - 1, 0), g)),
        out_shape=jax.ShapeDtypeStruct((B, T, Q_WIDTH), BF16),
        scratch_shapes=[pltpu.VMEM((total, GQA_REP * tq), F32), pltpu.VMEM((1, GQA_REP * tq), F32)],
        compiler_params=_params("parallel", "parallel", "arbitrary"),
    )(*args)


def _proj_residual_kernel(x_ref, y_ref, mod_ref, w_ref, o_ref):
    out = jnp.dot(y_ref[0], w_ref[...], preferred_element_type=F32)
    o_ref[0] = x_ref[0] + mod_ref[0][2:3] * out


def _proj_residual(x, y, mod, w_bf):
    B, T, _ = x.shape
    tm = min(T, 512)
    return pl.pallas_call(
        _proj_residual_kernel,
        grid=(B, T // tm),
        in_specs=[_row_spec(tm), _row_spec(tm), _mod_spec(), _const_spec((D_MODEL, D_MODEL))],
        out_specs=_row_spec(tm),
        out_shape=jax.ShapeDtypeStruct(x.shape, F32),
        compiler_params=_params("parallel", "parallel"),
    )(x, y, mod, w_bf)


def _mlp_kernel(x_ref, g_ref, mod_ref, win_ref, wout_ref, o_ref, *, ff_chunk):
    x = x_ref[0]
    mod = mod_ref[0]
    h = _norm_mod(x, g_ref[...], mod, 3).astype(BF16)
    acc = jnp.zeros(x.shape, F32)
    for f0 in range(0, D_FF, ff_chunk):
        u = jnp.maximum(jnp.dot(h, win_ref[:, f0:f0 + ff_chunk], preferred_element_type=F32), 0.0)
        acc = acc + jnp.dot((u * u).astype(BF16), wout_ref[f0:f0 + ff_chunk, :], preferred_element_type=F32)
    o_ref[0] = x + mod[5:6] * acc


def _mlp(x, g, mod, win_bf, wout_bf):
    B, T, _ = x.shape
    tm = min(T, 512)
    return pl.pallas_call(
        functools.partial(_mlp_kernel, ff_chunk=1024),
        grid=(B, T // tm),
        in_specs=[_row_spec(tm), _const_spec((1, D_MODEL)), _mod_spec(),
                  _const_spec((D_MODEL, D_FF)), _const_spec((D_FF, D_MODEL))],
        out_specs=_row_spec(tm),
        out_shape=jax.ShapeDtypeStruct(x.shape, F32),
        compiler_params=_params("parallel", "parallel"),
    )(x, g, mod, win_bf, wout_bf)


DECAY_SCALE = float(np.exp(-0.5))


def _rwkv_prep_kernel(x_ref, xp_ref, xn_ref, g_ref, mod_ref, mu_ref, wrkv_ref, w1_ref, w2_ref, w0_ref, a1_ref, a2_ref,
                      a0_ref, g1_ref, g2_ref, kk_ref, ka_ref,
                      r_o, v_o, kk_o, g_o, lw0_o, lw1_o, key0_o, key1_o, a0_o, a1_o, *, tm):
    i = pl.program_id(1)
    last = pl.num_programs(1) - 1
    g = g_ref[...]
    mod = mod_ref[0]
    h = _norm_mod(x_ref[0], g, mod, 0)
    hp = _norm_mod(xp_ref[0][SUBLANES - 1:SUBLANES], g, mod, 0) * jnp.where(i == 0, 0.0, 1.0)
    hn = _norm_mod(xn_ref[0][0:1], g, mod, 0) * jnp.where(i == last, 0.0, 1.0)
    row = lax.broadcasted_iota(jnp.int32, (tm, 1), 0)
    prev = jnp.where(row == 0, hp, pltpu.roll(h, 1, 0))
    nxt = jnp.where(row == tm - 1, hn, pltpu.roll(h, tm - 1, 0))
    xx = 0.5 * (prev + nxt) - h
    mix = lambda m: (h + xx * mu_ref[m:m + 1]).astype(BF16)
    dot = functools.partial(jnp.dot, preferred_element_type=F32)
    r = dot(mix(0), wrkv_ref[0])
    k = dot(mix(2), wrkv_ref[1])
    v = dot(mix(3), wrkv_ref[2])
    r_o[0] = r.astype(BF16)
    v_o[0] = v.astype(BF16)
    g_o[0] = dot(jax.nn.sigmoid(dot(mix(5), g1_ref[...])).astype(BF16), g2_ref[...]).astype(BF16)
    kk = k * kk_ref[...]
    kk_o[0] = (kk / jnp.maximum(jnp.sqrt(_seg_sum_wide(kk * kk)), 1e-12)).astype(BF16)
    tw = jnp.tanh(dot(mix(1), w1_ref[...])).astype(BF16)
    ta = dot(mix(4), a1_ref[...]).astype(BF16)
    for d, (lw_o, key_o, a_o) in enumerate(((lw0_o, key0_o, a0_o), (lw1_o, key1_o, a1_o))):
        z = w0_ref[d:d + 1] + dot(tw[:, d * DECAY_LORA:(d + 1) * DECAY_LORA], w2_ref[d])
        lw_o[0] = -DECAY_SCALE * jax.nn.sigmoid(z)
        a = jax.nn.sigmoid(a0_ref[d:d + 1] + dot(ta[:, d * ICLR_LORA:(d + 1) * ICLR_LORA], a2_ref[d]))
        a_o[0] = a.astype(BF16)
        key_o[0] = (k * (1.0 + (a - 1.0) * ka_ref[...])).astype(BF16)


def _rwkv_prepare(x, g, mod, p):
    B, T, _ = x.shape
    tm = min(T, 256)
    nb = tm // SUBLANES
    n_blk8 = T // SUBLANES
    prev_spec = pl.BlockSpec((1, SUBLANES, D_MODEL), lambda b, i: (b, jnp.maximum(i * nb - 1, 0), 0))
    next_spec = pl.BlockSpec((1, SUBLANES, D_MODEL), lambda b, i: (b, jnp.minimum((i + 1) * nb, n_blk8 - 1), 0))
    o_bf, o_f32 = jax.ShapeDtypeStruct(x.shape, BF16), jax.ShapeDtypeStruct(x.shape, F32)
    return pl.pallas_call(
        functools.partial(_rwkv_prep_kernel, tm=tm),
        grid=(B, T // tm),
        in_specs=[_row_spec(tm), prev_spec, next_spec, _const_spec((1, D_MODEL)), _mod_spec(),
                  _const_spec((6, D_MODEL)), _const_spec((3, D_MODEL, D_MODEL)),
                  _const_spec((D_MODEL, 2 * DECAY_LORA)), _const_spec((2, DECAY_LORA, D_MODEL)),
                  _const_spec((2, D_MODEL)),
                  _const_spec((D_MODEL, 2 * ICLR_LORA)), _const_spec((2, ICLR_LORA, D_MODEL)),
                  _const_spec((2, D_MODEL)),
                  _const_spec((D_MODEL, GATE_LORA)), _const_spec((GATE_LORA, D_MODEL)),
                  _const_spec((1, D_MODEL)), _const_spec((1, D_MODEL))],
        out_specs=[_row_spec(tm)] * 10,
        out_shape=[o_bf, o_bf, o_bf, o_bf, o_f32, o_f32, o_bf, o_bf, o_bf, o_bf],
        compiler_params=_params("parallel", "parallel"),
    )(x, x, x, g, mod, p["mu"], p["w_rkv"], p["w1"], p["w2"], p["w0"], p["a1"], p["a2"], p["a0"], p["g1"], p["g2"],
      p["k_k"], p["k_a"])


def _wkv_kernel(r_ref, lw_ref, k_ref, v_ref, kk_ref, a_ref, s0_ref, y_ref, s_ref, state, *, n_chunks, n_pairs,
                reverse):
    C = CHUNK
    S2 = 2 * C
    def mm(dims):
        return lambda a, b: lax.dot_general(a.astype(BF16), b.astype(BF16), dims, preferred_element_type=F32)

    hp = mm((((1,), (0,)), ((), ())))
    nt = mm((((1,), (1,)), ((), ())))
    tn = mm((((0,), (0,)), ((), ())))

    @pl.when(pl.program_id(2) == 0)
    def _():
        state[...] = s0_ref[0]

    ti = lax.broadcasted_iota(jnp.int32, (C, C), 0)
    tj = lax.broadcasted_iota(jnp.int32, (C, C), 1)
    tri = jnp.where((ti <= tj) if reverse else (ti >= tj), 1.0, 0.0).astype(BF16)
    si = lax.broadcasted_iota(jnp.int32, (S2, S2), 0)
    sj = lax.broadcasted_iota(jnp.int32, (S2, S2), 1)
    strict = (si < sj) if reverse else (si > sj)
    incl = (si <= sj) if reverse else (si >= sj)
    blk = (si // INV_BLOCK) == (sj // INV_BLOCK)
    eye = si == sj
    ident = jnp.where(eye, 1.0, 0.0)
    head0 = lax.broadcasted_iota(jnp.int32, (1, LANES), 1) < RWKV_HEAD
    stack = lambda z: jnp.concatenate([jnp.where(head0, z, 0.0), jnp.where(head0, 0.0, z)], axis=0)
    unstack = lambda z: z[:C] + z[C:]

    each = lambda f, *ls: [f(*xs) for xs in zip(*ls)]
    units = [(slice(c * C, (c + 1) * C), slice(p * LANES, (p + 1) * LANES))
             for p in range(n_pairs) for c in range(n_chunks)]
    load = lambda ref: [ref[0, rows, cols].astype(F32) for rows, cols in units]
    r, lw, k, v, kk, a = (load(ref) for ref in (r_ref, lw_ref, k_ref, v_ref, kk_ref, a_ref))

    def cumsum(x):
        x1 = x.astype(BF16)
        x2 = (x - x1.astype(F32)).astype(BF16)
        x3 = (x - x1.astype(F32) - x2.astype(F32)).astype(BF16)
        out = jnp.dot(tri, jnp.concatenate([x1, x2, x3], axis=1), preferred_element_type=F32)
        return out[:, :LANES] + out[:, LANES:2 * LANES] + out[:, 2 * LANES:]

    cum = each(cumsum, lw)
    total = each(lambda z: z[0:1] if reverse else z[C - 1:C], cum)
    g_in = each(lambda z: jnp.exp(-z), cum)
    g_to_end = each(lambda t, z: jnp.exp(t - z), total, cum)
    b = each(lambda x, y: x * y, kk, a)
    a_s = each(lambda x, z, l: stack(x * jnp.exp(z - l)), kk, cum, lw)
    b_s = each(lambda x, y: stack(x * y), b, g_in)
    k_s = each(lambda x, y: stack(x * y), k, g_in)
    r_s = each(lambda x, z: stack(x * jnp.exp(z)), r, cum)
    v_s = each(stack, v)
    bh_s = each(lambda x, y: stack(x * y), b, g_to_end)
    kh_s = each(lambda x, y: stack(x * y), k, g_to_end)
    big = each(lambda a_, r_, b_, k_: nt(jnp.concatenate([a_, r_], axis=0), jnp.concatenate([b_, k_], axis=0)),
               a_s, r_s, b_s, k_s)
    l_ab = each(lambda z: jnp.where(strict, z[:S2, :S2], 0.0), big)
    l_ak = each(lambda z: jnp.where(strict, z[:S2, S2:], 0.0), big)
    p_rb = each(lambda z: jnp.where(incl, z[S2:, :S2], 0.0), big)
    p_rk = each(lambda z: jnp.where(incl, z[S2:, S2:], 0.0), big)
    l_d = each(lambda z: jnp.where(blk, z, 0.0), l_ab)
    l_o = each(lambda x, y: x - y, l_ab, l_d)
    side = lambda x, y: jnp.concatenate([x, y], axis=1)
    l2 = each(hp, l_d, l_d)
    lv = each(hp, l_ak, v_s)
    prv = each(hp, p_rk, v_s)
    khv = each(tn, kh_s, v_s)
    l34 = each(lambda s2, d: hp(s2, side(d, s2)), l2, l_d)
    p1 = each(lambda d, s2, s34: ident - d + s2 - s34[:, :S2], l_d, l2, l34)
    x48 = each(lambda s34, p: hp(s34[:, S2:], side(p, s34[:, S2:])), l34, p1)
    p2 = each(lambda p, x: p + x[:, :S2], p1, x48)
    t_d = each(lambda p, x: p + hp(x[:, S2:], p), p2, x48)
    m1 = each(hp, t_d, l_o)
    mm = each(lambda x, t: hp(x, side(x, t)), m1, t_d)
    y_d = each(lambda t, x: t - x[:, S2:], t_d, mm)
    t_inv = each(lambda y, x: y + hp(x[:, :S2], y), y_d, mm)
    w = each(lambda t, x, y: hp(t, jnp.concatenate([x, y], axis=1)), t_inv, a_s, lv)
    pw = each(hp, p_rb, w)
    bw = each(tn, bh_s, w)
    q1 = each(lambda x, y: unstack(x - y[:, :S2]), r_s, pw)
    y2 = each(lambda x, y: unstack(x - y[:, S2:]), prv, pw)
    g_mat = each(lambda t, y: jnp.where(eye, jnp.exp(t), 0.0) - y[:, :S2], total, bw)
    h_mat = each(lambda x, y: x - y[:, S2:], khv, bw)
    pre = list(zip(q1, y2, g_mat, h_mat))

    s = [state[p] for p in range(n_pairs)]
    order = range(n_chunks - 1, -1, -1) if reverse else range(n_chunks)
    for c in order:
        for p in range(n_pairs):
            q1, y2, g_mat, h_mat = pre[p * n_chunks + c]
            y_ref[0, c * C:(c + 1) * C, p * LANES:(p + 1) * LANES] = hp(q1, s[p]) + y2
            s[p] = hp(g_mat, s[p]) + h_mat
    for p in range(n_pairs):
        state[p] = s[p]

    @pl.when(pl.program_id(2) == pl.num_programs(2) - 1)
    def _():
        for p in range(n_pairs):
            s_ref[0, p] = s[p]


WKV_PAIRS = 4
WKV_CHUNKS = 4


def _wkv(r, lw, k, v, kk, a, s0, reverse):
    B, T, _ = r.shape
    rows = min(T, WKV_CHUNKS * CHUNK)
    n_steps = T // rows
    width = WKV_PAIRS * LANES
    tmap = (lambda b, p, j: (b, n_steps - 1 - j, p)) if reverse else (lambda b, p, j: (b, j, p))
    seq_spec = pl.BlockSpec((1, rows, width), tmap)
    st_spec = pl.BlockSpec((1, WKV_PAIRS, LANES, LANES), lambda b, p, j: (b, p, 0, 0))
    return pl.pallas_call(
        functools.partial(_wkv_kernel, n_chunks=rows // CHUNK, n_pairs=WKV_PAIRS, reverse=reverse),
        grid=(B, PAIRS // WKV_PAIRS, n_steps),
        in_specs=[seq_spec] * 6 + [st_spec],
        out_specs=[seq_spec, st_spec],
        out_shape=[jax.ShapeDtypeStruct(r.shape, F32), jax.ShapeDtypeStruct((B, PAIRS, LANES, LANES), F32)],
        scratch_shapes=[pltpu.VMEM((WKV_PAIRS, LANES, LANES), F32)],
        compiler_params=_params("parallel", "parallel", "arbitrary"),
    )(r, lw, k, v, kk, a, s0)


def _rwkv_finish_kernel(x_ref, y0_ref, y1_ref, r_ref, v_ref, k0_ref, k1_ref, g_ref, mod_ref, rk_ref, lng_ref, lnb_ref,
                        wo_ref, o_ref):
    y = y0_ref[0] + y1_ref[0]
    inv_n = 1.0 / RWKV_HEAD
    mean = _seg_sum_wide(y) * inv_n
    yc = y - mean
    var = _seg_sum_wide(yc * yc) * inv_n
    yn = yc * lax.rsqrt(var + GN_EPS) * lng_ref[...] + lnb_ref[...]
    f32 = lambda ref: ref[0].astype(F32)
    bonus = _seg_sum_wide(f32(r_ref) * (f32(k0_ref) + f32(k1_ref)) * rk_ref[...]) * f32(v_ref)
    out = jnp.dot(((yn + bonus) * f32(g_ref)).astype(BF16), wo_ref[...], preferred_element_type=F32)
    o_ref[0] = x_ref[0] + mod_ref[0][2:3] * out


def _rwkv_finish(x, y0, y1, r, v, k0, k1, gate, mod, p):
    B, T, _ = x.shape
    tm = min(T, 256)
    vec = _const_spec((1, D_MODEL))
    return pl.pallas_call(
        _rwkv_finish_kernel,
        grid=(B, T // tm),
        in_specs=[_row_spec(tm)] * 8 + [_mod_spec(), vec, vec, vec, _const_spec((D_MODEL, D_MODEL))],
        out_specs=_row_spec(tm),
        out_shape=jax.ShapeDtypeStruct(x.shape, F32),
        compiler_params=_params("parallel", "parallel"),
    )(x, y0, y1, r, v, k0, k1, gate, mod, p["r_k"], p["ln_g"], p["ln_b"], p["w_o"])


HALO = SUBLANES


def _pool_kernel(x_ref, xp_ref, xn_ref, g_ref, mod_ref, w_ref, sc_ref, o_ref, *, tm, seq_len):
    i = pl.program_id(1)
    last = pl.num_programs(1) - 1
    g = g_ref[...]
    mod = mod_ref[0]
    x = x_ref[0]
    h = _norm_mod(x, g, mod, 0)
    hp = _norm_mod(xp_ref[0], g, mod, 0) * jnp.where(i == 0, 0.0, 1.0)
    hn = _norm_mod(xn_ref[0], g, mod, 0) * jnp.where(i == last, 0.0, 1.0)
    ext = jnp.concatenate([hp, h, hn], axis=0)
    n_ext = tm + 2 * HALO
    t = i * tm + lax.broadcasted_iota(jnp.int32, (tm, 1), 0)
    outs = []
    for gi, win in enumerate(POOL_WINDOWS):
        e = ext[:, gi * POOL_GROUP:(gi + 1) * POOL_GROUP]
        acc = e + pltpu.roll(e, 1, 0)
        step = 1
        while 2 * step < win:
            acc = pltpu.roll(acc, step, 0) + pltpu.roll(acc, n_ext - step, 0)
            step *= 2
        half = win // 2
        cnt = (jnp.minimum(t + half, seq_len) - jnp.maximum(t - half, 0)).astype(F32)
        pooled = acc[HALO:HALO + tm] / cnt - e[HALO:HALO + tm]
        outs.append(jnp.dot(pooled.astype(BF16), w_ref[gi], preferred_element_type=F32))
    y = jnp.concatenate(outs, axis=1) * sc_ref[...]
    o_ref[0] = x + mod[2:3] * y


def _pool_mixer(x, g, mod, w_bf, scale):
    B, T, _ = x.shape
    tm = min(T, 256)
    nb = tm // HALO
    n_blk = T // HALO
    prev_spec = pl.BlockSpec((1, HALO, D_MODEL), lambda b, i: (b, jnp.maximum(i * nb - 1, 0), 0))
    next_spec = pl.BlockSpec((1, HALO, D_MODEL), lambda b, i: (b, jnp.minimum((i + 1) * nb, n_blk - 1), 0))
    n_grp = len(POOL_WINDOWS)
    return pl.pallas_call(
        functools.partial(_pool_kernel, tm=tm, seq_len=T),
        grid=(B, T // tm),
        in_specs=[_row_spec(tm), prev_spec, next_spec, _const_spec((1, D_MODEL)), _mod_spec(),
                  _const_spec((n_grp, POOL_GROUP, POOL_GROUP)), _const_spec((1, D_MODEL))],
        out_specs=_row_spec(tm),
        out_shape=jax.ShapeDtypeStruct(x.shape, F32),
        compiler_params=_params("parallel", "parallel"),
    )(x, x, x, g, mod, w_bf, scale)


def _rope_tables(n_tokens):
    rows = n_tokens // GRID_W
    n_freq = HEAD_DIM // 4
    inv = ROPE_THETA ** (-jnp.arange(n_freq, dtype=F32) / n_freq)
    ang_r = jnp.arange(rows, dtype=F32)[:, None] * inv
    ang_c = jnp.arange(GRID_W, dtype=F32)[:, None] * inv
    ang = jnp.concatenate([
        jnp.broadcast_to(ang_r[:, None, :], (rows, GRID_W, n_freq)),
        jnp.broadcast_to(ang_c[None, :, :], (rows, GRID_W, n_freq))], axis=-1).reshape(rows * GRID_W, 2 * n_freq)
    cos, sin = jnp.cos(ang), jnp.sin(ang)
    return jnp.tile(cos, (1, 4)), jnp.tile(jnp.concatenate([-sin, sin], axis=-1), (1, 2))


def kernel(x, c, ctx, c_ctx, w_mod, b_mod, norm1_g, norm2_g, mlp_w_in, mlp_w_out, attn_w_qkv, attn_q_gain, attn_k_gain, attn_w_o, rwkv_mu, rwkv_w_rkv, rwkv_w0, rwkv_w1, rwkv_w2, rwkv_a0, rwkv_a1, rwkv_a2, rwkv_g1, rwkv_g2, rwkv_k_k, rwkv_k_a, rwkv_r_k, rwkv_ln_g, rwkv_ln_b, rwkv_w_o, pool_w, pool_scale):
    B, S, _ = x.shape
    L = ctx.shape[1]
    depth = w_mod.shape[0]
    assert x.shape[2] == D_MODEL and S % (4 * CHUNK) == 0 and L % CHUNK == 0 and S % GRID_W == 0

    n_rows = -(-(B + 1) // SUBLANES) * SUBLANES
    cvec = jnp.concatenate([c, c_ctx[None], jnp.zeros((n_rows - B - 1, D_MODEL), F32)], axis=0)
    mod_all = _modulation(cvec, w_mod, b_mod).reshape(depth, n_rows, N_MOD, D_MODEL)
    cos_t, sin_t = _rope_tables(S)
    zero_tab = jnp.zeros((L, LANES), F32)
    row = lambda a: a.reshape(1, -1)

    for i in range(depth):
        last = i == depth - 1
        j = i // N_MIXERS
        mod_l = mod_all[i, :B]
        mod_c = jnp.broadcast_to(mod_all[i, B][None], (B, N_MOD, D_MODEL))
        g1 = row(norm1_g[i])
        kind = i % N_MIXERS
        if kind == 0:
            w_qkv = attn_w_qkv[j].astype(BF16)
            w_o = attn_w_o[j].astype(BF16)
            qg = jnp.tile(row(attn_q_gain[j]), (1, 2))
            kg = jnp.tile(row(attn_k_gain[j]), (1, 2))
            q_l, k_l, v_l = _qkv_project(x, g1, mod_l, w_qkv, qg, kg, cos_t, sin_t, True)
            q_c, k_c, v_c = _qkv_project(ctx, g1, mod_c, w_qkv, qg, kg, zero_tab, zero_tab, False)
            o_l = _attention(q_l, [(k_l, v_l), (k_c, v_c)])
            x = _proj_residual(x, o_l, mod_l, w_o)
            if not last:
                o_c = _attention(q_c, [(k_c, v_c)])
                ctx = _proj_residual(ctx, o_c, mod_c, w_o)
        elif kind == 1:
            p = {
                "mu": rwkv_mu[j], "w_rkv": rwkv_w_rkv[j].astype(BF16),
                "w1": jnp.concatenate([rwkv_w1[j, 0], rwkv_w1[j, 1]], axis=1).astype(BF16),
                "w2": rwkv_w2[j].astype(BF16), "w0": rwkv_w0[j],
                "a1": jnp.concatenate([rwkv_a1[j, 0], rwkv_a1[j, 1]], axis=1).astype(BF16),
                "a2": rwkv_a2[j].astype(BF16), "a0": rwkv_a0[j],
                "g1": rwkv_g1[j].astype(BF16), "g2": rwkv_g2[j].astype(BF16),
                "k_k": row(rwkv_k_k[j]), "k_a": row(rwkv_k_a[j]), "r_k": row(rwkv_r_k[j]),
                "ln_g": row(rwkv_ln_g[j]), "ln_b": row(rwkv_ln_b[j]), "w_o": rwkv_w_o[j].astype(BF16),
            }
            r_l, v_l, kk_l, g_l, lw0_l, lw1_l, key0_l, key1_l, a0_l, a1_l = _rwkv_prepare(x, g1, mod_l, p)
            r_c, v_c, kk_c, g_c, lw0_c, lw1_c, key0_c, key1_c, a0_c, a1_c = _rwkv_prepare(ctx, g1, mod_c, p)
            zero_state = jnp.zeros((B, PAIRS, LANES, LANES), F32)
            ys_l, ys_c = [], []
            for rev, (lw_l, key_l, a_l, lw_c, key_c, a_c) in enumerate(
                    ((lw0_l, key0_l, a0_l, lw0_c, key0_c, a0_c), (lw1_l, key1_l, a1_l, lw1_c, key1_c, a1_c))):
                y_c, s_ctx = _wkv(r_c, lw_c, key_c, v_c, kk_c, a_c, zero_state, bool(rev))
                y_l, _ = _wkv(r_l, lw_l, key_l, v_l, kk_l, a_l, s_ctx, bool(rev))
                ys_l.append(y_l)
                ys_c.append(y_c)
            x = _rwkv_finish(x, ys_l[0], ys_l[1], r_l, v_l, key0_l, key1_l, g_l, mod_l, p)
            if not last:
                ctx = _rwkv_finish(ctx, ys_c[0], ys_c[1], r_c, v_c, key0_c, key1_c, g_c, mod_c, p)
        else:
            w_p = pool_w[j].astype(BF16)
            sc = row(pool_scale[j])
            x = _pool_mixer(x, g1, mod_l, w_p, sc)
            if not last:
                ctx = _pool_mixer(ctx, g1, mod_c, w_p, sc)
        g2 = row(norm2_g[i])
        w_in = mlp_w_in[i].astype(BF16)
        w_out = mlp_w_out[i].astype(BF16)
        x = _mlp(x, g2, mod_l, w_in, w_out)
        if not last:
            ctx = _mlp(ctx, g2, mod_c, w_in, w_out)
    return x
```

```python
import functools

import jax
import jax.numpy as jnp
import numpy as np
from jax import lax
from jax.experimental import pallas as pl
from jax.experimental.pallas import tpu as pltpu

F32 = jnp.float32
BF16 = jnp.bfloat16

D_MODEL = 1024
GRID_W = 64
N_MIXERS = 3
N_MOD = 6
EPS = 1e-6
N_HEADS = 16
N_KV_HEADS = 4
HEAD_DIM = 64
GQA_REP = N_HEADS // N_KV_HEADS
Q_WIDTH = N_HEADS * HEAD_DIM
KV_WIDTH = N_KV_HEADS * HEAD_DIM
QKV_WIDTH = Q_WIDTH + 2 * KV_WIDTH
ROPE_THETA = 10000.0
RWKV_HEAD = 64
DECAY_LORA = 64
ICLR_LORA = 64
GATE_LORA = 160
GN_EPS = RWKV_HEAD * 1e-5
POOL_WINDOWS = (2, 4, 8, 16)
POOL_GROUP = D_MODEL // len(POOL_WINDOWS)
D_FF = 4 * D_MODEL

LANES = 128
SUBLANES = 8
PAIRS = D_MODEL // LANES
CHUNK = 64
INV_BLOCK = 16
VMEM_LIMIT = 56 * 1024 * 1024

HIGHEST = lax.Precision.HIGHEST


def _params(*sem):
    return pltpu.CompilerParams(dimension_semantics=sem, vmem_limit_bytes=VMEM_LIMIT)


def _const_spec(shape):
    zeros = (0,) * len(shape)
    return pl.BlockSpec(shape, lambda *_: zeros, pipeline_mode=pl.Buffered(1))


def _row_spec(tm, width=D_MODEL):
    return pl.BlockSpec((1, tm, width), lambda b, i: (b, i, 0))


def _mod_spec():
    return pl.BlockSpec((1, N_MOD, D_MODEL), lambda b, i: (b, 0, 0))


def _norm_mod(x, g, mod, k):
    y = x * lax.rsqrt(jnp.mean(x * x, axis=-1, keepdims=True) + EPS) * g
    return y * (1.0 + mod[k + 1:k + 2]) + mod[k:k + 1]


def _seg_sum(x):
    lo = lax.broadcasted_iota(jnp.int32, (1, LANES), 1) < HEAD_DIM
    s_lo = jnp.sum(jnp.where(lo, x, 0.0), axis=-1, keepdims=True)
    s_hi = jnp.sum(jnp.where(lo, 0.0, x), axis=-1, keepdims=True)
    return jnp.where(lo, s_lo, s_hi)


def _seg_sum_wide(x):
    return jnp.concatenate([_seg_sum(x[:, p * LANES:(p + 1) * LANES]) for p in range(x.shape[1] // LANES)], axis=1)


def _mod_kernel(c_ref, w_ref, b_ref, o_ref):
    c = c_ref[...]
    s = c * jax.nn.sigmoid(c)
    o_ref[0] = jnp.dot(s, w_ref[0], precision=HIGHEST, preferred_element_type=F32) + b_ref[0]


def _modulation(cvec, w_mod, b_mod):
    depth = w_mod.shape[0]
    rows = cvec.shape[0]
    tn = 1536
    return pl.pallas_call(
        _mod_kernel,
        grid=(depth, N_MOD * D_MODEL // tn),
        in_specs=[pl.BlockSpec((rows, D_MODEL), lambda l, j: (0, 0)),
                  pl.BlockSpec((1, D_MODEL, tn), lambda l, j: (l, 0, j)),
                  pl.BlockSpec((1, 1, tn), lambda l, j: (l, 0, j))],
        out_specs=pl.BlockSpec((1, rows, tn), lambda l, j: (l, 0, j)),
        out_shape=jax.ShapeDtypeStruct((depth, rows, N_MOD * D_MODEL), F32),
        compiler_params=_params("parallel", "parallel"),
    )(cvec, w_mod, b_mod.reshape(depth, 1, N_MOD * D_MODEL))


VT_ROWS = HEAD_DIM + 16
Q_SCALE = float(HEAD_DIM ** -0.5 * np.log2(np.e))


def _qkv_kernel(x_ref, g_ref, mod_ref, w_ref, qg_ref, kg_ref, cos_ref, sin_ref, q_ref, k_ref, v_ref, *, use_rope):
    h = _norm_mod(x_ref[0], g_ref[...], mod_ref[0], 0).astype(BF16)
    acc = jnp.dot(h, w_ref[...], preferred_element_type=F32)
    lane = lax.broadcasted_iota(jnp.int32, (1, LANES), 1)
    first_half = (lane % HEAD_DIM) < (HEAD_DIM // 2)

    def head_pair(xp, gain, scale):
        y = xp * lax.rsqrt(_seg_sum(xp * xp) * (1.0 / HEAD_DIM) + EPS) * gain
        if use_rope:
            rot = jnp.where(first_half, pltpu.roll(y, LANES - HEAD_DIM // 2, 1), pltpu.roll(y, HEAD_DIM // 2, 1))
            y = y * cos_ref[...] + rot * sin_ref[...]
        return (y * scale).astype(BF16)

    for p in range(Q_WIDTH // LANES):
        y = head_pair(acc[:, p * LANES:(p + 1) * LANES], qg_ref[...], Q_SCALE)
        q_ref[0, 2 * p] = y[:, :HEAD_DIM]
        q_ref[0, 2 * p + 1] = y[:, HEAD_DIM:]
    for p in range(KV_WIDTH // LANES):
        y = head_pair(acc[:, Q_WIDTH + p * LANES:Q_WIDTH + (p + 1) * LANES], kg_ref[...], 1.0)
        k_ref[0, 2 * p] = y[:, :HEAD_DIM]
        k_ref[0, 2 * p + 1] = y[:, HEAD_DIM:]
    ones = jnp.ones((VT_ROWS - HEAD_DIM, acc.shape[0]), F32)
    for p in range(KV_WIDTH // LANES):
        c0 = Q_WIDTH + KV_WIDTH + p * LANES
        vt = acc[:, c0:c0 + LANES].T
        v_ref[0, 2 * p] = jnp.concatenate([vt[:HEAD_DIM], ones], axis=0).astype(BF16)
        v_ref[0, 2 * p + 1] = jnp.concatenate([vt[HEAD_DIM:], ones], axis=0).astype(BF16)


def _qkv_project(x, g, mod, w_bf, qg, kg, cos_t, sin_t, use_rope):
    B, T, _ = x.shape
    tm = min(T, 512)
    tab_spec = pl.BlockSpec((tm, LANES), lambda b, i: (i, 0))
    head_spec = lambda n: pl.BlockSpec((1, n, tm, HEAD_DIM), lambda b, i: (b, 0, i, 0))
    return pl.pallas_call(
        functools.partial(_qkv_kernel, use_rope=use_rope),
        grid=(B, T // tm),
        in_specs=[_row_spec(tm), _const_spec((1, D_MODEL)), _mod_spec(), _const_spec((D_MODEL, QKV_WIDTH)),
                  _const_spec((1, LANES)), _const_spec((1, LANES)), tab_spec, tab_spec],
        out_specs=[head_spec(N_HEADS), head_spec(N_KV_HEADS),
                   pl.BlockSpec((1, N_KV_HEADS, VT_ROWS, tm), lambda b, i: (b, 0, 0, i))],
        out_shape=[jax.ShapeDtypeStruct((B, N_HEADS, T, HEAD_DIM), BF16),
                   jax.ShapeDtypeStruct((B, N_KV_HEADS, T, HEAD_DIM), BF16),
                   jax.ShapeDtypeStruct((B, N_KV_HEADS, VT_ROWS, T), BF16)],
        compiler_params=_params("parallel", "parallel"),
    )(x, g, mod, w_bf, qg, kg, cos_t, sin_t)


def _attn_kernel(*refs, n_src, tq):
    q_ref = refs[0]
    kv_refs = refs[1:1 + 2 * n_src]
    o_ref = refs[1 + 2 * n_src]
    s_scr, m_scr = refs[2 + 2 * n_src:]
    nq = GQA_REP * tq
    i = pl.program_id(2)
    pieces = []
    row = 0
    for s in range(n_src):
        keys_total = kv_refs[2 * s].shape[2]
        ck = min(ATTN_KV_CHUNK, keys_total)
        for r0 in range(0, keys_total, ck):
            pieces.append((kv_refs[2 * s], kv_refs[2 * s + 1], r0, ck, row))
            row += ck

    def run(with_pass2):
        q = q_ref[0].reshape(nq, HEAD_DIM)
        m8 = None
        if with_pass2:
            m_prev = m_scr[...]
            acc = jnp.zeros((VT_ROWS, nq), F32)
        for k_ref, vt_ref, r0, ck, c0 in pieces:
            if with_pass2:
                p = jnp.exp2(s_scr[c0:c0 + ck, :] - m_prev).astype(BF16)
                acc = acc + jnp.dot(vt_ref[0, 0, :, r0:r0 + ck], p, preferred_element_type=F32)
            s_blk = lax.dot_general(k_ref[0, 0, r0:r0 + ck, :], q, (((1,), (1,)), ((), ())),
                                    preferred_element_type=F32)
            s_scr[c0:c0 + ck, :] = s_blk
            bm = jnp.max(s_blk.reshape(ck // SUBLANES, SUBLANES, nq), axis=0)
            m8 = bm if m8 is None else jnp.maximum(m8, bm)
        m_scr[...] = jnp.max(m8, axis=0, keepdims=True)
        if with_pass2:
            o = (acc[:HEAD_DIM] / acc[HEAD_DIM:HEAD_DIM + 1]).T
            o_ref[0] = jnp.concatenate([o[h * tq:(h + 1) * tq] for h in range(GQA_REP)], axis=1).astype(BF16)

    pl.when(i == 0)(lambda: run(False))
    pl.when(i > 0)(lambda: run(True))


ATTN_KV_CHUNK = 256


def _attention(q, kv_sources):
    B, _, T, _ = q.shape
    tq = min(T, 256)
    n_tiles = T // tq
    total = sum(k.shape[2] for k, _ in kv_sources)
    in_specs = [pl.BlockSpec((1, GQA_REP, tq, HEAD_DIM), lambda b, g, i: (b, g, jnp.minimum(i, n_tiles - 1), 0))]
    args = [q]
    for k, vt in kv_sources:
        in_specs += [pl.BlockSpec((1, 1, k.shape[2], HEAD_DIM), lambda b, g, i: (b, g, 0, 0)),
                     pl.BlockSpec((1, 1, VT_ROWS, k.shape[2]), lambda b, g, i: (b, g, 0, 0))]
        args += [k, vt]
    return pl.pallas_call(
        functools.partial(_attn_kernel, n_src=len(kv_sources), tq=tq),
        grid=(B, N_KV_HEADS, n_tiles + 1),
        in_specs=in_specs,
        out_specs=pl.BlockSpec((1, tq, GQA_REP * HEAD_DIM), lambda b, g, i: (b, jnp.maximum(i - 1, 0), g)),
        out_shape=jax.ShapeDtypeStruct((B, T, Q_WIDTH), BF16),
        scratch_shapes=[pltpu.VMEM((total, GQA_REP * tq), F32), pltpu.VMEM((1, GQA_REP * tq), F32)],
        compiler_params=_params("parallel", "parallel", "arbitrary"),
    )(*args)


def _proj_residual_kernel(x_ref, y_ref, mod_ref, w_ref, o_ref):
    out = jnp.dot(y_ref[0], w_ref[...], preferred_element_type=F32)
    o_ref[0] = x_ref[0] + mod_ref[0][2:3] * out


def _proj_residual(x, y, mod, w_bf):
    B, T, _ = x.shape
    tm = min(T, 512)
    return pl.pallas_call(
        _proj_residual_kernel,
        grid=(B, T // tm),
        in_specs=[_row_spec(tm), _row_spec(tm), _mod_spec(), _const_spec((D_MODEL, D_MODEL))],
        out_specs=_row_spec(tm),
        out_shape=jax.ShapeDtypeStruct(x.shape, F32),
        compiler_params=_params("parallel", "parallel"),
    )(x, y, mod, w_bf)


def _mlp_kernel(x_ref, g_ref, mod_ref, win_ref, wout_ref, o_ref, *, ff_chunk):
    x = x_ref[0]
    mod = mod_ref[0]
    h = _norm_mod(x, g_ref[...], mod, 3).astype(BF16)
    acc = jnp.zeros(x.shape, F32)
    for f0 in range(0, D_FF, ff_chunk):
        u = jnp.maximum(jnp.dot(h, win_ref[:, f0:f0 + ff_chunk], preferred_element_type=F32), 0.0)
        acc = acc + jnp.dot((u * u).astype(BF16), wout_ref[f0:f0 + ff_chunk, :], preferred_element_type=F32)
    o_ref[0] = x + mod[5:6] * acc


def _mlp(x, g, mod, win_bf, wout_bf):
    B, T, _ = x.shape
    tm = min(T, 512)
    return pl.pallas_call(
        functools.partial(_mlp_kernel, ff_chunk=1024),
        grid=(B, T // tm),
        in_specs=[_row_spec(tm), _const_spec((1, D_MODEL)), _mod_spec(),
                  _const_spec((D_MODEL, D_FF)), _const_spec((D_FF, D_MODEL))],
        out_specs=_row_spec(tm),
        out_shape=jax.ShapeDtypeStruct(x.shape, F32),
        compiler_params=_params("parallel", "parallel"),
    )(x, g, mod, win_bf, wout_bf)


DECAY_SCALE = float(np.exp(-0.5))


def _rwkv_prep_kernel(x_ref, xp_ref, xn_ref, g_ref, mod_ref, mu_ref, wrkv_ref, w1_ref, w2_ref, w0_ref, a1_ref, a2_ref,
                      a0_ref, g1_ref, g2_ref, kk_ref, ka_ref,
                      r_o, v_o, kk_o, g_o, lw0_o, lw1_o, key0_o, key1_o, a0_o, a1_o, *, tm):
    i = pl.program_id(1)
    last = pl.num_programs(1) - 1
    g = g_ref[...]
    mod = mod_ref[0]
    h = _norm_mod(x_ref[0], g, mod, 0)
    hp = _norm_mod(xp_ref[0][SUBLANES - 1:SUBLANES], g, mod, 0) * jnp.where(i == 0, 0.0, 1.0)
    hn = _norm_mod(xn_ref[0][0:1], g, mod, 0) * jnp.where(i == last, 0.0, 1.0)
    row = lax.broadcasted_iota(jnp.int32, (tm, 1), 0)
    prev = jnp.where(row == 0, hp, pltpu.roll(h, 1, 0))
    nxt = jnp.where(row == tm - 1, hn, pltpu.roll(h, tm - 1, 0))
    xx = 0.5 * (prev + nxt) - h
    mix = lambda m: (h + xx * mu_ref[m:m + 1]).astype(BF16)
    dot = functools.partial(jnp.dot, preferred_element_type=F32)
    r = dot(mix(0), wrkv_ref[0])
    k = dot(mix(2), wrkv_ref[1])
    v = dot(mix(3), wrkv_ref[2])
    r_o[0] = r.astype(BF16)
    v_o[0] = v.astype(BF16)
    g_o[0] = dot(jax.nn.sigmoid(dot(mix(5), g1_ref[...])).astype(BF16), g2_ref[...]).astype(BF16)
    kk = k * kk_ref[...]
    kk_o[0] = (kk / jnp.maximum(jnp.sqrt(_seg_sum_wide(kk * kk)), 1e-12)).astype(BF16)
    tw = jnp.tanh(dot(mix(1), w1_ref[...])).astype(BF16)
    ta = dot(mix(4), a1_ref[...]).astype(BF16)
    for d, (lw_o, key_o, a_o) in enumerate(((lw0_o, key0_o, a0_o), (lw1_o, key1_o, a1_o))):
        z = w0_ref[d:d + 1] + dot(tw[:, d * DECAY_LORA:(d + 1) * DECAY_LORA], w2_ref[d])
        lw_o[0] = -DECAY_SCALE * jax.nn.sigmoid(z)
        a = jax.nn.sigmoid(a0_ref[d:d + 1] + dot(ta[:, d * ICLR_LORA:(d + 1) * ICLR_LORA], a2_ref[d]))
        a_o[0] = a.astype(BF16)
        key_o[0] = (k * (1.0 + (a - 1.0) * ka_ref[...])).astype(BF16)


def _rwkv_prepare(x, g, mod, p):
    B, T, _ = x.shape
    tm = min(T, 256)
    nb = tm // SUBLANES
    n_blk8 = T // SUBLANES
    prev_spec = pl.BlockSpec((1, SUBLANES, D_MODEL), lambda b, i: (b, jnp.maximum(i * nb - 1, 0), 0))
    next_spec = pl.BlockSpec((1, SUBLANES, D_MODEL), lambda b, i: (b, jnp.minimum((i + 1) * nb, n_blk8 - 1), 0))
    o_bf, o_f32 = jax.ShapeDtypeStruct(x.shape, BF16), jax.ShapeDtypeStruct(x.shape, F32)
    return pl.pallas_call(
        functools.partial(_rwkv_prep_kernel, tm=tm),
        grid=(B, T // tm),
        in_specs=[_row_spec(tm), prev_spec, next_spec, _const_spec((1, D_MODEL)), _mod_spec(),
                  _const_spec((6, D_MODEL)), _const_spec((3, D_MODEL, D_MODEL)),
                  _const_spec((D_MODEL, 2 * DECAY_LORA)), _const_spec((2, DECAY_LORA, D_MODEL)),
                  _const_spec((2, D_MODEL)),
                  _const_spec((D_MODEL, 2 * ICLR_LORA)), _const_spec((2, ICLR_LORA, D_MODEL)),
                  _const_spec((2, D_MODEL)),
                  _const_spec((D_MODEL, GATE_LORA)), _const_spec((GATE_LORA, D_MODEL)),
                  _const_spec((1, D_MODEL)), _const_spec((1, D_MODEL))],
        out_specs=[_row_spec(tm)] * 10,
        out_shape=[o_bf, o_bf, o_bf, o_bf, o_f32, o_f32, o_bf, o_bf, o_bf, o_bf],
        compiler_params=_params("parallel", "parallel"),
    )(x, x, x, g, mod, p["mu"], p["w_rkv"], p["w1"], p["w2"], p["w0"], p["a1"], p["a2"], p["a0"], p["g1"], p["g2"],
      p["k_k"], p["k_a"])


def _wkv_kernel(r_ref, lw_ref, k_ref, v_ref, kk_ref, a_ref, s0_ref, y_ref, s_ref, state, *, n_chunks, n_pairs,
                reverse):
    C = CHUNK

    def mm(dims):
        return lambda a, b: lax.dot_general(a.astype(BF16), b.astype(BF16), dims, preferred_element_type=F32)

    hp = mm((((1,), (0,)), ((), ())))
    nt = mm((((1,), (1,)), ((), ())))
    tn = mm((((0,), (0,)), ((), ())))

    @pl.when(pl.program_id(2) == 0)
    def _():
        state[...] = s0_ref[0]

    pi = lax.broadcasted_iota(jnp.int32, (C, LANES), 0)
    pj = lax.broadcasted_iota(jnp.int32, (C, LANES), 1) % C
    strict = (pi < pj) if reverse else (pi > pj)
    incl = (pi <= pj) if reverse else (pi >= pj)
    blk = (pi // INV_BLOCK) == (pj // INV_BLOCK)
    eye = pi == pj
    ident = jnp.where(eye, 1.0, 0.0)
    head0 = lax.broadcasted_iota(jnp.int32, (1, LANES), 1) < RWKV_HEAD

    def stack(z):
        zb = z.astype(BF16)
        zero = jnp.zeros_like(zb)
        return jnp.concatenate([jnp.where(head0, zb, zero), jnp.where(head0, zero, zb)], axis=0)

    side = lambda x, y: jnp.concatenate([x, y], axis=1)
    diag_blocks = lambda z: jnp.where(head0, z[:C], z[C:])

    each = lambda f, *ls: [f(*xs) for xs in zip(*ls)]
    order = range(n_chunks - 1, -1, -1) if reverse else range(n_chunks)

    def cumsum(x):
        step = 1
        while step < C:
            if reverse:
                x = x + jnp.where(pi < C - step, pltpu.roll(x, C - step, 0), 0.0)
            else:
                x = x + jnp.where(pi >= step, pltpu.roll(x, step, 0), 0.0)
            step *= 2
        return x

    def chain(pairs):
        units = [(slice(c * C, (c + 1) * C), slice(p * LANES, (p + 1) * LANES)) for p in pairs for c in range(n_chunks)]
        load = lambda ref: [ref[0, rows, cols].astype(F32) for rows, cols in units]
        r, lw, k, v, kk, a = (load(ref) for ref in (r_ref, lw_ref, k_ref, v_ref, kk_ref, a_ref))
        cum = each(cumsum, lw)
        total = each(lambda z: z[0:1] if reverse else z[C - 1:C], cum)
        g_in = each(lambda z: jnp.exp(-z), cum)
        g_to_end = each(lambda t, z: jnp.exp(t - z), total, cum)
        b = each(lambda x, y: x * y, kk, a)
        a_u = each(lambda x, z, l: x * jnp.exp(z - l), kk, cum, lw)
        r_u = each(lambda x, z: x * jnp.exp(z), r, cum)
        b_s = each(lambda x, y: stack(x * y), b, g_in)
        k_s = each(lambda x, y: stack(x * y), k, g_in)
        v_s = each(stack, v)
        bh = each(lambda x, y: x * y, b, g_to_end)
        kh = each(lambda x, y: x * y, k, g_to_end)
        big = each(lambda a_, r_, b_, k_: nt(jnp.concatenate([a_, r_], axis=0), jnp.concatenate([b_, k_], axis=0)),
                   a_u, r_u, b_s, k_s)
        l_ab = each(lambda z: jnp.where(strict, z[:C, :LANES], 0.0), big)
        l_ak = each(lambda z: jnp.where(strict, z[:C, LANES:], 0.0), big)
        p_rb = each(lambda z: jnp.where(incl, z[C:, :LANES], 0.0), big)
        p_rk = each(lambda z: jnp.where(incl, z[C:, LANES:], 0.0), big)
        l_d = each(lambda z: jnp.where(blk, z, 0.0), l_ab)
        l_o = each(lambda x, y: x - y, l_ab, l_d)
        l2 = each(lambda d: hp(d, stack(d)), l_d)
        lpv = each(lambda x, y, vs: hp(jnp.concatenate([x, y], axis=0), vs), l_ak, p_rk, v_s)
        lv = each(lambda z: z[:C], lpv)
        prv = each(lambda z: z[C:], lpv)
        khv = each(tn, kh, v)
        l34 = each(lambda d, s2: hp(jnp.concatenate([d, s2], axis=0), stack(s2)), l_d, l2)
        p1 = each(lambda d, s2, s34: ident - d + s2 - s34[:C], l_d, l2, l34)
        x48 = each(lambda p, s34: hp(jnp.concatenate([p, s34[C:]], axis=0), stack(s34[C:])), p1, l34)
        p2 = each(lambda p, x: p + x[:C], p1, x48)
        t_d = each(lambda p, x: p + hp(p, stack(x[C:])), p2, x48)
        m1 = each(lambda t, o: hp(t, stack(o)), t_d, l_o)
        mm_ = each(lambda x, t: hp(x, side(stack(x), stack(t))), m1, t_d)
        y_d = each(lambda t, x: t - x[:, LANES:], t_d, mm_)
        t_inv = each(lambda y, x: y + hp(x[:, :LANES], stack(y)), y_d, mm_)
        w = each(lambda t, x, y: hp(t, side(stack(x), stack(y))), t_inv, a_u, lv)
        pw = each(lambda p, x: hp(p, side(stack(x[:, :LANES]), stack(x[:, LANES:]))), p_rb, w)
        bw = each(tn, bh, w)
        q1 = each(lambda x, y: x - y[:, :LANES], r_u, pw)
        y2 = each(lambda x, y: x - y[:, LANES:], prv, pw)
        g_mat = each(lambda t, y: jnp.where(eye, jnp.exp(t), 0.0) - diag_blocks(y[:, :LANES]), total, bw)
        h_mat = each(lambda x, y: diag_blocks(x) - diag_blocks(y[:, LANES:]), khv, bw)
        s = [state[p] for p in pairs]
        for c in order:
            for i, p in enumerate(pairs):
                u = i * n_chunks + c
                ys = hp(jnp.concatenate([q1[u], g_mat[u]], axis=0), stack(s[i]))
                y_ref[0, c * C:(c + 1) * C, p * LANES:(p + 1) * LANES] = ys[:C] + y2[u]
                s[i] = ys[C:] + h_mat[u]
        for i, p in enumerate(pairs):
            state[p] = s[i]
        return s

    final_state = chain(list(range(n_pairs)))

    @pl.when(pl.program_id(2) == pl.num_programs(2) - 1)
    def _():
        for p in range(n_pairs):
            s_ref[0, p] = final_state[p]


WKV_PAIRS = 4
WKV_CHUNKS = 4


def _wkv(r, lw, k, v, kk, a, s0, reverse):
    B, T, _ = r.shape
    rows = min(T, WKV_CHUNKS * CHUNK)
    n_steps = T // rows
    width = WKV_PAIRS * LANES
    tmap = (lambda b, p, j: (b, n_steps - 1 - j, p)) if reverse else (lambda b, p, j: (b, j, p))
    seq_spec = pl.BlockSpec((1, rows, width), tmap)
    st_spec = pl.BlockSpec((1, WKV_PAIRS, RWKV_HEAD, LANES), lambda b, p, j: (b, p, 0, 0))
    return pl.pallas_call(
        functools.partial(_wkv_kernel, n_chunks=rows // CHUNK, n_pairs=WKV_PAIRS, reverse=reverse),
        grid=(B, PAIRS // WKV_PAIRS, n_steps),
        in_specs=[seq_spec] * 6 + [st_spec],
        out_specs=[seq_spec, st_spec],
        out_shape=[jax.ShapeDtypeStruct(r.shape, F32), jax.ShapeDtypeStruct((B, PAIRS, RWKV_HEAD, LANES), F32)],
        scratch_shapes=[pltpu.VMEM((WKV_PAIRS, RWKV_HEAD, LANES), F32)],
        compiler_params=_params("parallel", "parallel", "arbitrary"),
    )(r, lw, k, v, kk, a, s0)


def _rwkv_finish_kernel(x_ref, y0_ref, y1_ref, r_ref, v_ref, k0_ref, k1_ref, g_ref, mod_ref, rk_ref, lng_ref, lnb_ref,
                        wo_ref, o_ref):
    y = y0_ref[0] + y1_ref[0]
    inv_n = 1.0 / RWKV_HEAD
    mean = _seg_sum_wide(y) * inv_n
    yc = y - mean
    var = _seg_sum_wide(yc * yc) * inv_n
    yn = yc * lax.rsqrt(var + GN_EPS) * lng_ref[...] + lnb_ref[...]
    f32 = lambda ref: ref[0].astype(F32)
    bonus = _seg_sum_wide(f32(r_ref) * (f32(k0_ref) + f32(k1_ref)) * rk_ref[...]) * f32(v_ref)
    out = jnp.dot(((yn + bonus) * f32(g_ref)).astype(BF16), wo_ref[...], preferred_element_type=F32)
    o_ref[0] = x_ref[0] + mod_ref[0][2:3] * out


def _rwkv_finish(x, y0, y1, r, v, k0, k1, gate, mod, p):
    B, T, _ = x.shape
    tm = min(T, 256)
    vec = _const_spec((1, D_MODEL))
    return pl.pallas_call(
        _rwkv_finish_kernel,
        grid=(B, T // tm),
        in_specs=[_row_spec(tm)] * 8 + [_mod_spec(), vec, vec, vec, _const_spec((D_MODEL, D_MODEL))],
        out_specs=_row_spec(tm),
        out_shape=jax.ShapeDtypeStruct(x.shape, F32),
        compiler_params=_params("parallel", "parallel"),
    )(x, y0, y1, r, v, k0, k1, gate, mod, p["r_k"], p["ln_g"], p["ln_b"], p["w_o"])


HALO = SUBLANES


def _pool_kernel(x_ref, xp_ref, xn_ref, g_ref, mod_ref, w_ref, sc_ref, o_ref, *, tm, seq_len):
    i = pl.program_id(1)
    last = pl.num_programs(1) - 1
    g = g_ref[...]
    mod = mod_ref[0]
    x = x_ref[0]
    h = _norm_mod(x, g, mod, 0)
    hp = _norm_mod(xp_ref[0], g, mod, 0) * jnp.where(i == 0, 0.0, 1.0)
    hn = _norm_mod(xn_ref[0], g, mod, 0) * jnp.where(i == last, 0.0, 1.0)
    ext = jnp.concatenate([hp, h, hn], axis=0)
    n_ext = tm + 2 * HALO
    t = i * tm + lax.broadcasted_iota(jnp.int32, (tm, 1), 0)
    outs = []
    for gi, win in enumerate(POOL_WINDOWS):
        e = ext[:, gi * POOL_GROUP:(gi + 1) * POOL_GROUP]
        acc = e + pltpu.roll(e, 1, 0)
        step = 1
        while 2 * step < win:
            acc = pltpu.roll(acc, step, 0) + pltpu.roll(acc, n_ext - step, 0)
            step *= 2
        half = win // 2
        cnt = (jnp.minimum(t + half, seq_len) - jnp.maximum(t - half, 0)).astype(F32)
        pooled = acc[HALO:HALO + tm] / cnt - e[HALO:HALO + tm]
        outs.append(jnp.dot(pooled.astype(BF16), w_ref[gi], preferred_element_type=F32))
    y = jnp.concatenate(outs, axis=1) * sc_ref[...]
    o_ref[0] = x + mod[2:3] * y


def _pool_mixer(x, g, mod, w_bf, scale):
    B, T, _ = x.shape
    tm = min(T, 256)
    nb = tm // HALO
    n_blk = T // HALO
    prev_spec = pl.BlockSpec((1, HALO, D_MODEL), lambda b, i: (b, jnp.maximum(i * nb - 1, 0), 0))
    next_spec = pl.BlockSpec((1, HALO, D_MODEL), lambda b, i: (b, jnp.minimum((i + 1) * nb, n_blk - 1), 0))
    n_grp = len(POOL_WINDOWS)
    return pl.pallas_call(
        functools.partial(_pool_kernel, tm=tm, seq_len=T),
        grid=(B, T // tm),
        in_specs=[_row_spec(tm), prev_spec, next_spec, _const_spec((1, D_MODEL)), _mod_spec(),
                  _const_spec((n_grp, POOL_GROUP, POOL_GROUP)), _const_spec((1, D_MODEL))],
        out_specs=_row_spec(tm),
        out_shape=jax.ShapeDtypeStruct(x.shape, F32),
        compiler_params=_params("parallel", "parallel"),
    )(x, x, x, g, mod, w_bf, scale)


def _rope_tables(n_tokens):
    rows = n_tokens // GRID_W
    n_freq = HEAD_DIM // 4
    inv = ROPE_THETA ** (-jnp.arange(n_freq, dtype=F32) / n_freq)
    ang_r = jnp.arange(rows, dtype=F32)[:, None] * inv
    ang_c = jnp.arange(GRID_W, dtype=F32)[:, None] * inv
    ang = jnp.concatenate([
        jnp.broadcast_to(ang_r[:, None, :], (rows, GRID_W, n_freq)),
        jnp.broadcast_to(ang_c[None, :, :], (rows, GRID_W, n_freq))], axis=-1).reshape(rows * GRID_W, 2 * n_freq)
    cos, sin = jnp.cos(ang), jnp.sin(ang)
    return jnp.tile(cos, (1, 4)), jnp.tile(jnp.concatenate([-sin, sin], axis=-1), (1, 2))


def kernel(x, c, ctx, c_ctx, w_mod, b_mod, norm1_g, norm2_g, mlp_w_in, mlp_w_out, attn_w_qkv, attn_q_gain, attn_k_gain, attn_w_o, rwkv_mu, rwkv_w_rkv, rwkv_w0, rwkv_w1, rwkv_w2, rwkv_a0, rwkv_a1, rwkv_a2, rwkv_g1, rwkv_g2, rwkv_k_k, rwkv_k_a, rwkv_r_k, rwkv_ln_g, rwkv_ln_b, rwkv_w_o, pool_w, pool_scale):
    B, S, _ = x.shape
    L = ctx.shape[1]
    depth = w_mod.shape[0]
    assert x.shape[2] == D_MODEL and S % (4 * CHUNK) == 0 and L % CHUNK == 0 and S % GRID_W == 0

    n_rows = -(-(B + 1) // SUBLANES) * SUBLANES
    cvec = jnp.concatenate([c, c_ctx[None], jnp.zeros((n_rows - B - 1, D_MODEL), F32)], axis=0)
    mod_all = _modulation(cvec, w_mod, b_mod).reshape(depth, n_rows, N_MOD, D_MODEL)
    cos_t, sin_t = _rope_tables(S)
    zero_tab = jnp.zeros((L, LANES), F32)
    row = lambda a: a.reshape(1, -1)

    for i in range(depth):
        last = i == depth - 1
        j = i // N_MIXERS
        mod_l = mod_all[i, :B]
        mod_c = jnp.broadcast_to(mod_all[i, B][None], (B, N_MOD, D_MODEL))
        g1 = row(norm1_g[i])
        kind = i % N_MIXERS
        if kind == 0:
            w_qkv = attn_w_qkv[j].astype(BF16)
            w_o = attn_w_o[j].astype(BF16)
            qg = jnp.tile(row(attn_q_gain[j]), (1, 2))
            kg = jnp.tile(row(attn_k_gain[j]), (1, 2))
            q_l, k_l, v_l = _qkv_project(x, g1, mod_l, w_qkv, qg, kg, cos_t, sin_t, True)
            q_c, k_c, v_c = _qkv_project(ctx, g1, mod_c, w_qkv, qg, kg, zero_tab, zero_tab, False)
            o_l = _attention(q_l, [(k_l, v_l), (k_c, v_c)])
            x = _proj_residual(x, o_l, mod_l, w_o)
            if not last:
                o_c = _attention(q_c, [(k_c, v_c)])
                ctx = _proj_residual(ctx, o_c, mod_c, w_o)
        elif kind == 1:
            p = {
                "mu": rwkv_mu[j], "w_rkv": rwkv_w_rkv[j].astype(BF16),
                "w1": jnp.concatenate([rwkv_w1[j, 0], rwkv_w1[j, 1]], axis=1).astype(BF16),
                "w2": rwkv_w2[j].astype(BF16), "w0": rwkv_w0[j],
                "a1": jnp.concatenate([rwkv_a1[j, 0], rwkv_a1[j, 1]], axis=1).astype(BF16),
                "a2": rwkv_a2[j].astype(BF16), "a0": rwkv_a0[j],
                "g1": rwkv_g1[j].astype(BF16), "g2": rwkv_g2[j].astype(BF16),
                "k_k": row(rwkv_k_k[j]), "k_a": row(rwkv_k_a[j]), "r_k": row(rwkv_r_k[j]),
                "ln_g": row(rwkv_ln_g[j]), "ln_b": row(rwkv_ln_b[j]), "w_o": rwkv_w_o[j].astype(BF16),
            }
            r_l, v_l, kk_l, g_l, lw0_l, lw1_l, key0_l, key1_l, a0_l, a1_l = _rwkv_prepare(x, g1, mod_l, p)
            r_c, v_c, kk_c, g_c, lw0_c, lw1_c, key0_c, key1_c, a0_c, a1_c = _rwkv_prepare(ctx, g1, mod_c, p)
            zero_state = jnp.zeros((B, PAIRS, RWKV_HEAD, LANES), F32)
            ys_l, ys_c = [], []
            for rev, (lw_l, key_l, a_l, lw_c, key_c, a_c) in enumerate(
                    ((lw0_l, key0_l, a0_l, lw0_c, key0_c, a0_c), (lw1_l, key1_l, a1_l, lw1_c, key1_c, a1_c))):
                y_c, s_ctx = _wkv(r_c, lw_c, key_c, v_c, kk_c, a_c, zero_state, bool(rev))
                y_l, _ = _wkv(r_l, lw_l, key_l, v_l, kk_l, a_l, s_ctx, bool(rev))
                ys_l.append(y_l)
                ys_c.append(y_c)
            x = _rwkv_finish(x, ys_l[0], ys_l[1], r_l, v_l, key0_l, key1_l, g_l, mod_l, p)
            if not last:
                ctx = _rwkv_finish(ctx, ys_c[0], ys_c[1], r_c, v_c, key0_c, key1_c, g_c, mod_c, p)
        else:
            w_p = pool_w[j].astype(BF16)
            sc = row(pool_scale[j])
            x = _pool_mixer(x, g1, mod_l, w_p, sc)
            if not last:
                ctx = _pool_mixer(ctx, g1, mod_c, w_p, sc)
        g2 = row(norm2_g[i])
        w_in = mlp_w_in[i].astype(BF16)
        w_out = mlp_w_out[i].astype(BF16)
        x = _mlp(x, g2, mod_l, w_in, w_out)
        if not last:
            ctx = _mlp(ctx, g2, mod_c, w_in, w_out)
    return x
```

```python
import functools

import jax
import jax.numpy as jnp
import numpy as np
from jax import lax
from jax.experimental import pallas as pl
from jax.experimental.pallas import tpu as pltpu

F32 = jnp.float32
BF16 = jnp.bfloat16

D_MODEL = 1024
GRID_W = 64
N_MIXERS = 3
N_MOD = 6
EPS = 1e-6
N_HEADS = 16
N_KV_HEADS = 4
HEAD_DIM = 64
GQA_REP = N_HEADS // N_KV_HEADS
Q_WIDTH = N_HEADS * HEAD_DIM
KV_WIDTH = N_KV_HEADS * HEAD_DIM
QKV_WIDTH = Q_WIDTH + 2 * KV_WIDTH
ROPE_THETA = 10000.0
RWKV_HEAD = 64
DECAY_LORA = 64
ICLR_LORA = 64
GATE_LORA = 160
GN_EPS = RWKV_HEAD * 1e-5
POOL_WINDOWS = (2, 4, 8, 16)
POOL_GROUP = D_MODEL // len(POOL_WINDOWS)
D_FF = 4 * D_MODEL

LANES = 128
SUBLANES = 8
PAIRS = D_MODEL // LANES
CHUNK = 64
INV_BLOCK = 16
VMEM_LIMIT = 56 * 1024 * 1024

HIGHEST = lax.Precision.HIGHEST


def _params(*sem):
    return pltpu.CompilerParams(dimension_semantics=sem, vmem_limit_bytes=VMEM_LIMIT)


def _const_spec(shape):
    zeros = (0,) * len(shape)
    return pl.BlockSpec(shape, lambda *_: zeros, pipeline_mode=pl.Buffered(1))


def _row_spec(tm, width=D_MODEL):
    return pl.BlockSpec((1, tm, width), lambda b, i: (b, i, 0))


def _mod_spec():
    return pl.BlockSpec((1, N_MOD, D_MODEL), lambda b, i: (b, 0, 0))


def _norm_mod(x, g, mod, k):
    y = x * lax.rsqrt(jnp.mean(x * x, axis=-1, keepdims=True) + EPS) * g
    return y * (1.0 + mod[k + 1:k + 2]) + mod[k:k + 1]


def _seg_sum(x):
    i = lax.broadcasted_iota(jnp.int32, (LANES, LANES), 0) // HEAD_DIM
    j = lax.broadcasted_iota(jnp.int32, (LANES, LANES), 1) // HEAD_DIM
    return jnp.dot(x.astype(BF16), jnp.where(i == j, 1.0, 0.0).astype(BF16), preferred_element_type=F32)


def _sigmoid(z):
    return 0.5 * jnp.tanh(0.5 * z) + 0.5


def _seg_sum_wide(x):
    return jnp.concatenate([_seg_sum(x[:, p * LANES:(p + 1) * LANES]) for p in range(x.shape[1] // LANES)], axis=1)


def _mod_kernel(c_ref, w_ref, b_ref, o_ref):
    c = c_ref[...]
    s = c * jax.nn.sigmoid(c)
    o_ref[0] = jnp.dot(s, w_ref[0], precision=HIGHEST, preferred_element_type=F32) + b_ref[0]


def _modulation(cvec, w_mod, b_mod):
    depth = w_mod.shape[0]
    rows = cvec.shape[0]
    tn = 1536
    return pl.pallas_call(
        _mod_kernel,
        grid=(depth, N_MOD * D_MODEL // tn),
        in_specs=[pl.BlockSpec((rows, D_MODEL), lambda l, j: (0, 0)),
                  pl.BlockSpec((1, D_MODEL, tn), lambda l, j: (l, 0, j)),
                  pl.BlockSpec((1, 1, tn), lambda l, j: (l, 0, j))],
        out_specs=pl.BlockSpec((1, rows, tn), lambda l, j: (l, 0, j)),
        out_shape=jax.ShapeDtypeStruct((depth, rows, N_MOD * D_MODEL), F32),
        compiler_params=_params("parallel", "parallel"),
    )(cvec, w_mod, b_mod.reshape(depth, 1, N_MOD * D_MODEL))


VT_ROWS = HEAD_DIM + 16
Q_SCALE = float(HEAD_DIM ** -0.5 * np.log2(np.e))


def _qkv_kernel(x_ref, g_ref, mod_ref, w_ref, qg_ref, kg_ref, cos_ref, sin_ref, q_ref, k_ref, v_ref, *, use_rope):
    h = _norm_mod(x_ref[0], g_ref[...], mod_ref[0], 0).astype(BF16)
    acc = jnp.dot(h, w_ref[...], preferred_element_type=F32)
    lane = lax.broadcasted_iota(jnp.int32, (1, LANES), 1)
    first_half = (lane % HEAD_DIM) < (HEAD_DIM // 2)

    def head_pair(xp, gain, scale):
        y = xp * lax.rsqrt(_seg_sum(xp * xp) * (1.0 / HEAD_DIM) + EPS) * gain
        if use_rope:
            rot = jnp.where(first_half, pltpu.roll(y, LANES - HEAD_DIM // 2, 1), pltpu.roll(y, HEAD_DIM // 2, 1))
            y = y * cos_ref[...] + rot * sin_ref[...]
        return (y * scale).astype(BF16)

    for p in range(Q_WIDTH // LANES):
        y = head_pair(acc[:, p * LANES:(p + 1) * LANES], qg_ref[...], Q_SCALE)
        q_ref[0, 2 * p] = y[:, :HEAD_DIM]
        q_ref[0, 2 * p + 1] = y[:, HEAD_DIM:]
    for p in range(KV_WIDTH // LANES):
        y = head_pair(acc[:, Q_WIDTH + p * LANES:Q_WIDTH + (p + 1) * LANES], kg_ref[...], 1.0)
        k_ref[0, 2 * p] = y[:, :HEAD_DIM]
        k_ref[0, 2 * p + 1] = y[:, HEAD_DIM:]
    ones = jnp.ones((VT_ROWS - HEAD_DIM, acc.shape[0]), F32)
    for p in range(KV_WIDTH // LANES):
        c0 = Q_WIDTH + KV_WIDTH + p * LANES
        vt = acc[:, c0:c0 + LANES].T
        v_ref[0, 2 * p] = jnp.concatenate([vt[:HEAD_DIM], ones], axis=0).astype(BF16)
        v_ref[0, 2 * p + 1] = jnp.concatenate([vt[HEAD_DIM:], ones], axis=0).astype(BF16)


def _qkv_project(x, g, mod, w_bf, qg, kg, cos_t, sin_t, use_rope):
    B, T, _ = x.shape
    tm = min(T, 512)
    tab_spec = pl.BlockSpec((tm, LANES), lambda b, i: (i, 0))
    head_spec = lambda n: pl.BlockSpec((1, n, tm, HEAD_DIM), lambda b, i: (b, 0, i, 0))
    return pl.pallas_call(
        functools.partial(_qkv_kernel, use_rope=use_rope),
        grid=(B, T // tm),
        in_specs=[_row_spec(tm), _const_spec((1, D_MODEL)), _mod_spec(), _const_spec((D_MODEL, QKV_WIDTH)),
                  _const_spec((1, LANES)), _const_spec((1, LANES)), tab_spec, tab_spec],
        out_specs=[head_spec(N_HEADS), head_spec(N_KV_HEADS),
                   pl.BlockSpec((1, N_KV_HEADS, VT_ROWS, tm), lambda b, i: (b, 0, 0, i))],
        out_shape=[jax.ShapeDtypeStruct((B, N_HEADS, T, HEAD_DIM), BF16),
                   jax.ShapeDtypeStruct((B, N_KV_HEADS, T, HEAD_DIM), BF16),
                   jax.ShapeDtypeStruct((B, N_KV_HEADS, VT_ROWS, T), BF16)],
        compiler_params=_params("parallel", "parallel"),
    )(x, g, mod, w_bf, qg, kg, cos_t, sin_t)


def _attn_kernel(*refs, n_src, tq):
    q_ref = refs[0]
    kv_refs = refs[1:1 + 2 * n_src]
    o_ref = refs[1 + 2 * n_src]
    s_scr, m_scr = refs[2 + 2 * n_src:]
    nq = GQA_REP * tq
    i = pl.program_id(2)
    pieces = []
    row = 0
    for s in range(n_src):
        keys_total = kv_refs[2 * s].shape[2]
        ck = min(ATTN_KV_CHUNK, keys_total)
        for r0 in range(0, keys_total, ck):
            pieces.append((kv_refs[2 * s], kv_refs[2 * s + 1], r0, ck, row))
            row += ck

    def run(with_pass2):
        q = q_ref[0].reshape(nq, HEAD_DIM)
        m8 = None
        if with_pass2:
            m_prev = m_scr[...]
            acc = jnp.zeros((VT_ROWS, nq), F32)
        for k_ref, vt_ref, r0, ck, c0 in pieces:
            if with_pass2:
                p = jnp.exp2(s_scr[c0:c0 + ck, :] - m_prev).astype(BF16)
                acc = acc + jnp.dot(vt_ref[0, 0, :, r0:r0 + ck], p, preferred_element_type=F32)
            s_blk = lax.dot_general(k_ref[0, 0, r0:r0 + ck, :], q, (((1,), (1,)), ((), ())),
                                    preferred_element_type=F32)
            s_scr[c0:c0 + ck, :] = s_blk
            bm = jnp.max(s_blk.reshape(ck // SUBLANES, SUBLANES, nq), axis=0)
            m8 = bm if m8 is None else jnp.maximum(m8, bm)
        m_scr[...] = jnp.max(m8, axis=0, keepdims=True)
        if with_pass2:
            o = (acc[:HEAD_DIM] / acc[HEAD_DIM:HEAD_DIM + 1]).T
            o_ref[0] = jnp.concatenate([o[h * tq:(h + 1) * tq] for h in range(GQA_REP)], axis=1).astype(BF16)

    pl.when(i == 0)(lambda: run(False))
    pl.when(i > 0)(lambda: run(True))


ATTN_KV_CHUNK = 256


def _attention(q, kv_sources):
    B, _, T, _ = q.shape
    tq = min(T, 256)
    n_tiles = T // tq
    total = sum(k.shape[2] for k, _ in kv_sources)
    in_specs = [pl.BlockSpec((1, GQA_REP, tq, HEAD_DIM), lambda b, g, i: (b, g, jnp.minimum(i, n_tiles - 1), 0))]
    args = [q]
    for k, vt in kv_sources:
        in_specs += [pl.BlockSpec((1, 1, k.shape[2], HEAD_DIM), lambda b, g, i: (b, g, 0, 0)),
                     pl.BlockSpec((1, 1, VT_ROWS, k.shape[2]), lambda b, g, i: (b, g, 0, 0))]
        args += [k, vt]
    return pl.pallas_call(
        functools.partial(_attn_kernel, n_src=len(kv_sources), tq=tq),
        grid=(B, N_KV_HEADS, n_tiles + 1),
        in_specs=in_specs,
        out_specs=pl.BlockSpec((1, tq, GQA_REP * HEAD_DIM), lambda b, g, i: (b, jnp.maximum(i - 1, 0), g)),
        out_shape=jax.ShapeDtypeStruct((B, T, Q_WIDTH), BF16),
        scratch_shapes=[pltpu.VMEM((total, GQA_REP * tq), F32), pltpu.VMEM((1, GQA_REP * tq), F32)],
        compiler_params=_params("parallel", "parallel", "arbitrary"),
    )(*args)


def _proj_branch(y_ref, w_ref):
    return jnp.dot(y_ref[0], w_ref[...], preferred_element_type=F32)


def _mixer_mlp_kernel(*refs, branch, ff_chunk):
    x_ref, mod_ref, g_ref, win_ref, wout_ref = refs[:5]
    o_ref = refs[-1]
    x = x_ref[0]
    mod = mod_ref[0]
    if branch is not None:
        x = x + mod[2:3] * branch(*refs[5:-1])
    h = _norm_mod(x, g_ref[...], mod, 3).astype(BF16)
    acc = jnp.zeros(x.shape, F32)
    for f0 in range(0, D_FF, ff_chunk):
        u = jnp.maximum(jnp.dot(h, win_ref[:, f0:f0 + ff_chunk], preferred_element_type=F32), 0.0)
        acc = acc + jnp.dot((u * u).astype(BF16), wout_ref[f0:f0 + ff_chunk, :], preferred_element_type=F32)
    o_ref[0] = x + mod[5:6] * acc


def _mixer_mlp(x, mod, g, win_bf, wout_bf, branch=None, branch_args=(), branch_specs=(), tm_max=512):
    B, T, _ = x.shape
    tm = min(T, tm_max)
    specs = [s(tm) if callable(s) else s for s in branch_specs]
    return pl.pallas_call(
        functools.partial(_mixer_mlp_kernel, branch=branch, ff_chunk=1024),
        grid=(B, T // tm),
        in_specs=[_row_spec(tm), _mod_spec(), _const_spec((1, D_MODEL)),
                  _const_spec((D_MODEL, D_FF)), _const_spec((D_FF, D_MODEL))] + specs,
        out_specs=_row_spec(tm),
        out_shape=jax.ShapeDtypeStruct(x.shape, F32),
        compiler_params=_params("parallel", "parallel"),
    )(x, mod, g, win_bf, wout_bf, *branch_args)


DECAY_SCALE = float(np.exp(-0.5))


def _rwkv_prep_kernel(x_ref, xp_ref, xn_ref, g_ref, mod_ref, mu_ref, wrkv_ref, w1_ref, w2_ref, w0_ref, a1_ref, a2_ref,
                      a0_ref, g1_ref, g2_ref, kk_ref, ka_ref,
                      r_o, v_o, kk_o, g_o, lw0_o, lw1_o, key0_o, key1_o, a0_o, a1_o, *, tm):
    i = pl.program_id(1)
    last = pl.num_programs(1) - 1
    g = g_ref[...]
    mod = mod_ref[0]
    h = _norm_mod(x_ref[0], g, mod, 0)
    hp = _norm_mod(xp_ref[0][SUBLANES - 1:SUBLANES], g, mod, 0) * jnp.where(i == 0, 0.0, 1.0)
    hn = _norm_mod(xn_ref[0][0:1], g, mod, 0) * jnp.where(i == last, 0.0, 1.0)
    row = lax.broadcasted_iota(jnp.int32, (tm, 1), 0)
    prev = jnp.where(row == 0, hp, pltpu.roll(h, 1, 0))
    nxt = jnp.where(row == tm - 1, hn, pltpu.roll(h, tm - 1, 0))
    xx = 0.5 * (prev + nxt) - h
    mix = lambda m: (h + xx * mu_ref[m:m + 1]).astype(BF16)
    dot = functools.partial(jnp.dot, preferred_element_type=F32)
    r = dot(mix(0), wrkv_ref[0])
    k = dot(mix(2), wrkv_ref[1])
    v = dot(mix(3), wrkv_ref[2])
    r_o[0] = r.astype(BF16)
    v_o[0] = v.astype(BF16)
    g_o[0] = dot(_sigmoid(dot(mix(5), g1_ref[...])).astype(BF16), g2_ref[...]).astype(BF16)
    kk = k * kk_ref[...]
    kk_o[0] = (kk * lax.rsqrt(jnp.maximum(_seg_sum_wide(kk * kk), 1e-24))).astype(BF16)
    tw = jnp.tanh(dot(mix(1), w1_ref[...])).astype(BF16)
    ta = dot(mix(4), a1_ref[...]).astype(BF16)
    for d, (lw_o, key_o, a_o) in enumerate(((lw0_o, key0_o, a0_o), (lw1_o, key1_o, a1_o))):
        z = w0_ref[d:d + 1] + dot(tw[:, d * DECAY_LORA:(d + 1) * DECAY_LORA], w2_ref[d])
        lw_o[0] = -DECAY_SCALE * _sigmoid(z)
        a = _sigmoid(a0_ref[d:d + 1] + dot(ta[:, d * ICLR_LORA:(d + 1) * ICLR_LORA], a2_ref[d]))
        a_o[0] = a.astype(BF16)
        key_o[0] = (k * (1.0 + (a - 1.0) * ka_ref[...])).astype(BF16)


def _rwkv_prepare(x, g, mod, p):
    B, T, _ = x.shape
    tm = min(T, 256)
    nb = tm // SUBLANES
    n_blk8 = T // SUBLANES
    prev_spec = pl.BlockSpec((1, SUBLANES, D_MODEL), lambda b, i: (b, jnp.maximum(i * nb - 1, 0), 0))
    next_spec = pl.BlockSpec((1, SUBLANES, D_MODEL), lambda b, i: (b, jnp.minimum((i + 1) * nb, n_blk8 - 1), 0))
    o_bf, o_f32 = jax.ShapeDtypeStruct(x.shape, BF16), jax.ShapeDtypeStruct(x.shape, F32)
    return pl.pallas_call(
        functools.partial(_rwkv_prep_kernel, tm=tm),
        grid=(B, T // tm),
        in_specs=[_row_spec(tm), prev_spec, next_spec, _const_spec((1, D_MODEL)), _mod_spec(),
                  _const_spec((6, D_MODEL)), _const_spec((3, D_MODEL, D_MODEL)),
                  _const_spec((D_MODEL, 2 * DECAY_LORA)), _const_spec((2, DECAY_LORA, D_MODEL)),
                  _const_spec((2, D_MODEL)),
                  _const_spec((D_MODEL, 2 * ICLR_LORA)), _const_spec((2, ICLR_LORA, D_MODEL)),
                  _const_spec((2, D_MODEL)),
                  _const_spec((D_MODEL, GATE_LORA)), _const_spec((GATE_LORA, D_MODEL)),
                  _const_spec((1, D_MODEL)), _const_spec((1, D_MODEL))],
        out_specs=[_row_spec(tm)] * 10,
        out_shape=[o_bf, o_bf, o_bf, o_bf, o_f32, o_f32, o_bf, o_bf, o_bf, o_bf],
        compiler_params=_params("parallel", "parallel"),
    )(x, x, x, g, mod, p["mu"], p["w_rkv"], p["w1"], p["w2"], p["w0"], p["a1"], p["a2"], p["a0"], p["g1"], p["g2"],
      p["k_k"], p["k_a"])


def _wkv_kernel(r_ref, lw_ref, k_ref, v_ref, kk_ref, a_ref, s0_ref, y_ref, s_ref, state, *, n_chunks, n_pairs,
                reverse):
    C = CHUNK

    def mm(dims):
        return lambda a, b: lax.dot_general(a.astype(BF16), b.astype(BF16), dims, preferred_element_type=F32)

    hp = mm((((1,), (0,)), ((), ())))
    nt = mm((((1,), (1,)), ((), ())))
    tn = mm((((0,), (0,)), ((), ())))

    @pl.when(pl.program_id(2) == 0)
    def _():
        state[...] = s0_ref[0]

    pi = lax.broadcasted_iota(jnp.int32, (C, LANES), 0)
    pj = lax.broadcasted_iota(jnp.int32, (C, LANES), 1) % C
    strict = (pi < pj) if reverse else (pi > pj)
    incl = (pi <= pj) if reverse else (pi >= pj)
    blk = (pi // INV_BLOCK) == (pj // INV_BLOCK)
    eye = pi == pj
    ident = jnp.where(eye, 1.0, 0.0)
    head0 = lax.broadcasted_iota(jnp.int32, (1, LANES), 1) < RWKV_HEAD

    def stack(z):
        zb = z.astype(BF16)
        zero = jnp.zeros_like(zb)
        return jnp.concatenate([jnp.where(head0, zb, zero), jnp.where(head0, zero, zb)], axis=0)

    side = lambda x, y: jnp.concatenate([x, y], axis=1)
    diag_blocks = lambda z: jnp.where(head0, z[:C], z[C:])

    each = lambda f, *ls: [f(*xs) for xs in zip(*ls)]
    order = range(n_chunks - 1, -1, -1) if reverse else range(n_chunks)

    def cumsum(x):
        step = 1
        while step < C:
            if reverse:
                x = x + jnp.where(pi < C - step, pltpu.roll(x, C - step, 0), 0.0)
            else:
                x = x + jnp.where(pi >= step, pltpu.roll(x, step, 0), 0.0)
            step *= 2
        return x

    def chain(pairs):
        units = [(slice(c * C, (c + 1) * C), slice(p * LANES, (p + 1) * LANES)) for p in pairs for c in range(n_chunks)]
        load = lambda ref: [ref[0, rows, cols].astype(F32) for rows, cols in units]
        r, lw, k, v, kk, a = (load(ref) for ref in (r_ref, lw_ref, k_ref, v_ref, kk_ref, a_ref))
        cum = each(cumsum, lw)
        total = each(lambda z: z[0:1] if reverse else z[C - 1:C], cum)
        g_in = each(lambda z: jnp.exp(-z), cum)
        g_to_end = each(lambda t, z: jnp.exp(t - z), total, cum)
        b = each(lambda x, y: x * y, kk, a)
        a_u = each(lambda x, z, l: x * jnp.exp(z - l), kk, cum, lw)
        r_u = each(lambda x, z: x * jnp.exp(z), r, cum)
        b_s = each(lambda x, y: stack(x * y), b, g_in)
        k_s = each(lambda x, y: stack(x * y), k, g_in)
        v_s = each(stack, v)
        bh = each(lambda x, y: x * y, b, g_to_end)
        kh = each(lambda x, y: x * y, k, g_to_end)
        big = each(lambda a_, r_, b_, k_: nt(jnp.concatenate([a_, r_], axis=0), jnp.concatenate([b_, k_], axis=0)),
                   a_u, r_u, b_s, k_s)
        l_ab = each(lambda z: jnp.where(strict, z[:C, :LANES], 0.0), big)
        l_ak = each(lambda z: jnp.where(strict, z[:C, LANES:], 0.0), big)
        p_rb = each(lambda z: jnp.where(incl, z[C:, :LANES], 0.0), big)
        p_rk = each(lambda z: jnp.where(incl, z[C:, LANES:], 0.0), big)
        l_d = each(lambda z: jnp.where(blk, z, 0.0), l_ab)
        l_o = each(lambda x, y: x - y, l_ab, l_d)
        l2 = each(lambda d: hp(d, stack(d)), l_d)
        lpv = each(lambda x, y, vs: hp(jnp.concatenate([x, y], axis=0), vs), l_ak, p_rk, v_s)
        lv = each(lambda z: z[:C], lpv)
        prv = each(lambda z: z[C:], lpv)
        khv = each(tn, kh, v)
        l34 = each(lambda d, s2: hp(jnp.concatenate([d, s2], axis=0), stack(s2)), l_d, l2)
        p1 = each(lambda d, s2, s34: ident - d + s2 - s34[:C], l_d, l2, l34)
        x48 = each(lambda p, s34: hp(jnp.concatenate([p, s34[C:]], axis=0), stack(s34[C:])), p1, l34)
        p2 = each(lambda p, x: p + x[:C], p1, x48)
        t_d = each(lambda p, x: p + hp(p, stack(x[C:])), p2, x48)
        m1 = each(lambda t, o: hp(t, stack(o)), t_d, l_o)
        mm_ = each(lambda x, t: hp(x, side(stack(x), stack(t))), m1, t_d)
        y_d = each(lambda t, x: t - x[:, LANES:], t_d, mm_)
        t_inv = each(lambda y, x: y + hp(x[:, :LANES], stack(y)), y_d, mm_)
        w = each(lambda t, x, y: hp(t, side(stack(x), stack(y))), t_inv, a_u, lv)
        pw = each(lambda p, x: hp(p, side(stack(x[:, :LANES]), stack(x[:, LANES:]))), p_rb, w)
        bw = each(tn, bh, w)
        q1 = each(lambda x, y: x - y[:, :LANES], r_u, pw)
        y2 = each(lambda x, y: x - y[:, LANES:], prv, pw)
        g_mat = each(lambda t, y: jnp.where(eye, jnp.exp(t), 0.0) - diag_blocks(y[:, :LANES]), total, bw)
        h_mat = each(lambda x, y: diag_blocks(x) - diag_blocks(y[:, LANES:]), khv, bw)
        s = [state[p] for p in pairs]
        for c in order:
            for i, p in enumerate(pairs):
                u = i * n_chunks + c
                ys = hp(jnp.concatenate([q1[u], g_mat[u]], axis=0), stack(s[i]))
                y_ref[0, c * C:(c + 1) * C, p * LANES:(p + 1) * LANES] = ys[:C] + y2[u]
                s[i] = ys[C:] + h_mat[u]
        for i, p in enumerate(pairs):
            state[p] = s[i]
        return s

    final_state = chain(list(range(n_pairs)))

    @pl.when(pl.program_id(2) == pl.num_programs(2) - 1)
    def _():
        for p in range(n_pairs):
            s_ref[0, p] = final_state[p]


WKV_PAIRS = 4
WKV_CHUNKS = 4


def _wkv(r, lw, k, v, kk, a, s0, reverse):
    B, T, _ = r.shape
    rows = min(T, WKV_CHUNKS * CHUNK)
    n_steps = T // rows
    width = WKV_PAIRS * LANES
    tmap = (lambda b, p, j: (b, n_steps - 1 - j, p)) if reverse else (lambda b, p, j: (b, j, p))
    seq_spec = pl.BlockSpec((1, rows, width), tmap)
    st_spec = pl.BlockSpec((1, WKV_PAIRS, RWKV_HEAD, LANES), lambda b, p, j: (b, p, 0, 0))
    return pl.pallas_call(
        functools.partial(_wkv_kernel, n_chunks=rows // CHUNK, n_pairs=WKV_PAIRS, reverse=reverse),
        grid=(B, PAIRS // WKV_PAIRS, n_steps),
        in_specs=[seq_spec] * 6 + [st_spec],
        out_specs=[seq_spec, st_spec],
        out_shape=[jax.ShapeDtypeStruct(r.shape, F32), jax.ShapeDtypeStruct((B, PAIRS, RWKV_HEAD, LANES), F32)],
        scratch_shapes=[pltpu.VMEM((WKV_PAIRS, RWKV_HEAD, LANES), F32)],
        compiler_params=_params("parallel", "parallel", "arbitrary"),
    )(r, lw, k, v, kk, a, s0)


def _rwkv_branch(y0_ref, y1_ref, r_ref, v_ref, k0_ref, k1_ref, g_ref, rk_ref, lng_ref, lnb_ref, wo_ref):
    y = y0_ref[0] + y1_ref[0]
    inv_n = 1.0 / RWKV_HEAD
    mean = _seg_sum_wide(y) * inv_n
    yc = y - mean
    var = _seg_sum_wide(yc * yc) * inv_n
    yn = yc * lax.rsqrt(var + GN_EPS) * lng_ref[...] + lnb_ref[...]
    f32 = lambda ref: ref[0].astype(F32)
    bonus = _seg_sum_wide(f32(r_ref) * (f32(k0_ref) + f32(k1_ref)) * rk_ref[...]) * f32(v_ref)
    return jnp.dot(((yn + bonus) * f32(g_ref)).astype(BF16), wo_ref[...], preferred_element_type=F32)


def _rwkv_branch_specs():
    vec = _const_spec((1, D_MODEL))
    return [_row_spec] * 7 + [vec, vec, vec, _const_spec((D_MODEL, D_MODEL))]


HALO = SUBLANES


def _pool_branch(x_ref, xp_ref, xn_ref, g_ref, mod_ref, w_ref, sc_ref, *, seq_len):
    tm = x_ref.shape[1]
    i = pl.program_id(1)
    last = pl.num_programs(1) - 1
    g = g_ref[...]
    mod = mod_ref[0]
    h = _norm_mod(x_ref[0], g, mod, 0)
    hp = _norm_mod(xp_ref[0], g, mod, 0) * jnp.where(i == 0, 0.0, 1.0)
    hn = _norm_mod(xn_ref[0], g, mod, 0) * jnp.where(i == last, 0.0, 1.0)
    ext = jnp.concatenate([hp, h, hn], axis=0)
    n_ext = tm + 2 * HALO
    t = i * tm + lax.broadcasted_iota(jnp.int32, (tm, 1), 0)
    outs = []
    for gi, win in enumerate(POOL_WINDOWS):
        e = ext[:, gi * POOL_GROUP:(gi + 1) * POOL_GROUP]
        acc = e + pltpu.roll(e, 1, 0)
        step = 1
        while 2 * step < win:
            acc = pltpu.roll(acc, step, 0) + pltpu.roll(acc, n_ext - step, 0)
            step *= 2
        half = win // 2
        cnt = (jnp.minimum(t + half, seq_len) - jnp.maximum(t - half, 0)).astype(F32)
        pooled = acc[HALO:HALO + tm] / cnt - e[HALO:HALO + tm]
        outs.append(jnp.dot(pooled.astype(BF16), w_ref[gi], preferred_element_type=F32))
    return jnp.concatenate(outs, axis=1) * sc_ref[...]


def _pool_branch_specs(seq_len):
    n_blk = seq_len // HALO
    prev_spec = lambda tm: pl.BlockSpec((1, HALO, D_MODEL), lambda b, i: (b, jnp.maximum(i * (tm // HALO) - 1, 0), 0))
    next_spec = lambda tm: pl.BlockSpec(
        (1, HALO, D_MODEL), lambda b, i: (b, jnp.minimum((i + 1) * (tm // HALO), n_blk - 1), 0))
    return [_row_spec, prev_spec, next_spec, _const_spec((1, D_MODEL)), lambda tm: _mod_spec(),
            _const_spec((len(POOL_WINDOWS), POOL_GROUP, POOL_GROUP)), _const_spec((1, D_MODEL))]


def _rope_tables(n_tokens):
    rows = n_tokens // GRID_W
    n_freq = HEAD_DIM // 4
    inv = ROPE_THETA ** (-jnp.arange(n_freq, dtype=F32) / n_freq)
    ang_r = jnp.arange(rows, dtype=F32)[:, None] * inv
    ang_c = jnp.arange(GRID_W, dtype=F32)[:, None] * inv
    ang = jnp.concatenate([
        jnp.broadcast_to(ang_r[:, None, :], (rows, GRID_W, n_freq)),
        jnp.broadcast_to(ang_c[None, :, :], (rows, GRID_W, n_freq))], axis=-1).reshape(rows * GRID_W, 2 * n_freq)
    cos, sin = jnp.cos(ang), jnp.sin(ang)
    return jnp.tile(cos, (1, 4)), jnp.tile(jnp.concatenate([-sin, sin], axis=-1), (1, 2))


def kernel(x, c, ctx, c_ctx, w_mod, b_mod, norm1_g, norm2_g, mlp_w_in, mlp_w_out, attn_w_qkv, attn_q_gain, attn_k_gain, attn_w_o, rwkv_mu, rwkv_w_rkv, rwkv_w0, rwkv_w1, rwkv_w2, rwkv_a0, rwkv_a1, rwkv_a2, rwkv_g1, rwkv_g2, rwkv_k_k, rwkv_k_a, rwkv_r_k, rwkv_ln_g, rwkv_ln_b, rwkv_w_o, pool_w, pool_scale):
    B, S, _ = x.shape
    L = ctx.shape[1]
    depth = w_mod.shape[0]
    assert x.shape[2] == D_MODEL and S % (4 * CHUNK) == 0 and L % CHUNK == 0 and S % GRID_W == 0

    n_rows = -(-(B + 1) // SUBLANES) * SUBLANES
    cvec = jnp.concatenate([c, c_ctx[None], jnp.zeros((n_rows - B - 1, D_MODEL), F32)], axis=0)
    mod_all = _modulation(cvec, w_mod, b_mod).reshape(depth, n_rows, N_MOD, D_MODEL)
    cos_t, sin_t = _rope_tables(S)
    zero_tab = jnp.zeros((L, LANES), F32)
    row = lambda a: a.reshape(1, -1)

    for i in range(depth):
        last = i == depth - 1
        j = i // N_MIXERS
        mod_l = mod_all[i, :B]
        mod_c = jnp.broadcast_to(mod_all[i, B][None], (B, N_MOD, D_MODEL))
        g1 = row(norm1_g[i])
        kind = i % N_MIXERS
        if kind == 0:
            w_qkv = attn_w_qkv[j].astype(BF16)
            w_o = attn_w_o[j].astype(BF16)
            qg = jnp.tile(row(attn_q_gain[j]), (1, 2))
            kg = jnp.tile(row(attn_k_gain[j]), (1, 2))
            q_l, k_l, v_l = _qkv_project(x, g1, mod_l, w_qkv, qg, kg, cos_t, sin_t, True)
            q_c, k_c, v_c = _qkv_project(ctx, g1, mod_c, w_qkv, qg, kg, zero_tab, zero_tab, False)
            proj_specs = [_row_spec, _const_spec((D_MODEL, D_MODEL))]
            mix_l = (_proj_branch, (_attention(q_l, [(k_l, v_l), (k_c, v_c)]), w_o), proj_specs, 512)
            if not last:
                mix_c = (_proj_branch, (_attention(q_c, [(k_c, v_c)]), w_o), proj_specs, 512)
        elif kind == 1:
            p = {
                "mu": rwkv_mu[j], "w_rkv": rwkv_w_rkv[j].astype(BF16),
                "w1": jnp.concatenate([rwkv_w1[j, 0], rwkv_w1[j, 1]], axis=1).astype(BF16),
                "w2": rwkv_w2[j].astype(BF16), "w0": rwkv_w0[j],
                "a1": jnp.concatenate([rwkv_a1[j, 0], rwkv_a1[j, 1]], axis=1).astype(BF16),
                "a2": rwkv_a2[j].astype(BF16), "a0": rwkv_a0[j],
                "g1": rwkv_g1[j].astype(BF16), "g2": rwkv_g2[j].astype(BF16),
                "k_k": row(rwkv_k_k[j]), "k_a": row(rwkv_k_a[j]), "r_k": row(rwkv_r_k[j]),
                "ln_g": row(rwkv_ln_g[j]), "ln_b": row(rwkv_ln_b[j]), "w_o": rwkv_w_o[j].astype(BF16),
            }
            r_l, v_l, kk_l, g_l, lw0_l, lw1_l, key0_l, key1_l, a0_l, a1_l = _rwkv_prepare(x, g1, mod_l, p)
            r_c, v_c, kk_c, g_c, lw0_c, lw1_c, key0_c, key1_c, a0_c, a1_c = _rwkv_prepare(ctx, g1, mod_c, p)
            zero_state = jnp.zeros((B, PAIRS, RWKV_HEAD, LANES), F32)
            ys_l, ys_c = [], []
            for rev, (lw_l, key_l, a_l, lw_c, key_c, a_c) in enumerate(
                    ((lw0_l, key0_l, a0_l, lw0_c, key0_c, a0_c), (lw1_l, key1_l, a1_l, lw1_c, key1_c, a1_c))):
                y_c, s_ctx = _wkv(r_c, lw_c, key_c, v_c, kk_c, a_c, zero_state, bool(rev))
                y_l, _ = _wkv(r_l, lw_l, key_l, v_l, kk_l, a_l, s_ctx, bool(rev))
                ys_l.append(y_l)
                ys_c.append(y_c)
            consts = (p["r_k"], p["ln_g"], p["ln_b"], p["w_o"])
            mix_l = (_rwkv_branch, (ys_l[0], ys_l[1], r_l, v_l, key0_l, key1_l, g_l) + consts, _rwkv_branch_specs(), 256)
            if not last:
                mix_c = (_rwkv_branch, (ys_c[0], ys_c[1], r_c, v_c, key0_c, key1_c, g_c) + consts,
                         _rwkv_branch_specs(), 256)
        else:
            w_p = pool_w[j].astype(BF16)
            sc = row(pool_scale[j])
            mix_l = (functools.partial(_pool_branch, seq_len=S), (x, x, x, g1, mod_l, w_p, sc), _pool_branch_specs(S), 256)
            if not last:
                mix_c = (functools.partial(_pool_branch, seq_len=L), (ctx, ctx, ctx, g1, mod_c, w_p, sc),
                         _pool_branch_specs(L), 256)
        g2 = row(norm2_g[i])
        w_in = mlp_w_in[i].astype(BF16)
        w_out = mlp_w_out[i].astype(BF16)
        x = _mixer_mlp(x, mod_l, g2, w_in, w_out, *mix_l)
        if not last:
            ctx = _mixer_mlp(ctx, mod_c, g2, w_in, w_out, *mix_c)
    return x
```

```python
import functools

import jax
import jax.numpy as jnp
import numpy as np
from jax import lax
from jax.experimental import pallas as pl
from jax.experimental.pallas import tpu as pltpu

F32 = jnp.float32
BF16 = jnp.bfloat16

D_MODEL = 1024
GRID_W = 64
N_MIXERS = 3
N_MOD = 6
EPS = 1e-6
N_HEADS = 16
N_KV_HEADS = 4
HEAD_DIM = 64
GQA_REP = N_HEADS // N_KV_HEADS
Q_WIDTH = N_HEADS * HEAD_DIM
KV_WIDTH = N_KV_HEADS * HEAD_DIM
QKV_WIDTH = Q_WIDTH + 2 * KV_WIDTH
ROPE_THETA = 10000.0
RWKV_HEAD = 64
DECAY_LORA = 64
ICLR_LORA = 64
GATE_LORA = 160
GN_EPS = RWKV_HEAD * 1e-5
POOL_WINDOWS = (2, 4, 8, 16)
POOL_GROUP = D_MODEL // len(POOL_WINDOWS)
D_FF = 4 * D_MODEL

LANES = 128
SUBLANES = 8
PAIRS = D_MODEL // LANES
CHUNK = 64
INV_BLOCK = 16
VMEM_LIMIT = 56 * 1024 * 1024

HIGHEST = lax.Precision.HIGHEST


def _params(*sem):
    return pltpu.CompilerParams(dimension_semantics=sem, vmem_limit_bytes=VMEM_LIMIT)


def _const_spec(shape):
    zeros = (0,) * len(shape)
    return pl.BlockSpec(shape, lambda *_: zeros, pipeline_mode=pl.Buffered(1))


def _row_spec(tm, width=D_MODEL):
    return pl.BlockSpec((1, tm, width), lambda b, i: (b, i, 0))


def _mod_spec():
    return pl.BlockSpec((1, N_MOD, D_MODEL), lambda b, i: (b, 0, 0))


def _norm_mod(x, g, mod, k):
    y = x * lax.rsqrt(jnp.mean(x * x, axis=-1, keepdims=True) + EPS) * g
    return y * (1.0 + mod[k + 1:k + 2]) + mod[k:k + 1]


def _seg_sum(x):
    i = lax.broadcasted_iota(jnp.int32, (LANES, LANES), 0) // HEAD_DIM
    j = lax.broadcasted_iota(jnp.int32, (LANES, LANES), 1) // HEAD_DIM
    return jnp.dot(x.astype(BF16), jnp.where(i == j, 1.0, 0.0).astype(BF16), preferred_element_type=F32)


def _sigmoid(z):
    return 0.5 * jnp.tanh(0.5 * z) + 0.5


def _seg_sum_wide(x):
    return jnp.concatenate([_seg_sum(x[:, p * LANES:(p + 1) * LANES]) for p in range(x.shape[1] // LANES)], axis=1)


def _mod_kernel(c_ref, w_ref, b_ref, o_ref):
    c = c_ref[...]
    s = c * jax.nn.sigmoid(c)
    o_ref[0] = jnp.dot(s, w_ref[0], precision=HIGHEST, preferred_element_type=F32) + b_ref[0]


def _modulation(cvec, w_mod, b_mod):
    depth = w_mod.shape[0]
    rows = cvec.shape[0]
    tn = 1536
    return pl.pallas_call(
        _mod_kernel,
        grid=(depth, N_MOD * D_MODEL // tn),
        in_specs=[pl.BlockSpec((rows, D_MODEL), lambda l, j: (0, 0)),
                  pl.BlockSpec((1, D_MODEL, tn), lambda l, j: (l, 0, j)),
                  pl.BlockSpec((1, 1, tn), lambda l, j: (l, 0, j))],
        out_specs=pl.BlockSpec((1, rows, tn), lambda l, j: (l, 0, j)),
        out_shape=jax.ShapeDtypeStruct((depth, rows, N_MOD * D_MODEL), F32),
        compiler_params=_params("parallel", "parallel"),
    )(cvec, w_mod, b_mod.reshape(depth, 1, N_MOD * D_MODEL))


VT_ROWS = HEAD_DIM + 16
Q_SCALE = float(HEAD_DIM ** -0.5 * np.log2(np.e))


def _qkv_kernel(x_ref, g_ref, mod_ref, w_ref, qg_ref, kg_ref, cos_ref, sin_ref, q_ref, k_ref, v_ref, *, use_rope):
    h = _norm_mod(x_ref[0], g_ref[...], mod_ref[0], 0).astype(BF16)
    acc = jnp.dot(h, w_ref[...], preferred_element_type=F32)
    lane = lax.broadcasted_iota(jnp.int32, (1, LANES), 1)
    first_half = (lane % HEAD_DIM) < (HEAD_DIM // 2)

    def head_pair(xp, gain, scale):
        y = xp * lax.rsqrt(_seg_sum(xp * xp) * (1.0 / HEAD_DIM) + EPS) * gain
        if use_rope:
            rot = jnp.where(first_half, pltpu.roll(y, LANES - HEAD_DIM // 2, 1), pltpu.roll(y, HEAD_DIM // 2, 1))
            y = y * cos_ref[...] + rot * sin_ref[...]
        return (y * scale).astype(BF16)

    for p in range(Q_WIDTH // LANES):
        y = head_pair(acc[:, p * LANES:(p + 1) * LANES], qg_ref[...], Q_SCALE)
        q_ref[0, 2 * p] = y[:, :HEAD_DIM]
        q_ref[0, 2 * p + 1] = y[:, HEAD_DIM:]
    for p in range(KV_WIDTH // LANES):
        y = head_pair(acc[:, Q_WIDTH + p * LANES:Q_WIDTH + (p + 1) * LANES], kg_ref[...], 1.0)
        k_ref[0, 2 * p] = y[:, :HEAD_DIM]
        k_ref[0, 2 * p + 1] = y[:, HEAD_DIM:]
    ones = jnp.ones((VT_ROWS - HEAD_DIM, acc.shape[0]), F32)
    for p in range(KV_WIDTH // LANES):
        c0 = Q_WIDTH + KV_WIDTH + p * LANES
        vt = acc[:, c0:c0 + LANES].T
        v_ref[0, 2 * p] = jnp.concatenate([vt[:HEAD_DIM], ones], axis=0).astype(BF16)
        v_ref[0, 2 * p + 1] = jnp.concatenate([vt[HEAD_DIM:], ones], axis=0).astype(BF16)


def _qkv_project(x, g, mod, w_bf, qg, kg, cos_t, sin_t, use_rope):
    B, T, _ = x.shape
    tm = min(T, 512)
    tab_spec = pl.BlockSpec((tm, LANES), lambda b, i: (i, 0))
    head_spec = lambda n: pl.BlockSpec((1, n, tm, HEAD_DIM), lambda b, i: (b, 0, i, 0))
    return pl.pallas_call(
        functools.partial(_qkv_kernel, use_rope=use_rope),
        grid=(B, T // tm),
        in_specs=[_row_spec(tm), _const_spec((1, D_MODEL)), _mod_spec(), _const_spec((D_MODEL, QKV_WIDTH)),
                  _const_spec((1, LANES)), _const_spec((1, LANES)), tab_spec, tab_spec],
        out_specs=[head_spec(N_HEADS), head_spec(N_KV_HEADS),
                   pl.BlockSpec((1, N_KV_HEADS, VT_ROWS, tm), lambda b, i: (b, 0, 0, i))],
        out_shape=[jax.ShapeDtypeStruct((B, N_HEADS, T, HEAD_DIM), BF16),
                   jax.ShapeDtypeStruct((B, N_KV_HEADS, T, HEAD_DIM), BF16),
                   jax.ShapeDtypeStruct((B, N_KV_HEADS, VT_ROWS, T), BF16)],
        compiler_params=_params("parallel", "parallel"),
    )(x, g, mod, w_bf, qg, kg, cos_t, sin_t)


def _attn_kernel(*refs, n_src, tq):
    q_ref = refs[0]
    kv_refs = refs[1:1 + 2 * n_src]
    o_ref = refs[1 + 2 * n_src]
    s_scr, m_scr = refs[2 + 2 * n_src:]
    nq = GQA_REP * tq
    i = pl.program_id(0)
    pieces = []
    row = 0
    for s in range(n_src):
        keys_total = kv_refs[2 * s].shape[2]
        ck = min(ATTN_KV_CHUNK, keys_total)
        for r0 in range(0, keys_total, ck):
            pieces.append((kv_refs[2 * s], kv_refs[2 * s + 1], r0, ck, row))
            row += ck

    def run(with_pass2):
        q = q_ref[0].reshape(nq, HEAD_DIM)
        m8 = None
        if with_pass2:
            m_prev = m_scr[...]
            acc = jnp.zeros((VT_ROWS, nq), F32)
        for k_ref, vt_ref, r0, ck, c0 in pieces:
            if with_pass2:
                p = jnp.exp2(s_scr[c0:c0 + ck, :] - m_prev).astype(BF16)
                acc = acc + jnp.dot(vt_ref[0, 0, :, r0:r0 + ck], p, preferred_element_type=F32)
            s_blk = lax.dot_general(k_ref[0, 0, r0:r0 + ck, :], q, (((1,), (1,)), ((), ())),
                                    preferred_element_type=F32)
            s_scr[c0:c0 + ck, :] = s_blk
            bm = jnp.max(s_blk.reshape(ck // SUBLANES, SUBLANES, nq), axis=0)
            m8 = bm if m8 is None else jnp.maximum(m8, bm)
        m_scr[...] = jnp.max(m8, axis=0, keepdims=True)
        if with_pass2:
            o = (acc[:HEAD_DIM] / acc[HEAD_DIM:HEAD_DIM + 1]).T
            o_ref[0] = jnp.concatenate([o[h * tq:(h + 1) * tq] for h in range(GQA_REP)], axis=1).astype(BF16)

    pl.when(i == 0)(lambda: run(False))
    pl.when(i > 0)(lambda: run(True))


ATTN_KV_CHUNK = 256


def _attention(q, kv_sources):
    B, _, T, _ = q.shape
    tq = min(T, 256)
    n_tiles = T // tq
    n_all = B * N_KV_HEADS * n_tiles
    total = sum(k.shape[2] for k, _ in kv_sources)

    def tile(t):
        t = jnp.clip(t, 0, n_all - 1)
        return t // (N_KV_HEADS * n_tiles), (t // n_tiles) % N_KV_HEADS, t % n_tiles

    def q_map(t):
        b, g, i = tile(t)
        return b, g, i, 0

    def k_map(t):
        b, g, _ = tile(t)
        return b, g, 0, 0

    def v_map(t):
        b, g, _ = tile(t - 1)
        return b, g, 0, 0

    def o_map(t):
        b, g, i = tile(t - 1)
        return b, i, g

    in_specs = [pl.BlockSpec((1, GQA_REP, tq, HEAD_DIM), q_map)]
    args = [q]
    for k, vt in kv_sources:
        in_specs += [pl.BlockSpec((1, 1, k.shape[2], HEAD_DIM), k_map),
                     pl.BlockSpec((1, 1, VT_ROWS, k.shape[2]), v_map)]
        args += [k, vt]
    return pl.pallas_call(
        functools.partial(_attn_kernel, n_src=len(kv_sources), tq=tq),
        grid=(n_all + 1,),
        in_specs=in_specs,
        out_specs=pl.BlockSpec((1, tq, GQA_REP * HEAD_DIM), o_map),
        out_shape=jax.ShapeDtypeStruct((B, T, Q_WIDTH), BF16),
        scratch_shapes=[pltpu.VMEM((total, GQA_REP * tq), F32), pltpu.VMEM((1, GQA_REP * tq), F32)],
        compiler_params=_params("arbitrary"),
    )(*args)


def _proj_branch(y_ref, w_ref):
    return jnp.dot(y_ref[0], w_ref[...], preferred_element_type=F32)


def _mixer_mlp_kernel(*refs, branch, ff_chunk):
    x_ref, mod_ref, g_ref, win_ref, wout_ref = refs[:5]
    o_ref = refs[-1]
    x = x_ref[0]
    mod = mod_ref[0]
    if branch is not None:
        x = x + mod[2:3] * branch(*refs[5:-1])
    h = _norm_mod(x, g_ref[...], mod, 3).astype(BF16)
    acc = jnp.zeros(x.shape, F32)
    for f0 in range(0, D_FF, ff_chunk):
        u = jnp.maximum(jnp.dot(h, win_ref[:, f0:f0 + ff_chunk], preferred_element_type=F32), 0.0)
        acc = acc + jnp.dot((u * u).astype(BF16), wout_ref[f0:f0 + ff_chunk, :], preferred_element_type=F32)
    o_ref[0] = x + mod[5:6] * acc


def _mixer_mlp(x, mod, g, win_bf, wout_bf, branch=None, branch_args=(), branch_specs=(), tm_max=512):
    B, T, _ = x.shape
    tm = min(T, tm_max)
    specs = [s(tm) if callable(s) else s for s in branch_specs]
    return pl.pallas_call(
        functools.partial(_mixer_mlp_kernel, branch=branch, ff_chunk=1024),
        grid=(B, T // tm),
        in_specs=[_row_spec(tm), _mod_spec(), _const_spec((1, D_MODEL)),
                  _const_spec((D_MODEL, D_FF)), _const_spec((D_FF, D_MODEL))] + specs,
        out_specs=_row_spec(tm),
        out_shape=jax.ShapeDtypeStruct(x.shape, F32),
        compiler_params=_params("parallel", "parallel"),
    )(x, mod, g, win_bf, wout_bf, *branch_args)


DECAY_SCALE = float(np.exp(-0.5))


def _rwkv_prep_kernel(x_ref, xp_ref, xn_ref, g_ref, mod_ref, mu_ref, wrkv_ref, w1_ref, w2_ref, w0_ref, a1_ref, a2_ref,
                      a0_ref, g1_ref, g2_ref, kk_ref, ka_ref,
                      r_o, v_o, kk_o, g_o, lw0_o, lw1_o, key0_o, key1_o, a0_o, a1_o, *, tm):
    i = pl.program_id(1)
    last = pl.num_programs(1) - 1
    g = g_ref[...]
    mod = mod_ref[0]
    h = _norm_mod(x_ref[0], g, mod, 0)
    hp = _norm_mod(xp_ref[0][SUBLANES - 1:SUBLANES], g, mod, 0) * jnp.where(i == 0, 0.0, 1.0)
    hn = _norm_mod(xn_ref[0][0:1], g, mod, 0) * jnp.where(i == last, 0.0, 1.0)
    row = lax.broadcasted_iota(jnp.int32, (tm, 1), 0)
    prev = jnp.where(row == 0, hp, pltpu.roll(h, 1, 0))
    nxt = jnp.where(row == tm - 1, hn, pltpu.roll(h, tm - 1, 0))
    xx = 0.5 * (prev + nxt) - h
    mix = lambda m: (h + xx * mu_ref[m:m + 1]).astype(BF16)
    dot = functools.partial(jnp.dot, preferred_element_type=F32)
    r = dot(mix(0), wrkv_ref[0])
    k = dot(mix(2), wrkv_ref[1])
    v = dot(mix(3), wrkv_ref[2])
    r_o[0] = r.astype(BF16)
    v_o[0] = v.astype(BF16)
    g_o[0] = dot(_sigmoid(dot(mix(5), g1_ref[...])).astype(BF16), g2_ref[...]).astype(BF16)
    kk = k * kk_ref[...]
    kk_o[0] = (kk * lax.rsqrt(jnp.maximum(_seg_sum_wide(kk * kk), 1e-24))).astype(BF16)
    tw = jnp.tanh(dot(mix(1), w1_ref[...])).astype(BF16)
    ta = dot(mix(4), a1_ref[...]).astype(BF16)
    for d, (lw_o, key_o, a_o) in enumerate(((lw0_o, key0_o, a0_o), (lw1_o, key1_o, a1_o))):
        z = w0_ref[d:d + 1] + dot(tw[:, d * DECAY_LORA:(d + 1) * DECAY_LORA], w2_ref[d])
        lw_o[0] = -DECAY_SCALE * _sigmoid(z)
        a = _sigmoid(a0_ref[d:d + 1] + dot(ta[:, d * ICLR_LORA:(d + 1) * ICLR_LORA], a2_ref[d]))
        a_o[0] = a.astype(BF16)
        key_o[0] = (k * (1.0 + (a - 1.0) * ka_ref[...])).astype(BF16)


def _rwkv_prepare(x, g, mod, p):
    B, T, _ = x.shape
    tm = min(T, 256)
    nb = tm // SUBLANES
    n_blk8 = T // SUBLANES
    prev_spec = pl.BlockSpec((1, SUBLANES, D_MODEL), lambda b, i: (b, jnp.maximum(i * nb - 1, 0), 0))
    next_spec = pl.BlockSpec((1, SUBLANES, D_MODEL), lambda b, i: (b, jnp.minimum((i + 1) * nb, n_blk8 - 1), 0))
    o_bf, o_f32 = jax.ShapeDtypeStruct(x.shape, BF16), jax.ShapeDtypeStruct(x.shape, F32)
    return pl.pallas_call(
        functools.partial(_rwkv_prep_kernel, tm=tm),
        grid=(B, T // tm),
        in_specs=[_row_spec(tm), prev_spec, next_spec, _const_spec((1, D_MODEL)), _mod_spec(),
                  _const_spec((6, D_MODEL)), _const_spec((3, D_MODEL, D_MODEL)),
                  _const_spec((D_MODEL, 2 * DECAY_LORA)), _const_spec((2, DECAY_LORA, D_MODEL)),
                  _const_spec((2, D_MODEL)),
                  _const_spec((D_MODEL, 2 * ICLR_LORA)), _const_spec((2, ICLR_LORA, D_MODEL)),
                  _const_spec((2, D_MODEL)),
                  _const_spec((D_MODEL, GATE_LORA)), _const_spec((GATE_LORA, D_MODEL)),
                  _const_spec((1, D_MODEL)), _const_spec((1, D_MODEL))],
        out_specs=[_row_spec(tm)] * 10,
        out_shape=[o_bf, o_bf, o_bf, o_bf, o_f32, o_f32, o_bf, o_bf, o_bf, o_bf],
        compiler_params=_params("parallel", "parallel"),
    )(x, x, x, g, mod, p["mu"], p["w_rkv"], p["w1"], p["w2"], p["w0"], p["a1"], p["a2"], p["a0"], p["g1"], p["g2"],
      p["k_k"], p["k_a"])


def _wkv_kernel(r_ref, lw_ref, k_ref, v_ref, kk_ref, a_ref, s0_ref, y_ref, s_ref, state, *, n_chunks, n_pairs,
                reverse):
    C = CHUNK

    def mm(dims):
        return lambda a, b: lax.dot_general(a.astype(BF16), b.astype(BF16), dims, preferred_element_type=F32)

    hp = mm((((1,), (0,)), ((), ())))
    nt = mm((((1,), (1,)), ((), ())))
    tn = mm((((0,), (0,)), ((), ())))

    @pl.when(pl.program_id(2) == 0)
    def _():
        state[...] = s0_ref[0]

    pi = lax.broadcasted_iota(jnp.int32, (C, LANES), 0)
    pj = lax.broadcasted_iota(jnp.int32, (C, LANES), 1) % C
    strict = (pi < pj) if reverse else (pi > pj)
    incl = (pi <= pj) if reverse else (pi >= pj)
    blk = (pi // INV_BLOCK) == (pj // INV_BLOCK)
    eye = pi == pj
    ident = jnp.where(eye, 1.0, 0.0)
    head0 = lax.broadcasted_iota(jnp.int32, (1, LANES), 1) < RWKV_HEAD

    def stack(z):
        zb = z.astype(BF16)
        zero = jnp.zeros_like(zb)
        return jnp.concatenate([jnp.where(head0, zb, zero), jnp.where(head0, zero, zb)], axis=0)

    side = lambda x, y: jnp.concatenate([x, y], axis=1)
    diag_blocks = lambda z: jnp.where(head0, z[:C], z[C:])

    each = lambda f, *ls: [f(*xs) for xs in zip(*ls)]
    order = range(n_chunks - 1, -1, -1) if reverse else range(n_chunks)

    def cumsum(x):
        step = 1
        while step < C:
            if reverse:
                x = x + jnp.where(pi < C - step, pltpu.roll(x, C - step, 0), 0.0)
            else:
                x = x + jnp.where(pi >= step, pltpu.roll(x, step, 0), 0.0)
            step *= 2
        return x

    def chain(pairs):
        units = [(slice(c * C, (c + 1) * C), slice(p * LANES, (p + 1) * LANES)) for p in pairs for c in range(n_chunks)]
        load = lambda ref: [ref[0, rows, cols].astype(F32) for rows, cols in units]
        r, lw, k, v, kk, a = (load(ref) for ref in (r_ref, lw_ref, k_ref, v_ref, kk_ref, a_ref))
        cum = each(cumsum, lw)
        total = each(lambda z: z[0:1] if reverse else z[C - 1:C], cum)
        g_in = each(lambda z: jnp.exp(-z), cum)
        g_to_end = each(lambda t, z: jnp.exp(t - z), total, cum)
        b = each(lambda x, y: x * y, kk, a)
        a_u = each(lambda x, z, l: x * jnp.exp(z - l), kk, cum, lw)
        r_u = each(lambda x, z: x * jnp.exp(z), r, cum)
        b_s = each(lambda x, y: stack(x * y), b, g_in)
        k_s = each(lambda x, y: stack(x * y), k, g_in)
        v_s = each(stack, v)
        bh = each(lambda x, y: x * y, b, g_to_end)
        kh = each(lambda x, y: x * y, k, g_to_end)
        big = each(lambda a_, r_, b_, k_: nt(jnp.concatenate([a_, r_], axis=0), jnp.concatenate([b_, k_], axis=0)),
                   a_u, r_u, b_s, k_s)
        l_ab = each(lambda z: jnp.where(strict, z[:C, :LANES], 0.0), big)
        l_ak = each(lambda z: jnp.where(strict, z[:C, LANES:], 0.0), big)
        p_rb = each(lambda z: jnp.where(incl, z[C:, :LANES], 0.0), big)
        p_rk = each(lambda z: jnp.where(incl, z[C:, LANES:], 0.0), big)
        l_d = each(lambda z: jnp.where(blk, z, 0.0), l_ab)
        l_o = each(lambda x, y: x - y, l_ab, l_d)
        l2 = each(lambda d: hp(d, stack(d)), l_d)
        lpv = each(lambda x, y, vs: hp(jnp.concatenate([x, y], axis=0), vs), l_ak, p_rk, v_s)
        lv = each(lambda z: z[:C], lpv)
        prv = each(lambda z: z[C:], lpv)
        khv = each(tn, kh, v)
        l34 = each(lambda d, s2: hp(jnp.concatenate([d, s2], axis=0), stack(s2)), l_d, l2)
        p1 = each(lambda d, s2, s34: ident - d + s2 - s34[:C], l_d, l2, l34)
        x48 = each(lambda p, s34: hp(jnp.concatenate([p, s34[C:]], axis=0), stack(s34[C:])), p1, l34)
        p2 = each(lambda p, x: p + x[:C], p1, x48)
        t_d = each(lambda p, x: p + hp(p, stack(x[C:])), p2, x48)
        m1 = each(lambda t, o: hp(t, stack(o)), t_d, l_o)
        mm_ = each(lambda x, t: hp(x, side(stack(x), stack(t))), m1, t_d)
        y_d = each(lambda t, x: t - x[:, LANES:], t_d, mm_)
        t_inv = each(lambda y, x: y + hp(x[:, :LANES], stack(y)), y_d, mm_)
        w = each(lambda t, x, y: hp(t, side(stack(x), stack(y))), t_inv, a_u, lv)
        pw = each(lambda p, x: hp(p, side(stack(x[:, :LANES]), stack(x[:, LANES:]))), p_rb, w)
        bw = each(tn, bh, w)
        q1 = each(lambda x, y: x - y[:, :LANES], r_u, pw)
        y2 = each(lambda x, y: x - y[:, LANES:], prv, pw)
        g_mat = each(lambda t, y: jnp.where(eye, jnp.exp(t), 0.0) - diag_blocks(y[:, :LANES]), total, bw)
        h_mat = each(lambda x, y: diag_blocks(x) - diag_blocks(y[:, LANES:]), khv, bw)
        s = [state[p] for p in pairs]
        for c in order:
            for i, p in enumerate(pairs):
                u = i * n_chunks + c
                ys = hp(jnp.concatenate([q1[u], g_mat[u]], axis=0), stack(s[i]))
                y_ref[0, c * C:(c + 1) * C, p * LANES:(p + 1) * LANES] = ys[:C] + y2[u]
                s[i] = ys[C:] + h_mat[u]
        for i, p in enumerate(pairs):
            state[p] = s[i]
        return s

    final_state = chain(list(range(n_pairs)))

    @pl.when(pl.program_id(2) == pl.num_programs(2) - 1)
    def _():
        for p in range(n_pairs):
            s_ref[0, p] = final_state[p]


WKV_PAIRS = 4
WKV_CHUNKS = 8


def _wkv(r, lw, k, v, kk, a, s0, reverse):
    B, T, _ = r.shape
    rows = min(T, WKV_CHUNKS * CHUNK)
    n_steps = T // rows
    width = WKV_PAIRS * LANES
    tmap = (lambda b, p, j: (b, n_steps - 1 - j, p)) if reverse else (lambda b, p, j: (b, j, p))
    seq_spec = pl.BlockSpec((1, rows, width), tmap)
    st_spec = pl.BlockSpec((1, WKV_PAIRS, RWKV_HEAD, LANES), lambda b, p, j: (b, p, 0, 0))
    return pl.pallas_call(
        functools.partial(_wkv_kernel, n_chunks=rows // CHUNK, n_pairs=WKV_PAIRS, reverse=reverse),
        grid=(B, PAIRS // WKV_PAIRS, n_steps),
        in_specs=[seq_spec] * 6 + [st_spec],
        out_specs=[seq_spec, st_spec],
        out_shape=[jax.ShapeDtypeStruct(r.shape, F32), jax.ShapeDtypeStruct((B, PAIRS, RWKV_HEAD, LANES), F32)],
        scratch_shapes=[pltpu.VMEM((WKV_PAIRS, RWKV_HEAD, LANES), F32)],
        compiler_params=_params("parallel", "parallel", "arbitrary"),
    )(r, lw, k, v, kk, a, s0)


def _rwkv_branch(y0_ref, y1_ref, r_ref, v_ref, k0_ref, k1_ref, g_ref, rk_ref, lng_ref, lnb_ref, wo_ref):
    y = y0_ref[0] + y1_ref[0]
    inv_n = 1.0 / RWKV_HEAD
    mean = _seg_sum_wide(y) * inv_n
    yc = y - mean
    var = _seg_sum_wide(yc * yc) * inv_n
    yn = yc * lax.rsqrt(var + GN_EPS) * lng_ref[...] + lnb_ref[...]
    f32 = lambda ref: ref[0].astype(F32)
    bonus = _seg_sum_wide(f32(r_ref) * (f32(k0_ref) + f32(k1_ref)) * rk_ref[...]) * f32(v_ref)
    return jnp.dot(((yn + bonus) * f32(g_ref)).astype(BF16), wo_ref[...], preferred_element_type=F32)


def _rwkv_branch_specs():
    vec = _const_spec((1, D_MODEL))
    return [_row_spec] * 7 + [vec, vec, vec, _const_spec((D_MODEL, D_MODEL))]


HALO = SUBLANES


def _pool_branch(x_ref, xp_ref, xn_ref, g_ref, mod_ref, w_ref, sc_ref, *, seq_len):
    tm = x_ref.shape[1]
    i = pl.program_id(1)
    last = pl.num_programs(1) - 1
    g = g_ref[...]
    mod = mod_ref[0]
    h = _norm_mod(x_ref[0], g, mod, 0)
    hp = _norm_mod(xp_ref[0], g, mod, 0) * jnp.where(i == 0, 0.0, 1.0)
    hn = _norm_mod(xn_ref[0], g, mod, 0) * jnp.where(i == last, 0.0, 1.0)
    ext = jnp.concatenate([hp, h, hn], axis=0)
    n_ext = tm + 2 * HALO
    t = i * tm + lax.broadcasted_iota(jnp.int32, (tm, 1), 0)
    outs = []
    for gi, win in enumerate(POOL_WINDOWS):
        e = ext[:, gi * POOL_GROUP:(gi + 1) * POOL_GROUP]
        acc = e + pltpu.roll(e, 1, 0)
        step = 1
        while 2 * step < win:
            acc = pltpu.roll(acc, step, 0) + pltpu.roll(acc, n_ext - step, 0)
            step *= 2
        half = win // 2
        cnt = (jnp.minimum(t + half, seq_len) - jnp.maximum(t - half, 0)).astype(F32)
        pooled = acc[HALO:HALO + tm] / cnt - e[HALO:HALO + tm]
        outs.append(jnp.dot(pooled.astype(BF16), w_ref[gi], preferred_element_type=F32))
    return jnp.concatenate(outs, axis=1) * sc_ref[...]


def _pool_branch_specs(seq_len):
    n_blk = seq_len // HALO
    prev_spec = lambda tm: pl.BlockSpec((1, HALO, D_MODEL), lambda b, i: (b, jnp.maximum(i * (tm // HALO) - 1, 0), 0))
    next_spec = lambda tm: pl.BlockSpec(
        (1, HALO, D_MODEL), lambda b, i: (b, jnp.minimum((i + 1) * (tm // HALO), n_blk - 1), 0))
    return [_row_spec, prev_spec, next_spec, _const_spec((1, D_MODEL)), lambda tm: _mod_spec(),
            _const_spec((len(POOL_WINDOWS), POOL_GROUP, POOL_GROUP)), _const_spec((1, D_MODEL))]


def _rope_tables(n_tokens):
    rows = n_tokens // GRID_W
    n_freq = HEAD_DIM // 4
    inv = ROPE_THETA ** (-jnp.arange(n_freq, dtype=F32) / n_freq)
    ang_r = jnp.arange(rows, dtype=F32)[:, None] * inv
    ang_c = jnp.arange(GRID_W, dtype=F32)[:, None] * inv
    ang = jnp.concatenate([
        jnp.broadcast_to(ang_r[:, None, :], (rows, GRID_W, n_freq)),
        jnp.broadcast_to(ang_c[None, :, :], (rows, GRID_W, n_freq))], axis=-1).reshape(rows * GRID_W, 2 * n_freq)
    cos, sin = jnp.cos(ang), jnp.sin(ang)
    return jnp.tile(cos, (1, 4)), jnp.tile(jnp.concatenate([-sin, sin], axis=-1), (1, 2))


def kernel(x, c, ctx, c_ctx, w_mod, b_mod, norm1_g, norm2_g, mlp_w_in, mlp_w_out, attn_w_qkv, attn_q_gain, attn_k_gain, attn_w_o, rwkv_mu, rwkv_w_rkv, rwkv_w0, rwkv_w1, rwkv_w2, rwkv_a0, rwkv_a1, rwkv_a2, rwkv_g1, rwkv_g2, rwkv_k_k, rwkv_k_a, rwkv_r_k, rwkv_ln_g, rwkv_ln_b, rwkv_w_o, pool_w, pool_scale):
    B, S, _ = x.shape
    L = ctx.shape[1]
    depth = w_mod.shape[0]
    assert x.shape[2] == D_MODEL and S % (4 * CHUNK) == 0 and L % CHUNK == 0 and S % GRID_W == 0

    n_rows = -(-(B + 1) // SUBLANES) * SUBLANES
    cvec = jnp.concatenate([c, c_ctx[None], jnp.zeros((n_rows - B - 1, D_MODEL), F32)], axis=0)
    mod_all = _modulation(cvec, w_mod, b_mod).reshape(depth, n_rows, N_MOD, D_MODEL)
    cos_t, sin_t = _rope_tables(S)
    zero_tab = jnp.zeros((L, LANES), F32)
    row = lambda a: a.reshape(1, -1)

    for i in range(depth):
        last = i == depth - 1
        j = i // N_MIXERS
        mod_l = mod_all[i, :B]
        mod_c = jnp.broadcast_to(mod_all[i, B][None], (B, N_MOD, D_MODEL))
        g1 = row(norm1_g[i])
        kind = i % N_MIXERS
        if kind == 0:
            w_qkv = attn_w_qkv[j].astype(BF16)
            w_o = attn_w_o[j].astype(BF16)
            qg = jnp.tile(row(attn_q_gain[j]), (1, 2))
            kg = jnp.tile(row(attn_k_gain[j]), (1, 2))
            q_l, k_l, v_l = _qkv_project(x, g1, mod_l, w_qkv, qg, kg, cos_t, sin_t, True)
            q_c, k_c, v_c = _qkv_project(ctx, g1, mod_c, w_qkv, qg, kg, zero_tab, zero_tab, False)
            proj_specs = [_row_spec, _const_spec((D_MODEL, D_MODEL))]
            mix_l = (_proj_branch, (_attention(q_l, [(k_l, v_l), (k_c, v_c)]), w_o), proj_specs, 512)
            if not last:
                mix_c = (_proj_branch, (_attention(q_c, [(k_c, v_c)]), w_o), proj_specs, 512)
        elif kind == 1:
            p = {
                "mu": rwkv_mu[j], "w_rkv": rwkv_w_rkv[j].astype(BF16),
                "w1": jnp.concatenate([rwkv_w1[j, 0], rwkv_w1[j, 1]], axis=1).astype(BF16),
                "w2": rwkv_w2[j].astype(BF16), "w0": rwkv_w0[j],
                "a1": jnp.concatenate([rwkv_a1[j, 0], rwkv_a1[j, 1]], axis=1).astype(BF16),
                "a2": rwkv_a2[j].astype(BF16), "a0": rwkv_a0[j],
                "g1": rwkv_g1[j].astype(BF16), "g2": rwkv_g2[j].astype(BF16),
                "k_k": row(rwkv_k_k[j]), "k_a": row(rwkv_k_a[j]), "r_k": row(rwkv_r_k[j]),
                "ln_g": row(rwkv_ln_g[j]), "ln_b": row(rwkv_ln_b[j]), "w_o": rwkv_w_o[j].astype(BF16),
            }
            r_l, v_l, kk_l, g_l, lw0_l, lw1_l, key0_l, key1_l, a0_l, a1_l = _rwkv_prepare(x, g1, mod_l, p)
            r_c, v_c, kk_c, g_c, lw0_c, lw1_c, key0_c, key1_c, a0_c, a1_c = _rwkv_prepare(ctx, g1, mod_c, p)
            zero_state = jnp.zeros((B, PAIRS, RWKV_HEAD, LANES), F32)
            ys_l, ys_c = [], []
            for rev, (lw_l, key_l, a_l, lw_c, key_c, a_c) in enumerate(
                    ((lw0_l, key0_l, a0_l, lw0_c, key0_c, a0_c), (lw1_l, key1_l, a1_l, lw1_c, key1_c, a1_c))):
                y_c, s_ctx = _wkv(r_c, lw_c, key_c, v_c, kk_c, a_c, zero_state, bool(rev))
                y_l, _ = _wkv(r_l, lw_l, key_l, v_l, kk_l, a_l, s_ctx, bool(rev))
                ys_l.append(y_l)
                ys_c.append(y_c)
            consts = (p["r_k"], p["ln_g"], p["ln_b"], p["w_o"])
            mix_l = (_rwkv_branch, (ys_l[0], ys_l[1], r_l, v_l, key0_l, key1_l, g_l) + consts, _rwkv_branch_specs(), 256)
            if not last:
                mix_c = (_rwkv_branch, (ys_c[0], ys_c[1], r_c, v_c, key0_c, key1_c, g_c) + consts,
                         _rwkv_branch_specs(), 256)
        else:
            w_p = pool_w[j].astype(BF16)
            sc = row(pool_scale[j])
            mix_l = (functools.partial(_pool_branch, seq_len=S), (x, x, x, g1, mod_l, w_p, sc), _pool_branch_specs(S), 256)
            if not last:
                mix_c = (functools.partial(_pool_branch, seq_len=L), (ctx, ctx, ctx, g1, mod_c, w_p, sc),
                         _pool_branch_specs(L), 256)
        g2 = row(norm2_g[i])
        w_in = mlp_w_in[i].astype(BF16)
        w_out = mlp_w_out[i].astype(BF16)
        x = _mixer_mlp(x, mod_l, g2, w_in, w_out, *mix_l)
        if not last:
            ctx = _mixer_mlp(ctx, mod_c, g2, w_in, w_out, *mix_c)
    return x
```

```python
import functools

import jax
import jax.numpy as jnp
import numpy as np
from jax import lax
from jax.experimental import pallas as pl
from jax.experimental.pallas import tpu as pltpu

F32 = jnp.float32
BF16 = jnp.bfloat16

D_MODEL = 1024
GRID_W = 64
N_MIXERS = 3
N_MOD = 6
EPS = 1e-6
N_HEADS = 16
N_KV_HEADS = 4
HEAD_DIM = 64
GQA_REP = N_HEADS // N_KV_HEADS
Q_WIDTH = N_HEADS * HEAD_DIM
KV_WIDTH = N_KV_HEADS * HEAD_DIM
QKV_WIDTH = Q_WIDTH + 2 * KV_WIDTH
ROPE_THETA = 10000.0
RWKV_HEAD = 64
DECAY_LORA = 64
ICLR_LORA = 64
GATE_LORA = 160
GN_EPS = RWKV_HEAD * 1e-5
POOL_WINDOWS = (2, 4, 8, 16)
POOL_GROUP = D_MODEL // len(POOL_WINDOWS)
D_FF = 4 * D_MODEL

LANES = 128
SUBLANES = 8
PAIRS = D_MODEL // LANES
CHUNK = 64
INV_BLOCK = 16
VMEM_LIMIT = 56 * 1024 * 1024

HIGHEST = lax.Precision.HIGHEST


def _params(*sem):
    return pltpu.CompilerParams(dimension_semantics=sem, vmem_limit_bytes=VMEM_LIMIT)


def _const_spec(shape):
    zeros = (0,) * len(shape)
    return pl.BlockSpec(shape, lambda *_: zeros, pipeline_mode=pl.Buffered(1))


def _row_spec(tm, width=D_MODEL):
    return pl.BlockSpec((1, tm, width), lambda b, i: (b, i, 0))


def _mod_spec():
    return pl.BlockSpec((1, N_MOD, D_MODEL), lambda b, i: (b, 0, 0))


def _norm_mod(x, g, mod, k):
    y = x * lax.rsqrt(jnp.mean(x * x, axis=-1, keepdims=True) + EPS) * g
    return y * (1.0 + mod[k + 1:k + 2]) + mod[k:k + 1]


def _seg_sum(x):
    i = lax.broadcasted_iota(jnp.int32, (LANES, LANES), 0) // HEAD_DIM
    j = lax.broadcasted_iota(jnp.int32, (LANES, LANES), 1) // HEAD_DIM
    return jnp.dot(x.astype(BF16), jnp.where(i == j, 1.0, 0.0).astype(BF16), preferred_element_type=F32)


def _sigmoid(z):
    return 0.5 * jnp.tanh(0.5 * z) + 0.5


def _seg_sum_wide(x):
    return jnp.concatenate([_seg_sum(x[:, p * LANES:(p + 1) * LANES]) for p in range(x.shape[1] // LANES)], axis=1)


def _mod_kernel(c_ref, w_ref, b_ref, o_ref):
    c = c_ref[...]
    s = c * jax.nn.sigmoid(c)
    o_ref[0] = jnp.dot(s, w_ref[0], precision=HIGHEST, preferred_element_type=F32) + b_ref[0]


def _modulation(cvec, w_mod, b_mod):
    depth = w_mod.shape[0]
    rows = cvec.shape[0]
    tn = 1536
    return pl.pallas_call(
        _mod_kernel,
        grid=(depth, N_MOD * D_MODEL // tn),
        in_specs=[pl.BlockSpec((rows, D_MODEL), lambda l, j: (0, 0)),
                  pl.BlockSpec((1, D_MODEL, tn), lambda l, j: (l, 0, j)),
                  pl.BlockSpec((1, 1, tn), lambda l, j: (l, 0, j))],
        out_specs=pl.BlockSpec((1, rows, tn), lambda l, j: (l, 0, j)),
        out_shape=jax.ShapeDtypeStruct((depth, rows, N_MOD * D_MODEL), F32),
        compiler_params=_params("parallel", "parallel"),
    )(cvec, w_mod, b_mod.reshape(depth, 1, N_MOD * D_MODEL))


VT_ROWS = HEAD_DIM + 16
Q_SCALE = float(HEAD_DIM ** -0.5 * np.log2(np.e))


def _qkv_kernel(x_ref, g_ref, mod_ref, w_ref, qg_ref, kg_ref, cos_ref, sin_ref, q_ref, k_ref, v_ref, *, use_rope):
    h = _norm_mod(x_ref[0], g_ref[...], mod_ref[0], 0).astype(BF16)
    acc = jnp.dot(h, w_ref[...], preferred_element_type=F32)
    src_lane = lax.broadcasted_iota(jnp.int32, (2 * LANES, LANES), 0) % LANES
    dst_lane = lax.broadcasted_iota(jnp.int32, (2 * LANES, LANES), 1)
    half = HEAD_DIM // 2
    first_half = (dst_lane % HEAD_DIM) < half
    rot_mat = jnp.where(first_half & (src_lane == dst_lane + half), -1.0,
                        jnp.where(jnp.logical_not(first_half) & (src_lane == dst_lane - half), 1.0, 0.0)).astype(BF16)

    def head_pair(xp, gain, scale):
        y = xp * lax.rsqrt(_seg_sum(xp * xp) * (1.0 / HEAD_DIM) + EPS) * gain
        if use_rope:
            y_hi = y.astype(BF16)
            y_lo = (y - y_hi.astype(F32)).astype(BF16)
            rot = jnp.dot(jnp.concatenate([y_hi, y_lo], axis=1), rot_mat, preferred_element_type=F32)
            y = y * cos_ref[...] + rot * sin_ref[...]
        return (y * scale).astype(BF16)

    for p in range(Q_WIDTH // LANES):
        y = head_pair(acc[:, p * LANES:(p + 1) * LANES], qg_ref[...], Q_SCALE)
        q_ref[0, 2 * p] = y[:, :HEAD_DIM]
        q_ref[0, 2 * p + 1] = y[:, HEAD_DIM:]
    for p in range(KV_WIDTH // LANES):
        y = head_pair(acc[:, Q_WIDTH + p * LANES:Q_WIDTH + (p + 1) * LANES], kg_ref[...], 1.0)
        k_ref[0, 2 * p] = y[:, :HEAD_DIM]
        k_ref[0, 2 * p + 1] = y[:, HEAD_DIM:]
    ones = jnp.ones((VT_ROWS - HEAD_DIM, acc.shape[0]), F32)
    for p in range(KV_WIDTH // LANES):
        c0 = Q_WIDTH + KV_WIDTH + p * LANES
        vt = acc[:, c0:c0 + LANES].T
        v_ref[0, 2 * p] = jnp.concatenate([vt[:HEAD_DIM], ones], axis=0).astype(BF16)
        v_ref[0, 2 * p + 1] = jnp.concatenate([vt[HEAD_DIM:], ones], axis=0).astype(BF16)


def _qkv_project(x, g, mod, w_bf, qg, kg, cos_t, sin_t, use_rope):
    B, T, _ = x.shape
    tm = min(T, 512)
    tab_spec = pl.BlockSpec((tm, LANES), lambda b, i: (i, 0))
    head_spec = lambda n: pl.BlockSpec((1, n, tm, HEAD_DIM), lambda b, i: (b, 0, i, 0))
    return pl.pallas_call(
        functools.partial(_qkv_kernel, use_rope=use_rope),
        grid=(B, T // tm),
        in_specs=[_row_spec(tm), _const_spec((1, D_MODEL)), _mod_spec(), _const_spec((D_MODEL, QKV_WIDTH)),
                  _const_spec((1, LANES)), _const_spec((1, LANES)), tab_spec, tab_spec],
        out_specs=[head_spec(N_HEADS), head_spec(N_KV_HEADS),
                   pl.BlockSpec((1, N_KV_HEADS, VT_ROWS, tm), lambda b, i: (b, 0, 0, i))],
        out_shape=[jax.ShapeDtypeStruct((B, N_HEADS, T, HEAD_DIM), BF16),
                   jax.ShapeDtypeStruct((B, N_KV_HEADS, T, HEAD_DIM), BF16),
                   jax.ShapeDtypeStruct((B, N_KV_HEADS, VT_ROWS, T), BF16)],
        compiler_params=_params("parallel", "parallel"),
    )(x, g, mod, w_bf, qg, kg, cos_t, sin_t)


def _attn_kernel(*refs, n_src, tq):
    q_ref = refs[0]
    kv_refs = refs[1:1 + 2 * n_src]
    o_ref = refs[1 + 2 * n_src]
    s_scr, m_scr = refs[2 + 2 * n_src:]
    nq = GQA_REP * tq
    i = pl.program_id(0)
    pieces = []
    row = 0
    for s in range(n_src):
        keys_total = kv_refs[2 * s].shape[2]
        ck = min(ATTN_KV_CHUNK, keys_total)
        for r0 in range(0, keys_total, ck):
            pieces.append((kv_refs[2 * s], kv_refs[2 * s + 1], r0, ck, row))
            row += ck

    def run(with_pass2):
        q = q_ref[0].reshape(nq, HEAD_DIM)
        m8 = None
        if with_pass2:
            m_prev = m_scr[...]
            acc = jnp.zeros((VT_ROWS, nq), F32)
        for k_ref, vt_ref, r0, ck, c0 in pieces:
            if with_pass2:
                p = jnp.exp2(s_scr[c0:c0 + ck, :] - m_prev).astype(BF16)
                acc = acc + jnp.dot(vt_ref[0, 0, :, r0:r0 + ck], p, preferred_element_type=F32)
            s_blk = lax.dot_general(k_ref[0, 0, r0:r0 + ck, :], q, (((1,), (1,)), ((), ())),
                                    preferred_element_type=F32)
            s_scr[c0:c0 + ck, :] = s_blk
            bm = jnp.max(s_blk.reshape(ck // SUBLANES, SUBLANES, nq), axis=0)
            m8 = bm if m8 is None else jnp.maximum(m8, bm)
        m_scr[...] = jnp.max(m8, axis=0, keepdims=True)
        if with_pass2:
            o = (acc[:HEAD_DIM] / acc[HEAD_DIM:HEAD_DIM + 1]).T
            o_ref[0] = jnp.concatenate([o[h * tq:(h + 1) * tq] for h in range(GQA_REP)], axis=1).astype(BF16)

    pl.when(i == 0)(lambda: run(False))
    pl.when(i > 0)(lambda: run(True))


ATTN_KV_CHUNK = 256


def _attention(q, kv_sources):
    B, _, T, _ = q.shape
    tq = min(T, 256)
    n_tiles = T // tq
    n_all = B * N_KV_HEADS * n_tiles
    total = sum(k.shape[2] for k, _ in kv_sources)

    def tile(t):
        t = jnp.clip(t, 0, n_all - 1)
        return t // (N_KV_HEADS * n_tiles), (t // n_tiles) % N_KV_HEADS, t % n_tiles

    def q_map(t):
        b, g, i = tile(t)
        return b, g, i, 0

    def k_map(t):
        b, g, _ = tile(t)
        return b, g, 0, 0

    def v_map(t):
        b, g, _ = tile(t - 1)
        return b, g, 0, 0

    def o_map(t):
        b, g, i = tile(t - 1)
        return b, i, g

    in_specs = [pl.BlockSpec((1, GQA_REP, tq, HEAD_DIM), q_map)]
    args = [q]
    for k, vt in kv_sources:
        in_specs += [pl.BlockSpec((1, 1, k.shape[2], HEAD_DIM), k_map),
                     pl.BlockSpec((1, 1, VT_ROWS, k.shape[2]), v_map)]
        args += [k, vt]
    return pl.pallas_call(
        functools.partial(_attn_kernel, n_src=len(kv_sources), tq=tq),
        grid=(n_all + 1,),
        in_specs=in_specs,
        out_specs=pl.BlockSpec((1, tq, GQA_REP * HEAD_DIM), o_map),
        out_shape=jax.ShapeDtypeStruct((B, T, Q_WIDTH), BF16),
        scratch_shapes=[pltpu.VMEM((total, GQA_REP * tq), F32), pltpu.VMEM((1, GQA_REP * tq), F32)],
        compiler_params=_params("arbitrary"),
    )(*args)


def _proj_branch(y_ref, w_ref):
    return jnp.dot(y_ref[0], w_ref[...], preferred_element_type=F32)


def _mixer_mlp_kernel(*refs, branch, ff_chunk):
    x_ref, mod_ref, g_ref, win_ref, wout_ref = refs[:5]
    o_ref = refs[-1]
    x = x_ref[0]
    mod = mod_ref[0]
    if branch is not None:
        x = x + mod[2:3] * branch(*refs[5:-1])
    h = _norm_mod(x, g_ref[...], mod, 3).astype(BF16)
    acc = jnp.zeros(x.shape, F32)
    for f0 in range(0, D_FF, ff_chunk):
        u = jnp.maximum(jnp.dot(h, win_ref[:, f0:f0 + ff_chunk], preferred_element_type=F32), 0.0)
        acc = acc + jnp.dot((u * u).astype(BF16), wout_ref[f0:f0 + ff_chunk, :], preferred_element_type=F32)
    o_ref[0] = x + mod[5:6] * acc


def _mixer_mlp(x, mod, g, win_bf, wout_bf, branch=None, branch_args=(), branch_specs=(), tm_max=512):
    B, T, _ = x.shape
    tm = min(T, tm_max)
    specs = [s(tm) if callable(s) else s for s in branch_specs]
    return pl.pallas_call(
        functools.partial(_mixer_mlp_kernel, branch=branch, ff_chunk=1024),
        grid=(B, T // tm),
        in_specs=[_row_spec(tm), _mod_spec(), _const_spec((1, D_MODEL)),
                  _const_spec((D_MODEL, D_FF)), _const_spec((D_FF, D_MODEL))] + specs,
        out_specs=_row_spec(tm),
        out_shape=jax.ShapeDtypeStruct(x.shape, F32),
        compiler_params=_params("parallel", "parallel"),
    )(x, mod, g, win_bf, wout_bf, *branch_args)


DECAY_SCALE = float(np.exp(-0.5))


def _rwkv_prep_kernel(x_ref, xp_ref, xn_ref, g_ref, mod_ref, mu_ref, wrkv_ref, w1_ref, w2_ref, w0_ref, a1_ref, a2_ref,
                      a0_ref, g1_ref, g2_ref, kk_ref, kac_ref, kah_ref,
                      r_o, v_o, kk_o, g_o, lw0_o, lw1_o, key0_o, key1_o, a0_o, a1_o, *, tm):
    i = pl.program_id(1)
    last = pl.num_programs(1) - 1
    g = g_ref[...]
    mod = mod_ref[0]
    h = _norm_mod(x_ref[0], g, mod, 0)
    hp = _norm_mod(xp_ref[0][SUBLANES - 1:SUBLANES], g, mod, 0) * jnp.where(i == 0, 0.0, 1.0)
    hn = _norm_mod(xn_ref[0][0:1], g, mod, 0) * jnp.where(i == last, 0.0, 1.0)
    row = lax.broadcasted_iota(jnp.int32, (tm, 1), 0)
    prev = jnp.where(row == 0, hp, pltpu.roll(h, 1, 0))
    nxt = jnp.where(row == tm - 1, hn, pltpu.roll(h, tm - 1, 0))
    xx = 0.5 * (prev + nxt) - h
    mix = lambda m: (h + xx * mu_ref[m:m + 1]).astype(BF16)
    dot = functools.partial(jnp.dot, preferred_element_type=F32)
    r = dot(mix(0), wrkv_ref[0])
    k = dot(mix(2), wrkv_ref[1])
    v = dot(mix(3), wrkv_ref[2])
    r_o[0] = r.astype(BF16)
    v_o[0] = v.astype(BF16)
    g_o[0] = dot(_sigmoid(dot(mix(5), g1_ref[...])).astype(BF16), g2_ref[...]).astype(BF16)
    kk = k * kk_ref[...]
    kk_o[0] = (kk * lax.rsqrt(jnp.maximum(_seg_sum_wide(kk * kk), 1e-24))).astype(BF16)
    tw = jnp.tanh(dot(mix(1), w1_ref[...])).astype(BF16)
    ta = dot(mix(4), a1_ref[...]).astype(BF16)
    for d, (lw_o, key_o, a_o) in enumerate(((lw0_o, key0_o, a0_o), (lw1_o, key1_o, a1_o))):
        th = jnp.tanh(w0_ref[d:d + 1] + dot(tw[:, d * DECAY_LORA:(d + 1) * DECAY_LORA], w2_ref[d]))
        lw_o[0] = (-0.5 * DECAY_SCALE) * th + (-0.5 * DECAY_SCALE)
        ah = jnp.tanh(a0_ref[d:d + 1] + dot(ta[:, d * ICLR_LORA:(d + 1) * ICLR_LORA], a2_ref[d]))
        a_o[0] = (0.5 * ah + 0.5).astype(BF16)
        key_o[0] = (k * (kac_ref[...] + kah_ref[...] * ah)).astype(BF16)


def _rwkv_prepare(x, g, mod, p):
    B, T, _ = x.shape
    tm = min(T, 256)
    nb = tm // SUBLANES
    n_blk8 = T // SUBLANES
    prev_spec = pl.BlockSpec((1, SUBLANES, D_MODEL), lambda b, i: (b, jnp.maximum(i * nb - 1, 0), 0))
    next_spec = pl.BlockSpec((1, SUBLANES, D_MODEL), lambda b, i: (b, jnp.minimum((i + 1) * nb, n_blk8 - 1), 0))
    o_bf, o_f32 = jax.ShapeDtypeStruct(x.shape, BF16), jax.ShapeDtypeStruct(x.shape, F32)
    return pl.pallas_call(
        functools.partial(_rwkv_prep_kernel, tm=tm),
        grid=(B, T // tm),
        in_specs=[_row_spec(tm), prev_spec, next_spec, _const_spec((1, D_MODEL)), _mod_spec(),
                  _const_spec((6, D_MODEL)), _const_spec((3, D_MODEL, D_MODEL)),
                  _const_spec((D_MODEL, 2 * DECAY_LORA)), _const_spec((2, DECAY_LORA, D_MODEL)),
                  _const_spec((2, D_MODEL)),
                  _const_spec((D_MODEL, 2 * ICLR_LORA)), _const_spec((2, ICLR_LORA, D_MODEL)),
                  _const_spec((2, D_MODEL)),
                  _const_spec((D_MODEL, GATE_LORA)), _const_spec((GATE_LORA, D_MODEL)),
                  _const_spec((1, D_MODEL)), _const_spec((1, D_MODEL)), _const_spec((1, D_MODEL))],
        out_specs=[_row_spec(tm)] * 10,
        out_shape=[o_bf, o_bf, o_bf, o_bf, o_f32, o_f32, o_bf, o_bf, o_bf, o_bf],
        compiler_params=_params("parallel", "parallel"),
    )(x, x, x, g, mod, p["mu"], p["w_rkv"], p["w1"], p["w2"], p["w0"], p["a1"], p["a2"], p["a0"], p["g1"], p["g2"],
      p["k_k"], p["k_ac"], p["k_ah"])


def _wkv_kernel(r_ref, lw_ref, k_ref, v_ref, kk_ref, a_ref, s0_ref, y_ref, s_ref, state, *, n_chunks, n_pairs,
                reverse):
    C = CHUNK

    def mm(dims):
        return lambda a, b: lax.dot_general(a.astype(BF16), b.astype(BF16), dims, preferred_element_type=F32)

    hp = mm((((1,), (0,)), ((), ())))
    nt = mm((((1,), (1,)), ((), ())))
    tn = mm((((0,), (0,)), ((), ())))

    @pl.when(pl.program_id(2) == 0)
    def _():
        state[...] = s0_ref[0]

    pi = lax.broadcasted_iota(jnp.int32, (C, LANES), 0)
    pj = lax.broadcasted_iota(jnp.int32, (C, LANES), 1) % C
    strict = (pi < pj) if reverse else (pi > pj)
    incl = (pi <= pj) if reverse else (pi >= pj)
    blk = (pi // INV_BLOCK) == (pj // INV_BLOCK)
    eye = pi == pj
    ident = jnp.where(eye, 1.0, 0.0)
    head0 = lax.broadcasted_iota(jnp.int32, (1, LANES), 1) < RWKV_HEAD

    def stack(z):
        zb = z.astype(BF16)
        zero = jnp.zeros_like(zb)
        return jnp.concatenate([jnp.where(head0, zb, zero), jnp.where(head0, zero, zb)], axis=0)

    side = lambda x, y: jnp.concatenate([x, y], axis=1)
    diag_blocks = lambda z: jnp.where(head0, z[:C], z[C:])

    each = lambda f, *ls: [f(*xs) for xs in zip(*ls)]
    order = range(n_chunks - 1, -1, -1) if reverse else range(n_chunks)

    def cumsum(x):
        step = 1
        while step < C:
            if reverse:
                x = x + jnp.where(pi < C - step, pltpu.roll(x, C - step, 0), 0.0)
            else:
                x = x + jnp.where(pi >= step, pltpu.roll(x, step, 0), 0.0)
            step *= 2
        return x

    def chain(pairs):
        units = [(slice(c * C, (c + 1) * C), slice(p * LANES, (p + 1) * LANES)) for p in pairs for c in range(n_chunks)]
        load = lambda ref: [ref[0, rows, cols].astype(F32) for rows, cols in units]
        r, lw, k, v, kk, a = (load(ref) for ref in (r_ref, lw_ref, k_ref, v_ref, kk_ref, a_ref))
        cum = each(cumsum, lw)
        total = each(lambda z: z[0:1] if reverse else z[C - 1:C], cum)
        g_in = each(lambda z: jnp.exp(-z), cum)
        g_to_end = each(lambda t, z: jnp.exp(t - z), total, cum)
        b = each(lambda x, y: x * y, kk, a)
        a_u = each(lambda x, z, l: x * jnp.exp(z - l), kk, cum, lw)
        r_u = each(lambda x, z: x * jnp.exp(z), r, cum)
        b_s = each(lambda x, y: stack(x * y), b, g_in)
        k_s = each(lambda x, y: stack(x * y), k, g_in)
        v_s = each(stack, v)
        bh = each(lambda x, y: x * y, b, g_to_end)
        kh = each(lambda x, y: x * y, k, g_to_end)
        big = each(lambda a_, r_, b_, k_: nt(jnp.concatenate([a_, r_], axis=0), jnp.concatenate([b_, k_], axis=0)),
                   a_u, r_u, b_s, k_s)
        l_ab = each(lambda z: jnp.where(strict, z[:C, :LANES], 0.0), big)
        l_ak = each(lambda z: jnp.where(strict, z[:C, LANES:], 0.0), big)
        p_rb = each(lambda z: jnp.where(incl, z[C:, :LANES], 0.0), big)
        p_rk = each(lambda z: jnp.where(incl, z[C:, LANES:], 0.0), big)
        l_d = each(lambda z: jnp.where(blk, z, 0.0), l_ab)
        l_o = each(lambda x, y: x - y, l_ab, l_d)
        l2 = each(lambda d: hp(d, stack(d)), l_d)
        lpv = each(lambda x, y, vs: hp(jnp.concatenate([x, y], axis=0), vs), l_ak, p_rk, v_s)
        lv = each(lambda z: z[:C], lpv)
        prv = each(lambda z: z[C:], lpv)
        khv = each(tn, kh, v)
        l34 = each(lambda d, s2: hp(jnp.concatenate([d, s2], axis=0), stack(s2)), l_d, l2)
        p1 = each(lambda d, s2, s34: ident - d + s2 - s34[:C], l_d, l2, l34)
        x48 = each(lambda p, s34: hp(jnp.concatenate([p, s34[C:]], axis=0), stack(s34[C:])), p1, l34)
        p2 = each(lambda p, x: p + x[:C], p1, x48)
        t_d = each(lambda p, x: p + hp(p, stack(x[C:])), p2, x48)
        m1 = each(lambda t, o: hp(t, stack(o)), t_d, l_o)
        mm_ = each(lambda x, t: hp(x, side(stack(x), stack(t))), m1, t_d)
        y_d = each(lambda t, x: t - x[:, LANES:], t_d, mm_)
        t_inv = each(lambda y, x: y + hp(x[:, :LANES], stack(y)), y_d, mm_)
        w = each(lambda t, x, y: hp(t, side(stack(x), stack(y))), t_inv, a_u, lv)
        pw = each(lambda p, x: hp(p, side(stack(x[:, :LANES]), stack(x[:, LANES:]))), p_rb, w)
        bw = each(tn, bh, w)
        q1 = each(lambda x, y: x - y[:, :LANES], r_u, pw)
        y2 = each(lambda x, y: x - y[:, LANES:], prv, pw)
        g_mat = each(lambda t, y: jnp.where(eye, jnp.exp(t), 0.0) - diag_blocks(y[:, :LANES]), total, bw)
        h_mat = each(lambda x, y: diag_blocks(x) - diag_blocks(y[:, LANES:]), khv, bw)
        s = [state[p] for p in pairs]
        for c in order:
            for i, p in enumerate(pairs):
                u = i * n_chunks + c
                ys = hp(jnp.concatenate([q1[u], g_mat[u]], axis=0), stack(s[i]))
                y_ref[0, c * C:(c + 1) * C, p * LANES:(p + 1) * LANES] = ys[:C] + y2[u]
                s[i] = ys[C:] + h_mat[u]
        for i, p in enumerate(pairs):
            state[p] = s[i]
        return s

    final_state = chain(list(range(n_pairs)))

    @pl.when(pl.program_id(2) == pl.num_programs(2) - 1)
    def _():
        for p in range(n_pairs):
            s_ref[0, p] = final_state[p]


WKV_PAIRS = 4
WKV_CHUNKS = 8


def _wkv(r, lw, k, v, kk, a, s0, reverse):
    B, T, _ = r.shape
    rows = min(T, WKV_CHUNKS * CHUNK)
    n_steps = T // rows
    width = WKV_PAIRS * LANES
    tmap = (lambda b, p, j: (b, n_steps - 1 - j, p)) if reverse else (lambda b, p, j: (b, j, p))
    seq_spec = pl.BlockSpec((1, rows, width), tmap)
    st_spec = pl.BlockSpec((1, WKV_PAIRS, RWKV_HEAD, LANES), lambda b, p, j: (b, p, 0, 0))
    return pl.pallas_call(
        functools.partial(_wkv_kernel, n_chunks=rows // CHUNK, n_pairs=WKV_PAIRS, reverse=reverse),
        grid=(B, PAIRS // WKV_PAIRS, n_steps),
        in_specs=[seq_spec] * 6 + [st_spec],
        out_specs=[seq_spec, st_spec],
        out_shape=[jax.ShapeDtypeStruct(r.shape, F32), jax.ShapeDtypeStruct((B, PAIRS, RWKV_HEAD, LANES), F32)],
        scratch_shapes=[pltpu.VMEM((WKV_PAIRS, RWKV_HEAD, LANES), F32)],
        compiler_params=_params("parallel", "parallel", "arbitrary"),
    )(r, lw, k, v, kk, a, s0)


def _rwkv_branch(y0_ref, y1_ref, r_ref, v_ref, k0_ref, k1_ref, g_ref, rk_ref, lng_ref, lnb_ref, wo_ref):
    y = y0_ref[0] + y1_ref[0]
    inv_n = 1.0 / RWKV_HEAD
    mean = _seg_sum_wide(y) * inv_n
    yc = y - mean
    var = _seg_sum_wide(yc * yc) * inv_n
    yn = yc * lax.rsqrt(var + GN_EPS) * lng_ref[...] + lnb_ref[...]
    f32 = lambda ref: ref[0].astype(F32)
    bonus = _seg_sum_wide(f32(r_ref) * (f32(k0_ref) + f32(k1_ref)) * rk_ref[...]) * f32(v_ref)
    return jnp.dot(((yn + bonus) * f32(g_ref)).astype(BF16), wo_ref[...], preferred_element_type=F32)


def _rwkv_branch_specs():
    vec = _const_spec((1, D_MODEL))
    return [_row_spec] * 7 + [vec, vec, vec, _const_spec((D_MODEL, D_MODEL))]


HALO = SUBLANES


def _pool_branch(x_ref, xp_ref, xn_ref, g_ref, mod_ref, w_ref, sc_ref, *, seq_len):
    tm = x_ref.shape[1]
    i = pl.program_id(1)
    last = pl.num_programs(1) - 1
    g = g_ref[...]
    mod = mod_ref[0]
    h = _norm_mod(x_ref[0], g, mod, 0)
    hp = _norm_mod(xp_ref[0], g, mod, 0) * jnp.where(i == 0, 0.0, 1.0)
    hn = _norm_mod(xn_ref[0], g, mod, 0) * jnp.where(i == last, 0.0, 1.0)
    ext = jnp.concatenate([hp, h, hn], axis=0)
    n_ext = tm + 2 * HALO
    t = i * tm + lax.broadcasted_iota(jnp.int32, (tm, 1), 0)
    outs = []
    for gi, win in enumerate(POOL_WINDOWS):
        e = ext[:, gi * POOL_GROUP:(gi + 1) * POOL_GROUP]
        acc = e + pltpu.roll(e, 1, 0)
        step = 1
        while 2 * step < win:
            acc = pltpu.roll(acc, step, 0) + pltpu.roll(acc, n_ext - step, 0)
            step *= 2
        half = win // 2
        cnt = (jnp.minimum(t + half, seq_len) - jnp.maximum(t - half, 0)).astype(F32)
        pooled = acc[HALO:HALO + tm] / cnt - e[HALO:HALO + tm]
        outs.append(jnp.dot(pooled.astype(BF16), w_ref[gi], preferred_element_type=F32))
    return jnp.concatenate(outs, axis=1) * sc_ref[...]


def _pool_branch_specs(seq_len):
    n_blk = seq_len // HALO
    prev_spec = lambda tm: pl.BlockSpec((1, HALO, D_MODEL), lambda b, i: (b, jnp.maximum(i * (tm // HALO) - 1, 0), 0))
    next_spec = lambda tm: pl.BlockSpec(
        (1, HALO, D_MODEL), lambda b, i: (b, jnp.minimum((i + 1) * (tm // HALO), n_blk - 1), 0))
    return [_row_spec, prev_spec, next_spec, _const_spec((1, D_MODEL)), lambda tm: _mod_spec(),
            _const_spec((len(POOL_WINDOWS), POOL_GROUP, POOL_GROUP)), _const_spec((1, D_MODEL))]


def _rope_tables(n_tokens):
    rows = n_tokens // GRID_W
    n_freq = HEAD_DIM // 4
    inv = ROPE_THETA ** (-jnp.arange(n_freq, dtype=F32) / n_freq)
    ang_r = jnp.arange(rows, dtype=F32)[:, None] * inv
    ang_c = jnp.arange(GRID_W, dtype=F32)[:, None] * inv
    ang = jnp.concatenate([
        jnp.broadcast_to(ang_r[:, None, :], (rows, GRID_W, n_freq)),
        jnp.broadcast_to(ang_c[None, :, :], (rows, GRID_W, n_freq))], axis=-1).reshape(rows * GRID_W, 2 * n_freq)
    cos, sin = jnp.cos(ang), jnp.sin(ang)
    return jnp.tile(cos, (1, 4)), jnp.tile(sin, (1, 4))


def kernel(x, c, ctx, c_ctx, w_mod, b_mod, norm1_g, norm2_g, mlp_w_in, mlp_w_out, attn_w_qkv, attn_q_gain, attn_k_gain, attn_w_o, rwkv_mu, rwkv_w_rkv, rwkv_w0, rwkv_w1, rwkv_w2, rwkv_a0, rwkv_a1, rwkv_a2, rwkv_g1, rwkv_g2, rwkv_k_k, rwkv_k_a, rwkv_r_k, rwkv_ln_g, rwkv_ln_b, rwkv_w_o, pool_w, pool_scale):
    B, S, _ = x.shape
    L = ctx.shape[1]
    depth = w_mod.shape[0]
    assert x.shape[2] == D_MODEL and S % (4 * CHUNK) == 0 and L % CHUNK == 0 and S % GRID_W == 0

    n_rows = -(-(B + 1) // SUBLANES) * SUBLANES
    cvec = jnp.concatenate([c, c_ctx[None], jnp.zeros((n_rows - B - 1, D_MODEL), F32)], axis=0)
    mod_all = _modulation(cvec, w_mod, b_mod).reshape(depth, n_rows, N_MOD, D_MODEL)
    cos_t, sin_t = _rope_tables(S)
    zero_tab = jnp.zeros((L, LANES), F32)
    row = lambda a: a.reshape(1, -1)

    for i in range(depth):
        last = i == depth - 1
        j = i // N_MIXERS
        mod_l = mod_all[i, :B]
        mod_c = jnp.broadcast_to(mod_all[i, B][None], (B, N_MOD, D_MODEL))
        g1 = row(norm1_g[i])
        kind = i % N_MIXERS
        if kind == 0:
            w_qkv = attn_w_qkv[j].astype(BF16)
            w_o = attn_w_o[j].astype(BF16)
            qg = jnp.tile(row(attn_q_gain[j]), (1, 2))
            kg = jnp.tile(row(attn_k_gain[j]), (1, 2))
            q_l, k_l, v_l = _qkv_project(x, g1, mod_l, w_qkv, qg, kg, cos_t, sin_t, True)
            q_c, k_c, v_c = _qkv_project(ctx, g1, mod_c, w_qkv, qg, kg, zero_tab, zero_tab, False)
            proj_specs = [_row_spec, _const_spec((D_MODEL, D_MODEL))]
            mix_l = (_proj_branch, (_attention(q_l, [(k_l, v_l), (k_c, v_c)]), w_o), proj_specs, 512)
            if not last:
                mix_c = (_proj_branch, (_attention(q_c, [(k_c, v_c)]), w_o), proj_specs, 512)
        elif kind == 1:
            p = {
                "mu": rwkv_mu[j], "w_rkv": rwkv_w_rkv[j].astype(BF16),
                "w1": jnp.concatenate([rwkv_w1[j, 0], rwkv_w1[j, 1]], axis=1).astype(BF16),
                "w2": (0.5 * rwkv_w2[j]).astype(BF16), "w0": 0.5 * rwkv_w0[j],
                "a1": jnp.concatenate([rwkv_a1[j, 0], rwkv_a1[j, 1]], axis=1).astype(BF16),
                "a2": (0.5 * rwkv_a2[j]).astype(BF16), "a0": 0.5 * rwkv_a0[j],
                "g1": rwkv_g1[j].astype(BF16), "g2": rwkv_g2[j].astype(BF16),
                "k_k": row(rwkv_k_k[j]), "k_ac": row(1.0 - 0.5 * rwkv_k_a[j]), "k_ah": row(0.5 * rwkv_k_a[j]), "r_k": row(rwkv_r_k[j]),
                "ln_g": row(rwkv_ln_g[j]), "ln_b": row(rwkv_ln_b[j]), "w_o": rwkv_w_o[j].astype(BF16),
            }
            r_l, v_l, kk_l, g_l, lw0_l, lw1_l, key0_l, key1_l, a0_l, a1_l = _rwkv_prepare(x, g1, mod_l, p)
            r_c, v_c, kk_c, g_c, lw0_c, lw1_c, key0_c, key1_c, a0_c, a1_c = _rwkv_prepare(ctx, g1, mod_c, p)
            zero_state = jnp.zeros((B, PAIRS, RWKV_HEAD, LANES), F32)
            ys_l, ys_c = [], []
            for rev, (lw_l, key_l, a_l, lw_c, key_c, a_c) in enumerate(
                    ((lw0_l, key0_l, a0_l, lw0_c, key0_c, a0_c), (lw1_l, key1_l, a1_l, lw1_c, key1_c, a1_c))):
                y_c, s_ctx = _wkv(r_c, lw_c, key_c, v_c, kk_c, a_c, zero_state, bool(rev))
                y_l, _ = _wkv(r_l, lw_l, key_l, v_l, kk_l, a_l, s_ctx, bool(rev))
                ys_l.append(y_l)
                ys_c.append(y_c)
            consts = (p["r_k"], p["ln_g"], p["ln_b"], p["w_o"])
            mix_l = (_rwkv_branch, (ys_l[0], ys_l[1], r_l, v_l, key0_l, key1_l, g_l) + consts, _rwkv_branch_specs(), 256)
            if not last:
                mix_c = (_rwkv_branch, (ys_c[0], ys_c[1], r_c, v_c, key0_c, key1_c, g_c) + consts,
                         _rwkv_branch_specs(), 256)
        else:
            w_p = pool_w[j].astype(BF16)
            sc = row(pool_scale[j])
            mix_l = (functools.partial(_pool_branch, seq_len=S), (x, x, x, g1, mod_l, w_p, sc), _pool_branch_specs(S), 256)
            if not last:
                mix_c = (functools.partial(_pool_branch, seq_len=L), (ctx, ctx, ctx, g1, mod_c, w_p, sc),
                         _pool_branch_specs(L), 256)
        g2 = row(norm2_g[i])
        w_in = mlp_w_in[i].astype(BF16)
        w_out = mlp_w_out[i].astype(BF16)
        x = _mixer_mlp(x, mod_l, g2, w_in, w_out, *mix_l)
        if not last:
            ctx = _mixer_mlp(ctx, mod_c, g2, w_in, w_out, *mix_c)
    return x
```

```python
import functools

import jax
import jax.numpy as jnp
import numpy as np
from jax import lax
from jax.experimental import pallas as pl
from jax.experimental.pallas import tpu as pltpu

F32 = jnp.float32
BF16 = jnp.bfloat16

D_MODEL = 1024
GRID_W = 64
N_MIXERS = 3
N_MOD = 6
EPS = 1e-6
N_HEADS = 16
N_KV_HEADS = 4
HEAD_DIM = 64
GQA_REP = N_HEADS // N_KV_HEADS
Q_WIDTH = N_HEADS * HEAD_DIM
KV_WIDTH = N_KV_HEADS * HEAD_DIM
QKV_WIDTH = Q_WIDTH + 2 * KV_WIDTH
ROPE_THETA = 10000.0
RWKV_HEAD = 64
DECAY_LORA = 64
ICLR_LORA = 64
GATE_LORA = 160
GN_EPS = RWKV_HEAD * 1e-5
POOL_WINDOWS = (2, 4, 8, 16)
POOL_GROUP = D_MODEL // len(POOL_WINDOWS)
D_FF = 4 * D_MODEL

LANES = 128
SUBLANES = 8
PAIRS = D_MODEL // LANES
CHUNK = 64
INV_BLOCK = 16
VMEM_LIMIT = 56 * 1024 * 1024

HIGHEST = lax.Precision.HIGHEST


def _params(*sem):
    return pltpu.CompilerParams(dimension_semantics=sem, vmem_limit_bytes=VMEM_LIMIT)


def _const_spec(shape):
    zeros = (0,) * len(shape)
    return pl.BlockSpec(shape, lambda *_: zeros, pipeline_mode=pl.Buffered(1))


def _row_spec(tm, width=D_MODEL):
    return pl.BlockSpec((1, tm, width), lambda b, i: (b, i, 0))


def _mod_spec():
    return pl.BlockSpec((1, N_MOD, D_MODEL), lambda b, i: (b, 0, 0))


def _norm_mod(x, g, mod, k):
    y = x * lax.rsqrt(jnp.mean(x * x, axis=-1, keepdims=True) + EPS) * g
    return y * (1.0 + mod[k + 1:k + 2]) + mod[k:k + 1]


def _seg_sum(x):
    i = lax.broadcasted_iota(jnp.int32, (LANES, LANES), 0) // HEAD_DIM
    j = lax.broadcasted_iota(jnp.int32, (LANES, LANES), 1) // HEAD_DIM
    return jnp.dot(x.astype(BF16), jnp.where(i == j, 1.0, 0.0).astype(BF16), preferred_element_type=F32)


def _sigmoid(z):
    return 0.5 * jnp.tanh(0.5 * z) + 0.5


def _seg_sum_wide(x):
    return jnp.concatenate([_seg_sum(x[:, p * LANES:(p + 1) * LANES]) for p in range(x.shape[1] // LANES)], axis=1)


def _mod_kernel(c_ref, w_ref, b_ref, o_ref):
    c = c_ref[...]
    s = c * jax.nn.sigmoid(c)
    o_ref[0] = jnp.dot(s, w_ref[0], precision=HIGHEST, preferred_element_type=F32) + b_ref[0]


def _modulation(cvec, w_mod, b_mod):
    depth = w_mod.shape[0]
    rows = cvec.shape[0]
    tn = 1536
    return pl.pallas_call(
        _mod_kernel,
        grid=(depth, N_MOD * D_MODEL // tn),
        in_specs=[pl.BlockSpec((rows, D_MODEL), lambda l, j: (0, 0)),
                  pl.BlockSpec((1, D_MODEL, tn), lambda l, j: (l, 0, j)),
                  pl.BlockSpec((1, 1, tn), lambda l, j: (l, 0, j))],
        out_specs=pl.BlockSpec((1, rows, tn), lambda l, j: (l, 0, j)),
        out_shape=jax.ShapeDtypeStruct((depth, rows, N_MOD * D_MODEL), F32),
        compiler_params=_params("parallel", "parallel"),
    )(cvec, w_mod, b_mod.reshape(depth, 1, N_MOD * D_MODEL))


VT_ROWS = HEAD_DIM + 16
Q_SCALE = float(HEAD_DIM ** -0.5 * np.log2(np.e))


def _qkv_kernel(x_ref, g_ref, mod_ref, w_ref, qg_ref, kg_ref, cos_ref, sin_ref, q_ref, k_ref, v_ref, *, use_rope):
    h = _norm_mod(x_ref[0], g_ref[...], mod_ref[0], 0).astype(BF16)
    acc = jnp.dot(h, w_ref[...], preferred_element_type=F32)
    src_lane = lax.broadcasted_iota(jnp.int32, (2 * LANES, LANES), 0) % LANES
    dst_lane = lax.broadcasted_iota(jnp.int32, (2 * LANES, LANES), 1)
    half = HEAD_DIM // 2
    first_half = (dst_lane % HEAD_DIM) < half
    rot_mat = jnp.where(first_half & (src_lane == dst_lane + half), -1.0,
                        jnp.where(jnp.logical_not(first_half) & (src_lane == dst_lane - half), 1.0, 0.0)).astype(BF16)

    def head_pair(xp, gain, scale):
        y = xp * lax.rsqrt(_seg_sum(xp * xp) * (1.0 / HEAD_DIM) + EPS) * gain
        if use_rope:
            y_hi = y.astype(BF16)
            y_lo = (y - y_hi.astype(F32)).astype(BF16)
            rot = jnp.dot(jnp.concatenate([y_hi, y_lo], axis=1), rot_mat, preferred_element_type=F32)
            y = y * cos_ref[...] + rot * sin_ref[...]
        return (y * scale).astype(BF16)

    for p in range(Q_WIDTH // LANES):
        y = head_pair(acc[:, p * LANES:(p + 1) * LANES], qg_ref[...], Q_SCALE)
        q_ref[0, 2 * p] = y[:, :HEAD_DIM]
        q_ref[0, 2 * p + 1] = y[:, HEAD_DIM:]
    for p in range(KV_WIDTH // LANES):
        y = head_pair(acc[:, Q_WIDTH + p * LANES:Q_WIDTH + (p + 1) * LANES], kg_ref[...], 1.0)
        k_ref[0, 2 * p] = y[:, :HEAD_DIM]
        k_ref[0, 2 * p + 1] = y[:, HEAD_DIM:]
    ones = jnp.ones((VT_ROWS - HEAD_DIM, acc.shape[0]), F32)
    for p in range(KV_WIDTH // LANES):
        c0 = Q_WIDTH + KV_WIDTH + p * LANES
        vt = acc[:, c0:c0 + LANES].T
        v_ref[0, 2 * p] = jnp.concatenate([vt[:HEAD_DIM], ones], axis=0).astype(BF16)
        v_ref[0, 2 * p + 1] = jnp.concatenate([vt[HEAD_DIM:], ones], axis=0).astype(BF16)


def _qkv_project(x, g, mod, w_bf, qg, kg, cos_t, sin_t, use_rope):
    B, T, _ = x.shape
    tm = min(T, 1024)
    tab_spec = pl.BlockSpec((tm, LANES), lambda b, i: (i, 0))
    head_spec = lambda n: pl.BlockSpec((1, n, tm, HEAD_DIM), lambda b, i: (b, 0, i, 0))
    return pl.pallas_call(
        functools.partial(_qkv_kernel, use_rope=use_rope),
        grid=(B, T // tm),
        in_specs=[_row_spec(tm), _const_spec((1, D_MODEL)), _mod_spec(), _const_spec((D_MODEL, QKV_WIDTH)),
                  _const_spec((1, LANES)), _const_spec((1, LANES)), tab_spec, tab_spec],
        out_specs=[head_spec(N_HEADS), head_spec(N_KV_HEADS),
                   pl.BlockSpec((1, N_KV_HEADS, VT_ROWS, tm), lambda b, i: (b, 0, 0, i))],
        out_shape=[jax.ShapeDtypeStruct((B, N_HEADS, T, HEAD_DIM), BF16),
                   jax.ShapeDtypeStruct((B, N_KV_HEADS, T, HEAD_DIM), BF16),
                   jax.ShapeDtypeStruct((B, N_KV_HEADS, VT_ROWS, T), BF16)],
        compiler_params=_params("parallel", "parallel"),
    )(x, g, mod, w_bf, qg, kg, cos_t, sin_t)


def _attn_kernel(*refs, n_src, tq):
    q_ref = refs[0]
    kv_refs = refs[1:1 + 2 * n_src]
    o_ref = refs[1 + 2 * n_src]
    s_scr, m_scr = refs[2 + 2 * n_src:]
    nq = GQA_REP * tq
    i = pl.program_id(0)
    pieces = []
    row = 0
    for s in range(n_src):
        keys_total = kv_refs[2 * s].shape[2]
        ck = min(ATTN_KV_CHUNK, keys_total)
        for r0 in range(0, keys_total, ck):
            pieces.append((kv_refs[2 * s], kv_refs[2 * s + 1], r0, ck, row))
            row += ck

    def run(with_pass2):
        q = q_ref[0].reshape(nq, HEAD_DIM)
        m8 = None
        if with_pass2:
            m_prev = m_scr[...]
            acc = jnp.zeros((VT_ROWS, nq), F32)
        for k_ref, vt_ref, r0, ck, c0 in pieces:
            if with_pass2:
                p = jnp.exp2(s_scr[c0:c0 + ck, :] - m_prev).astype(BF16)
                acc = acc + jnp.dot(vt_ref[0, 0, :, r0:r0 + ck], p, preferred_element_type=F32)
            s_blk = lax.dot_general(k_ref[0, 0, r0:r0 + ck, :], q, (((1,), (1,)), ((), ())),
                                    preferred_element_type=F32)
            s_scr[c0:c0 + ck, :] = s_blk
            bm = jnp.max(s_blk.reshape(ck // SUBLANES, SUBLANES, nq), axis=0)
            m8 = bm if m8 is None else jnp.maximum(m8, bm)
        m_scr[...] = jnp.max(m8, axis=0, keepdims=True)
        if with_pass2:
            o = (acc[:HEAD_DIM] / acc[HEAD_DIM:HEAD_DIM + 1]).T
            o_ref[0] = jnp.concatenate([o[h * tq:(h + 1) * tq] for h in range(GQA_REP)], axis=1).astype(BF16)

    pl.when(i == 0)(lambda: run(False))
    pl.when(i > 0)(lambda: run(True))


ATTN_KV_CHUNK = 256


def _attention(q, kv_sources):
    B, _, T, _ = q.shape
    tq = min(T, 256)
    n_tiles = T // tq
    n_all = B * N_KV_HEADS * n_tiles
    total = sum(k.shape[2] for k, _ in kv_sources)

    def tile(t):
        t = jnp.clip(t, 0, n_all - 1)
        return t // (N_KV_HEADS * n_tiles), (t // n_tiles) % N_KV_HEADS, t % n_tiles

    def q_map(t):
        b, g, i = tile(t)
        return b, g, i, 0

    def k_map(t):
        b, g, _ = tile(t)
        return b, g, 0, 0

    def v_map(t):
        b, g, _ = tile(t - 1)
        return b, g, 0, 0

    def o_map(t):
        b, g, i = tile(t - 1)
        return b, i, g

    in_specs = [pl.BlockSpec((1, GQA_REP, tq, HEAD_DIM), q_map)]
    args = [q]
    for k, vt in kv_sources:
        in_specs += [pl.BlockSpec((1, 1, k.shape[2], HEAD_DIM), k_map),
                     pl.BlockSpec((1, 1, VT_ROWS, k.shape[2]), v_map)]
        args += [k, vt]
    return pl.pallas_call(
        functools.partial(_attn_kernel, n_src=len(kv_sources), tq=tq),
        grid=(n_all + 1,),
        in_specs=in_specs,
        out_specs=pl.BlockSpec((1, tq, GQA_REP * HEAD_DIM), o_map),
        out_shape=jax.ShapeDtypeStruct((B, T, Q_WIDTH), BF16),
        scratch_shapes=[pltpu.VMEM((total, GQA_REP * tq), F32), pltpu.VMEM((1, GQA_REP * tq), F32)],
        compiler_params=_params("arbitrary"),
    )(*args)


def _proj_branch(y_ref, w_ref):
    return jnp.dot(y_ref[0], w_ref[...], preferred_element_type=F32)


def _mixer_mlp_kernel(*refs, branch, ff_chunk):
    x_ref, mod_ref, g_ref, win_ref, wout_ref = refs[:5]
    o_ref = refs[-1]
    x = x_ref[0]
    mod = mod_ref[0]
    if branch is not None:
        x = x + mod[2:3] * branch(*refs[5:-1])
    h = _norm_mod(x, g_ref[...], mod, 3).astype(BF16)
    acc = jnp.zeros(x.shape, F32)
    for f0 in range(0, D_FF, ff_chunk):
        u = jnp.maximum(jnp.dot(h, win_ref[:, f0:f0 + ff_chunk], preferred_element_type=F32), 0.0)
        acc = acc + jnp.dot((u * u).astype(BF16), wout_ref[f0:f0 + ff_chunk, :], preferred_element_type=F32)
    o_ref[0] = x + mod[5:6] * acc


def _mixer_mlp(x, mod, g, win_bf, wout_bf, branch=None, branch_args=(), branch_specs=(), tm_max=512):
    B, T, _ = x.shape
    tm = min(T, tm_max)
    specs = [s(tm) if callable(s) else s for s in branch_specs]
    return pl.pallas_call(
        functools.partial(_mixer_mlp_kernel, branch=branch, ff_chunk=1024),
        grid=(B, T // tm),
        in_specs=[_row_spec(tm), _mod_spec(), _const_spec((1, D_MODEL)),
                  _const_spec((D_MODEL, D_FF)), _const_spec((D_FF, D_MODEL))] + specs,
        out_specs=_row_spec(tm),
        out_shape=jax.ShapeDtypeStruct(x.shape, F32),
        compiler_params=_params("parallel", "parallel"),
    )(x, mod, g, win_bf, wout_bf, *branch_args)


DECAY_SCALE = float(np.exp(-0.5))


def _rwkv_prep_kernel(x_ref, xp_ref, xn_ref, g_ref, mod_ref, mu_ref, wrkv_ref, w1_ref, w2_ref, w0_ref, a1_ref, a2_ref,
                      a0_ref, g1_ref, g2_ref, kk_ref, kac_ref, kah_ref,
                      r_o, v_o, kk_o, g_o, lw0_o, lw1_o, key0_o, key1_o, a0_o, a1_o, *, tm):
    i = pl.program_id(1)
    last = pl.num_programs(1) - 1
    g = g_ref[...]
    mod = mod_ref[0]
    h = _norm_mod(x_ref[0], g, mod, 0)
    hp = _norm_mod(xp_ref[0][SUBLANES - 1:SUBLANES], g, mod, 0) * jnp.where(i == 0, 0.0, 1.0)
    hn = _norm_mod(xn_ref[0][0:1], g, mod, 0) * jnp.where(i == last, 0.0, 1.0)
    row = lax.broadcasted_iota(jnp.int32, (tm, 1), 0)
    prev = jnp.where(row == 0, hp, pltpu.roll(h, 1, 0))
    nxt = jnp.where(row == tm - 1, hn, pltpu.roll(h, tm - 1, 0))
    xx = 0.5 * (prev + nxt) - h
    mix = lambda m: (h + xx * mu_ref[m:m + 1]).astype(BF16)
    dot = functools.partial(jnp.dot, preferred_element_type=F32)
    r = dot(mix(0), wrkv_ref[0])
    k = dot(mix(2), wrkv_ref[1])
    v = dot(mix(3), wrkv_ref[2])
    r_o[0] = r.astype(BF16)
    v_o[0] = v.astype(BF16)
    g_o[0] = dot(_sigmoid(dot(mix(5), g1_ref[...])).astype(BF16), g2_ref[...]).astype(BF16)
    kk = k * kk_ref[...]
    kk_o[0] = (kk * lax.rsqrt(jnp.maximum(_seg_sum_wide(kk * kk), 1e-24))).astype(BF16)
    tw = jnp.tanh(dot(mix(1), w1_ref[...])).astype(BF16)
    ta = dot(mix(4), a1_ref[...]).astype(BF16)
    for d, (lw_o, key_o, a_o) in enumerate(((lw0_o, key0_o, a0_o), (lw1_o, key1_o, a1_o))):
        th = jnp.tanh(w0_ref[d:d + 1] + dot(tw[:, d * DECAY_LORA:(d + 1) * DECAY_LORA], w2_ref[d]))
        lw_o[0] = (-0.5 * DECAY_SCALE) * th + (-0.5 * DECAY_SCALE)
        ah = jnp.tanh(a0_ref[d:d + 1] + dot(ta[:, d * ICLR_LORA:(d + 1) * ICLR_LORA], a2_ref[d]))
        a_o[0] = (0.5 * ah + 0.5).astype(BF16)
        key_o[0] = (k * (kac_ref[...] + kah_ref[...] * ah)).astype(BF16)


def _rwkv_prepare(x, g, mod, p):
    B, T, _ = x.shape
    tm = min(T, 512)
    nb = tm // SUBLANES
    n_blk8 = T // SUBLANES
    prev_spec = pl.BlockSpec((1, SUBLANES, D_MODEL), lambda b, i: (b, jnp.maximum(i * nb - 1, 0), 0))
    next_spec = pl.BlockSpec((1, SUBLANES, D_MODEL), lambda b, i: (b, jnp.minimum((i + 1) * nb, n_blk8 - 1), 0))
    o_bf, o_f32 = jax.ShapeDtypeStruct(x.shape, BF16), jax.ShapeDtypeStruct(x.shape, F32)
    return pl.pallas_call(
        functools.partial(_rwkv_prep_kernel, tm=tm),
        grid=(B, T // tm),
        in_specs=[_row_spec(tm), prev_spec, next_spec, _const_spec((1, D_MODEL)), _mod_spec(),
                  _const_spec((6, D_MODEL)), _const_spec((3, D_MODEL, D_MODEL)),
                  _const_spec((D_MODEL, 2 * DECAY_LORA)), _const_spec((2, DECAY_LORA, D_MODEL)),
                  _const_spec((2, D_MODEL)),
                  _const_spec((D_MODEL, 2 * ICLR_LORA)), _const_spec((2, ICLR_LORA, D_MODEL)),
                  _const_spec((2, D_MODEL)),
                  _const_spec((D_MODEL, GATE_LORA)), _const_spec((GATE_LORA, D_MODEL)),
                  _const_spec((1, D_MODEL)), _const_spec((1, D_MODEL)), _const_spec((1, D_MODEL))],
        out_specs=[_row_spec(tm)] * 10,
        out_shape=[o_bf, o_bf, o_bf, o_bf, o_f32, o_f32, o_bf, o_bf, o_bf, o_bf],
        compiler_params=_params("parallel", "parallel"),
    )(x, x, x, g, mod, p["mu"], p["w_rkv"], p["w1"], p["w2"], p["w0"], p["a1"], p["a2"], p["a0"], p["g1"], p["g2"],
      p["k_k"], p["k_ac"], p["k_ah"])


def _wkv_kernel(r_ref, lw_ref, k_ref, v_ref, kk_ref, a_ref, s0_ref, y_ref, s_ref, state, *, n_chunks, n_pairs,
                reverse):
    C = CHUNK

    def mm(dims):
        return lambda a, b: lax.dot_general(a.astype(BF16), b.astype(BF16), dims, preferred_element_type=F32)

    hp = mm((((1,), (0,)), ((), ())))
    nt = mm((((1,), (1,)), ((), ())))
    tn = mm((((0,), (0,)), ((), ())))

    @pl.when(pl.program_id(2) == 0)
    def _():
        state[...] = s0_ref[0]

    pi = lax.broadcasted_iota(jnp.int32, (C, LANES), 0)
    pj = lax.broadcasted_iota(jnp.int32, (C, LANES), 1) % C
    strict = (pi < pj) if reverse else (pi > pj)
    incl = (pi <= pj) if reverse else (pi >= pj)
    blk = (pi // INV_BLOCK) == (pj // INV_BLOCK)
    eye = pi == pj
    ident = jnp.where(eye, 1.0, 0.0)
    head0 = lax.broadcasted_iota(jnp.int32, (1, LANES), 1) < RWKV_HEAD

    def stack(z):
        zb = z.astype(BF16)
        zero = jnp.zeros_like(zb)
        return jnp.concatenate([jnp.where(head0, zb, zero), jnp.where(head0, zero, zb)], axis=0)

    side = lambda x, y: jnp.concatenate([x, y], axis=1)
    diag_blocks = lambda z: jnp.where(head0, z[:C], z[C:])

    each = lambda f, *ls: [f(*xs) for xs in zip(*ls)]
    order = range(n_chunks - 1, -1, -1) if reverse else range(n_chunks)

    def cumsum(x):
        step = 1
        while step < C:
            if reverse:
                x = x + jnp.where(pi < C - step, pltpu.roll(x, C - step, 0), 0.0)
            else:
                x = x + jnp.where(pi >= step, pltpu.roll(x, step, 0), 0.0)
            step *= 2
        return x

    def chain(pairs):
        units = [(slice(c * C, (c + 1) * C), slice(p * LANES, (p + 1) * LANES)) for p in pairs for c in range(n_chunks)]
        load = lambda ref: [ref[0, rows, cols].astype(F32) for rows, cols in units]
        r, lw, k, v, kk, a = (load(ref) for ref in (r_ref, lw_ref, k_ref, v_ref, kk_ref, a_ref))
        cum = each(cumsum, lw)
        total = each(lambda z: z[0:1] if reverse else z[C - 1:C], cum)
        g_in = each(lambda z: jnp.exp(-z), cum)
        g_to_end = each(lambda t, z: jnp.exp(t - z), total, cum)
        b = each(lambda x, y: x * y, kk, a)
        a_u = each(lambda x, z, l: x * jnp.exp(z - l), kk, cum, lw)
        r_u = each(lambda x, z: x * jnp.exp(z), r, cum)
        b_s = each(lambda x, y: stack(x * y), b, g_in)
        k_s = each(lambda x, y: stack(x * y), k, g_in)
        v_s = each(stack, v)
        bh = each(lambda x, y: x * y, b, g_to_end)
        kh = each(lambda x, y: x * y, k, g_to_end)
        big = each(lambda a_, r_, b_, k_: nt(jnp.concatenate([a_, r_], axis=0), jnp.concatenate([b_, k_], axis=0)),
                   a_u, r_u, b_s, k_s)
        l_ab = each(lambda z: jnp.where(strict, z[:C, :LANES], 0.0), big)
        l_ak = each(lambda z: jnp.where(strict, z[:C, LANES:], 0.0), big)
        p_rb = each(lambda z: jnp.where(incl, z[C:, :LANES], 0.0), big)
        p_rk = each(lambda z: jnp.where(incl, z[C:, LANES:], 0.0), big)
        l_d = each(lambda z: jnp.where(blk, z, 0.0), l_ab)
        l_o = each(lambda x, y: x - y, l_ab, l_d)
        l2 = each(lambda d: hp(d, stack(d)), l_d)
        lpv = each(lambda x, y, vs: hp(jnp.concatenate([x, y], axis=0), vs), l_ak, p_rk, v_s)
        lv = each(lambda z: z[:C], lpv)
        prv = each(lambda z: z[C:], lpv)
        khv = each(tn, kh, v)
        l34 = each(lambda d, s2: hp(jnp.concatenate([d, s2], axis=0), stack(s2)), l_d, l2)
        p1 = each(lambda d, s2, s34: ident - d + s2 - s34[:C], l_d, l2, l34)
        x48 = each(lambda p, s34: hp(jnp.concatenate([p, s34[C:]], axis=0), stack(s34[C:])), p1, l34)
        p2 = each(lambda p, x: p + x[:C], p1, x48)
        t_d = each(lambda p, x: p + hp(p, stack(x[C:])), p2, x48)
        m1 = each(lambda t, o: hp(t, stack(o)), t_d, l_o)
        mm_ = each(lambda x, t: hp(x, side(stack(x), stack(t))), m1, t_d)
        y_d = each(lambda t, x: t - x[:, LANES:], t_d, mm_)
        t_inv = each(lambda y, x: y + hp(x[:, :LANES], stack(y)), y_d, mm_)
        w = each(lambda t, x, y: hp(t, side(stack(x), stack(y))), t_inv, a_u, lv)
        pw = each(lambda p, x: hp(p, side(stack(x[:, :LANES]), stack(x[:, LANES:]))), p_rb, w)
        bw = each(tn, bh, w)
        q1 = each(lambda x, y: x - y[:, :LANES], r_u, pw)
        y2 = each(lambda x, y: x - y[:, LANES:], prv, pw)
        g_mat = each(lambda t, y: jnp.where(eye, jnp.exp(t), 0.0) - diag_blocks(y[:, :LANES]), total, bw)
        h_mat = each(lambda x, y: diag_blocks(x) - diag_blocks(y[:, LANES:]), khv, bw)
        s = [state[p] for p in pairs]
        for c in order:
            for i, p in enumerate(pairs):
                u = i * n_chunks + c
                ys = hp(jnp.concatenate([q1[u], g_mat[u]], axis=0), stack(s[i]))
                y_ref[0, c * C:(c + 1) * C, p * LANES:(p + 1) * LANES] = ys[:C] + y2[u]
                s[i] = ys[C:] + h_mat[u]
        for i, p in enumerate(pairs):
            state[p] = s[i]
        return s

    final_state = chain(list(range(n_pairs)))

    @pl.when(pl.program_id(2) == pl.num_programs(2) - 1)
    def _():
        for p in range(n_pairs):
            s_ref[0, p] = final_state[p]


WKV_PAIRS = 4
WKV_CHUNKS = 8


def _wkv(r, lw, k, v, kk, a, s0, reverse):
    B, T, _ = r.shape
    rows = min(T, WKV_CHUNKS * CHUNK)
    n_steps = T // rows
    width = WKV_PAIRS * LANES
    tmap = (lambda b, p, j: (b, n_steps - 1 - j, p)) if reverse else (lambda b, p, j: (b, j, p))
    seq_spec = pl.BlockSpec((1, rows, width), tmap)
    st_spec = pl.BlockSpec((1, WKV_PAIRS, RWKV_HEAD, LANES), lambda b, p, j: (b, p, 0, 0))
    return pl.pallas_call(
        functools.partial(_wkv_kernel, n_chunks=rows // CHUNK, n_pairs=WKV_PAIRS, reverse=reverse),
        grid=(B, PAIRS // WKV_PAIRS, n_steps),
        in_specs=[seq_spec] * 6 + [st_spec],
        out_specs=[seq_spec, st_spec],
        out_shape=[jax.ShapeDtypeStruct(r.shape, F32), jax.ShapeDtypeStruct((B, PAIRS, RWKV_HEAD, LANES), F32)],
        scratch_shapes=[pltpu.VMEM((WKV_PAIRS, RWKV_HEAD, LANES), F32)],
        compiler_params=_params("parallel", "parallel", "arbitrary"),
    )(r, lw, k, v, kk, a, s0)


def _rwkv_branch(y0_ref, y1_ref, r_ref, v_ref, k0_ref, k1_ref, g_ref, rk_ref, lng_ref, lnb_ref, wo_ref):
    y = y0_ref[0] + y1_ref[0]
    inv_n = 1.0 / RWKV_HEAD
    mean = _seg_sum_wide(y) * inv_n
    yc = y - mean
    var = _seg_sum_wide(yc * yc) * inv_n
    yn = yc * lax.rsqrt(var + GN_EPS) * lng_ref[...] + lnb_ref[...]
    f32 = lambda ref: ref[0].astype(F32)
    bonus = _seg_sum_wide(f32(r_ref) * (f32(k0_ref) + f32(k1_ref)) * rk_ref[...]) * f32(v_ref)
    return jnp.dot(((yn + bonus) * f32(g_ref)).astype(BF16), wo_ref[...], preferred_element_type=F32)


def _rwkv_branch_specs():
    vec = _const_spec((1, D_MODEL))
    return [_row_spec] * 7 + [vec, vec, vec, _const_spec((D_MODEL, D_MODEL))]


HALO = SUBLANES


def _pool_branch(x_ref, xp_ref, xn_ref, g_ref, mod_ref, w_ref, sc_ref, *, seq_len):
    tm = x_ref.shape[1]
    i = pl.program_id(1)
    last = pl.num_programs(1) - 1
    g = g_ref[...]
    mod = mod_ref[0]
    h = _norm_mod(x_ref[0], g, mod, 0)
    hp = _norm_mod(xp_ref[0], g, mod, 0) * jnp.where(i == 0, 0.0, 1.0)
    hn = _norm_mod(xn_ref[0], g, mod, 0) * jnp.where(i == last, 0.0, 1.0)
    ext = jnp.concatenate([hp, h, hn], axis=0)
    n_ext = tm + 2 * HALO
    t = i * tm + lax.broadcasted_iota(jnp.int32, (tm, 1), 0)
    outs = []
    for gi, win in enumerate(POOL_WINDOWS):
        e = ext[:, gi * POOL_GROUP:(gi + 1) * POOL_GROUP]
        acc = e + pltpu.roll(e, 1, 0)
        step = 1
        while 2 * step < win:
            acc = pltpu.roll(acc, step, 0) + pltpu.roll(acc, n_ext - step, 0)
            step *= 2
        half = win // 2
        cnt = (jnp.minimum(t + half, seq_len) - jnp.maximum(t - half, 0)).astype(F32)
        pooled = acc[HALO:HALO + tm] / cnt - e[HALO:HALO + tm]
        outs.append(jnp.dot(pooled.astype(BF16), w_ref[gi], preferred_element_type=F32))
    return jnp.concatenate(outs, axis=1) * sc_ref[...]


def _pool_branch_specs(seq_len):
    n_blk = seq_len // HALO
    prev_spec = lambda tm: pl.BlockSpec((1, HALO, D_MODEL), lambda b, i: (b, jnp.maximum(i * (tm // HALO) - 1, 0), 0))
    next_spec = lambda tm: pl.BlockSpec(
        (1, HALO, D_MODEL), lambda b, i: (b, jnp.minimum((i + 1) * (tm // HALO), n_blk - 1), 0))
    return [_row_spec, prev_spec, next_spec, _const_spec((1, D_MODEL)), lambda tm: _mod_spec(),
            _const_spec((len(POOL_WINDOWS), POOL_GROUP, POOL_GROUP)), _const_spec((1, D_MODEL))]


def _rope_tables(n_tokens):
    rows = n_tokens // GRID_W
    n_freq = HEAD_DIM // 4
    inv = ROPE_THETA ** (-jnp.arange(n_freq, dtype=F32) / n_freq)
    ang_r = jnp.arange(rows, dtype=F32)[:, None] * inv
    ang_c = jnp.arange(GRID_W, dtype=F32)[:, None] * inv
    ang = jnp.concatenate([
        jnp.broadcast_to(ang_r[:, None, :], (rows, GRID_W, n_freq)),
        jnp.broadcast_to(ang_c[None, :, :], (rows, GRID_W, n_freq))], axis=-1).reshape(rows * GRID_W, 2 * n_freq)
    cos, sin = jnp.cos(ang), jnp.sin(ang)
    return jnp.tile(cos, (1, 4)), jnp.tile(sin, (1, 4))


def kernel(x, c, ctx, c_ctx, w_mod, b_mod, norm1_g, norm2_g, mlp_w_in, mlp_w_out, attn_w_qkv, attn_q_gain, attn_k_gain, attn_w_o, rwkv_mu, rwkv_w_rkv, rwkv_w0, rwkv_w1, rwkv_w2, rwkv_a0, rwkv_a1, rwkv_a2, rwkv_g1, rwkv_g2, rwkv_k_k, rwkv_k_a, rwkv_r_k, rwkv_ln_g, rwkv_ln_b, rwkv_w_o, pool_w, pool_scale):
    B, S, _ = x.shape
    L = ctx.shape[1]
    depth = w_mod.shape[0]
    assert x.shape[2] == D_MODEL and S % (4 * CHUNK) == 0 and L % CHUNK == 0 and S % GRID_W == 0

    n_rows = -(-(B + 1) // SUBLANES) * SUBLANES
    cvec = jnp.concatenate([c, c_ctx[None], jnp.zeros((n_rows - B - 1, D_MODEL), F32)], axis=0)
    mod_all = _modulation(cvec, w_mod, b_mod).reshape(depth, n_rows, N_MOD, D_MODEL)
    cos_t, sin_t = _rope_tables(S)
    zero_tab = jnp.zeros((L, LANES), F32)
    row = lambda a: a.reshape(1, -1)

    for i in range(depth):
        last = i == depth - 1
        j = i // N_MIXERS
        mod_l = mod_all[i, :B]
        mod_c = jnp.broadcast_to(mod_all[i, B][None], (B, N_MOD, D_MODEL))
        g1 = row(norm1_g[i])
        kind = i % N_MIXERS
        if kind == 0:
            w_qkv = attn_w_qkv[j].astype(BF16)
            w_o = attn_w_o[j].astype(BF16)
            qg = jnp.tile(row(attn_q_gain[j]), (1, 2))
            kg = jnp.tile(row(attn_k_gain[j]), (1, 2))
            q_l, k_l, v_l = _qkv_project(x, g1, mod_l, w_qkv, qg, kg, cos_t, sin_t, True)
            q_c, k_c, v_c = _qkv_project(ctx, g1, mod_c, w_qkv, qg, kg, zero_tab, zero_tab, False)
            proj_specs = [_row_spec, _const_spec((D_MODEL, D_MODEL))]
            mix_l = (_proj_branch, (_attention(q_l, [(k_l, v_l), (k_c, v_c)]), w_o), proj_specs, 512)
            if not last:
                mix_c = (_proj_branch, (_attention(q_c, [(k_c, v_c)]), w_o), proj_specs, 512)
        elif kind == 1:
            p = {
                "mu": rwkv_mu[j], "w_rkv": rwkv_w_rkv[j].astype(BF16),
                "w1": jnp.concatenate([rwkv_w1[j, 0], rwkv_w1[j, 1]], axis=1).astype(BF16),
                "w2": (0.5 * rwkv_w2[j]).astype(BF16), "w0": 0.5 * rwkv_w0[j],
                "a1": jnp.concatenate([rwkv_a1[j, 0], rwkv_a1[j, 1]], axis=1).astype(BF16),
                "a2": (0.5 * rwkv_a2[j]).astype(BF16), "a0": 0.5 * rwkv_a0[j],
                "g1": rwkv_g1[j].astype(BF16), "g2": rwkv_g2[j].astype(BF16),
                "k_k": row(rwkv_k_k[j]), "k_ac": row(1.0 - 0.5 * rwkv_k_a[j]), "k_ah": row(0.5 * rwkv_k_a[j]), "r_k": row(rwkv_r_k[j]),
                "ln_g": row(rwkv_ln_g[j]), "ln_b": row(rwkv_ln_b[j]), "w_o": rwkv_w_o[j].astype(BF16),
            }
            r_l, v_l, kk_l, g_l, lw0_l, lw1_l, key0_l, key1_l, a0_l, a1_l = _rwkv_prepare(x, g1, mod_l, p)
            r_c, v_c, kk_c, g_c, lw0_c, lw1_c, key0_c, key1_c, a0_c, a1_c = _rwkv_prepare(ctx, g1, mod_c, p)
            zero_state = jnp.zeros((B, PAIRS, RWKV_HEAD, LANES), F32)
            ys_l, ys_c = [], []
            for rev, (lw_l, key_l, a_l, lw_c, key_c, a_c) in enumerate(
                    ((lw0_l, key0_l, a0_l, lw0_c, key0_c, a0_c), (lw1_l, key1_l, a1_l, lw1_c, key1_c, a1_c))):
                y_c, s_ctx = _wkv(r_c, lw_c, key_c, v_c, kk_c, a_c, zero_state, bool(rev))
                y_l, _ = _wkv(r_l, lw_l, key_l, v_l, kk_l, a_l, s_ctx, bool(rev))
                ys_l.append(y_l)
                ys_c.append(y_c)
            consts = (p["r_k"], p["ln_g"], p["ln_b"], p["w_o"])
            mix_l = (_rwkv_branch, (ys_l[0], ys_l[1], r_l, v_l, key0_l, key1_l, g_l) + consts, _rwkv_branch_specs(), 512)
            if not last:
                mix_c = (_rwkv_branch, (ys_c[0], ys_c[1], r_c, v_c, key0_c, key1_c, g_c) + consts,
                         _rwkv_branch_specs(), 256)
        else:
            w_p = pool_w[j].astype(BF16)
            sc = row(pool_scale[j])
            mix_l = (functools.partial(_pool_branch, seq_len=S), (x, x, x, g1, mod_l, w_p, sc), _pool_branch_specs(S), 512)
            if not last:
                mix_c = (functools.partial(_pool_branch, seq_len=L), (ctx, ctx, ctx, g1, mod_c, w_p, sc),
                         _pool_branch_specs(L), 256)
        g2 = row(norm2_g[i])
        w_in = mlp_w_in[i].astype(BF16)
        w_out = mlp_w_out[i].astype(BF16)
        x = _mixer_mlp(x, mod_l, g2, w_in, w_out, *mix_l)
        if not last:
            ctx = _mixer_mlp(ctx, mod_c, g2, w_in, w_out, *mix_c)
    return x
```

```python
import functools

import jax
import jax.numpy as jnp
import numpy as np
from jax import lax
from jax.experimental import pallas as pl
from jax.experimental.pallas import tpu as pltpu

F32 = jnp.float32
BF16 = jnp.bfloat16

D_MODEL = 1024
GRID_W = 64
N_MIXERS = 3
N_MOD = 6
EPS = 1e-6
N_HEADS = 16
N_KV_HEADS = 4
HEAD_DIM = 64
GQA_REP = N_HEADS // N_KV_HEADS
Q_WIDTH = N_HEADS * HEAD_DIM
KV_WIDTH = N_KV_HEADS * HEAD_DIM
QKV_WIDTH = Q_WIDTH + 2 * KV_WIDTH
ROPE_THETA = 10000.0
RWKV_HEAD = 64
DECAY_LORA = 64
ICLR_LORA = 64
GATE_LORA = 160
GN_EPS = RWKV_HEAD * 1e-5
POOL_WINDOWS = (2, 4, 8, 16)
POOL_GROUP = D_MODEL // len(POOL_WINDOWS)
D_FF = 4 * D_MODEL

LANES = 128
SUBLANES = 8
PAIRS = D_MODEL // LANES
CHUNK = 64
INV_BLOCK = 16
VMEM_LIMIT = 56 * 1024 * 1024

HIGHEST = lax.Precision.HIGHEST


def _params(*sem):
    return pltpu.CompilerParams(dimension_semantics=sem, vmem_limit_bytes=VMEM_LIMIT)


def _const_spec(shape):
    zeros = (0,) * len(shape)
    return pl.BlockSpec(shape, lambda *_: zeros, pipeline_mode=pl.Buffered(1))


def _row_spec(tm, width=D_MODEL):
    return pl.BlockSpec((1, tm, width), lambda b, i: (b, i, 0))


def _mod_spec():
    return pl.BlockSpec((1, N_MOD, D_MODEL), lambda b, i: (b, 0, 0))


def _norm_mod(x, g, mod, k):
    y = x * lax.rsqrt(jnp.mean(x * x, axis=-1, keepdims=True) + EPS) * g
    return y * (1.0 + mod[k + 1:k + 2]) + mod[k:k + 1]


def _seg_sum(x):
    i = lax.broadcasted_iota(jnp.int32, (LANES, LANES), 0) // HEAD_DIM
    j = lax.broadcasted_iota(jnp.int32, (LANES, LANES), 1) // HEAD_DIM
    return jnp.dot(x.astype(BF16), jnp.where(i == j, 1.0, 0.0).astype(BF16), preferred_element_type=F32)


def _sigmoid(z):
    return 0.5 * jnp.tanh(0.5 * z) + 0.5


def _seg_sum_wide(x):
    return jnp.concatenate([_seg_sum(x[:, p * LANES:(p + 1) * LANES]) for p in range(x.shape[1] // LANES)], axis=1)


def _mod_kernel(c_ref, w_ref, b_ref, o_ref):
    c = c_ref[...]
    s = c * jax.nn.sigmoid(c)
    o_ref[0] = jnp.dot(s, w_ref[0], precision=HIGHEST, preferred_element_type=F32) + b_ref[0]


def _modulation(cvec, w_mod, b_mod):
    depth = w_mod.shape[0]
    rows = cvec.shape[0]
    tn = 1536
    return pl.pallas_call(
        _mod_kernel,
        grid=(depth, N_MOD * D_MODEL // tn),
        in_specs=[pl.BlockSpec((rows, D_MODEL), lambda l, j: (0, 0)),
                  pl.BlockSpec((1, D_MODEL, tn), lambda l, j: (l, 0, j)),
                  pl.BlockSpec((1, 1, tn), lambda l, j: (l, 0, j))],
        out_specs=pl.BlockSpec((1, rows, tn), lambda l, j: (l, 0, j)),
        out_shape=jax.ShapeDtypeStruct((depth, rows, N_MOD * D_MODEL), F32),
        compiler_params=_params("parallel", "parallel"),
    )(cvec, w_mod, b_mod.reshape(depth, 1, N_MOD * D_MODEL))


VT_ROWS = HEAD_DIM + 16
Q_SCALE = float(HEAD_DIM ** -0.5 * np.log2(np.e))


def _qkv_kernel(x_ref, g_ref, mod_ref, w_ref, qg_ref, kg_ref, cos_ref, sin_ref, q_ref, k_ref, v_ref, *, use_rope):
    h = _norm_mod(x_ref[0], g_ref[...], mod_ref[0], 0).astype(BF16)
    acc = jnp.dot(h, w_ref[...], preferred_element_type=F32)
    src_lane = lax.broadcasted_iota(jnp.int32, (2 * LANES, LANES), 0) % LANES
    dst_lane = lax.broadcasted_iota(jnp.int32, (2 * LANES, LANES), 1)
    half = HEAD_DIM // 2
    first_half = (dst_lane % HEAD_DIM) < half
    rot_mat = jnp.where(first_half & (src_lane == dst_lane + half), -1.0,
                        jnp.where(jnp.logical_not(first_half) & (src_lane == dst_lane - half), 1.0, 0.0)).astype(BF16)

    def head_pair(xp, gain, scale):
        y = xp * lax.rsqrt(_seg_sum(xp * xp) * (1.0 / HEAD_DIM) + EPS) * gain
        if use_rope:
            y_hi = y.astype(BF16)
            y_lo = (y - y_hi.astype(F32)).astype(BF16)
            rot = jnp.dot(jnp.concatenate([y_hi, y_lo], axis=1), rot_mat, preferred_element_type=F32)
            y = y * cos_ref[...] + rot * sin_ref[...]
        return (y * scale).astype(BF16)

    for p in range(Q_WIDTH // LANES):
        y = head_pair(acc[:, p * LANES:(p + 1) * LANES], qg_ref[...], Q_SCALE)
        q_ref[0, 2 * p] = y[:, :HEAD_DIM]
        q_ref[0, 2 * p + 1] = y[:, HEAD_DIM:]
    for p in range(KV_WIDTH // LANES):
        y = head_pair(acc[:, Q_WIDTH + p * LANES:Q_WIDTH + (p + 1) * LANES], kg_ref[...], 1.0)
        k_ref[0, 2 * p] = y[:, :HEAD_DIM]
        k_ref[0, 2 * p + 1] = y[:, HEAD_DIM:]
    ones = jnp.ones((VT_ROWS - HEAD_DIM, acc.shape[0]), F32)
    for p in range(KV_WIDTH // LANES):
        c0 = Q_WIDTH + KV_WIDTH + p * LANES
        vt = acc[:, c0:c0 + LANES].T
        v_ref[0, 2 * p] = jnp.concatenate([vt[:HEAD_DIM], ones], axis=0).astype(BF16)
        v_ref[0, 2 * p + 1] = jnp.concatenate([vt[HEAD_DIM:], ones], axis=0).astype(BF16)


def _qkv_project(x, g, mod, w_bf, qg, kg, cos_t, sin_t, use_rope):
    B, T, _ = x.shape
    tm = min(T, 1024)
    tab_spec = pl.BlockSpec((tm, LANES), lambda b, i: (i, 0))
    head_spec = lambda n: pl.BlockSpec((1, n, tm, HEAD_DIM), lambda b, i: (b, 0, i, 0))
    return pl.pallas_call(
        functools.partial(_qkv_kernel, use_rope=use_rope),
        grid=(B, T // tm),
        in_specs=[_row_spec(tm), _const_spec((1, D_MODEL)), _mod_spec(), _const_spec((D_MODEL, QKV_WIDTH)),
                  _const_spec((1, LANES)), _const_spec((1, LANES)), tab_spec, tab_spec],
        out_specs=[head_spec(N_HEADS), head_spec(N_KV_HEADS),
                   pl.BlockSpec((1, N_KV_HEADS, VT_ROWS, tm), lambda b, i: (b, 0, 0, i))],
        out_shape=[jax.ShapeDtypeStruct((B, N_HEADS, T, HEAD_DIM), BF16),
                   jax.ShapeDtypeStruct((B, N_KV_HEADS, T, HEAD_DIM), BF16),
                   jax.ShapeDtypeStruct((B, N_KV_HEADS, VT_ROWS, T), BF16)],
        compiler_params=_params("parallel", "parallel"),
    )(x, g, mod, w_bf, qg, kg, cos_t, sin_t)


def _attn_kernel(*refs, n_src, tq):
    q_ref = refs[0]
    kv_refs = refs[1:1 + 2 * n_src]
    o_ref = refs[1 + 2 * n_src]
    s_scr, m_scr = refs[2 + 2 * n_src:]
    nq = GQA_REP * tq
    i = pl.program_id(0)
    pieces = []
    row = 0
    for s in range(n_src):
        keys_total = kv_refs[2 * s].shape[2]
        ck = min(ATTN_KV_CHUNK, keys_total)
        for r0 in range(0, keys_total, ck):
            pieces.append((kv_refs[2 * s], kv_refs[2 * s + 1], r0, ck, row))
            row += ck

    def run(with_pass2):
        q = q_ref[0].reshape(nq, HEAD_DIM)
        m8 = None
        if with_pass2:
            m_prev = m_scr[...]
            acc = jnp.zeros((VT_ROWS, nq), F32)
        for k_ref, vt_ref, r0, ck, c0 in pieces:
            if with_pass2:
                p = jnp.exp2(s_scr[c0:c0 + ck, :nq] - m_prev).astype(BF16)
                acc = acc + jnp.dot(vt_ref[0, 0, :, r0:r0 + ck], p, preferred_element_type=F32)
            s_blk = lax.dot_general(k_ref[0, 0, r0:r0 + ck, :], q, (((1,), (1,)), ((), ())),
                                    preferred_element_type=F32)
            s_scr[c0:c0 + ck, :nq] = s_blk
            bm = jnp.max(s_blk.reshape(ck // SUBLANES, SUBLANES, nq), axis=0)
            m8 = bm if m8 is None else jnp.maximum(m8, bm)
        m_scr[...] = jnp.max(m8, axis=0, keepdims=True)
        if with_pass2:
            o = (acc[:HEAD_DIM] / acc[HEAD_DIM:HEAD_DIM + 1]).T
            o_ref[0] = jnp.concatenate([o[h * tq:(h + 1) * tq] for h in range(GQA_REP)], axis=1).astype(BF16)

    pl.when(i == 0)(lambda: run(False))
    pl.when(i > 0)(lambda: run(True))


ATTN_KV_CHUNK = 256
ATTN_SCORE_PAD = LANES


def _attention(q, kv_sources):
    B, _, T, _ = q.shape
    tq = min(T, 256)
    n_tiles = T // tq
    n_all = B * N_KV_HEADS * n_tiles
    total = sum(k.shape[2] for k, _ in kv_sources)

    def tile(t):
        t = jnp.clip(t, 0, n_all - 1)
        return t // (N_KV_HEADS * n_tiles), (t // n_tiles) % N_KV_HEADS, t % n_tiles

    def q_map(t):
        b, g, i = tile(t)
        return b, g, i, 0

    def k_map(t):
        b, g, _ = tile(t)
        return b, g, 0, 0

    def v_map(t):
        b, g, _ = tile(t - 1)
        return b, g, 0, 0

    def o_map(t):
        b, g, i = tile(t - 1)
        return b, i, g

    in_specs = [pl.BlockSpec((1, GQA_REP, tq, HEAD_DIM), q_map)]
    args = [q]
    for k, vt in kv_sources:
        in_specs += [pl.BlockSpec((1, 1, k.shape[2], HEAD_DIM), k_map),
                     pl.BlockSpec((1, 1, VT_ROWS, k.shape[2]), v_map)]
        args += [k, vt]
    return pl.pallas_call(
        functools.partial(_attn_kernel, n_src=len(kv_sources), tq=tq),
        grid=(n_all + 1,),
        in_specs=in_specs,
        out_specs=pl.BlockSpec((1, tq, GQA_REP * HEAD_DIM), o_map),
        out_shape=jax.ShapeDtypeStruct((B, T, Q_WIDTH), BF16),
        scratch_shapes=[pltpu.VMEM((total, GQA_REP * tq + ATTN_SCORE_PAD), F32), pltpu.VMEM((1, GQA_REP * tq), F32)],
        compiler_params=_params("arbitrary"),
    )(*args)


def _proj_branch(y_ref, w_ref):
    return jnp.dot(y_ref[0], w_ref[...], preferred_element_type=F32)


def _mixer_mlp_kernel(*refs, branch, ff_chunk):
    x_ref, mod_ref, g_ref, win_ref, wout_ref = refs[:5]
    o_ref = refs[-1]
    x = x_ref[0]
    mod = mod_ref[0]
    if branch is not None:
        x = x + mod[2:3] * branch(*refs[5:-1])
    h = _norm_mod(x, g_ref[...], mod, 3).astype(BF16)
    acc = jnp.zeros(x.shape, F32)
    for f0 in range(0, D_FF, ff_chunk):
        u = jnp.maximum(jnp.dot(h, win_ref[:, f0:f0 + ff_chunk], preferred_element_type=F32), 0.0)
        acc = acc + jnp.dot((u * u).astype(BF16), wout_ref[f0:f0 + ff_chunk, :], preferred_element_type=F32)
    o_ref[0] = x + mod[5:6] * acc


def _mixer_mlp(x, mod, g, win_bf, wout_bf, branch=None, branch_args=(), branch_specs=(), tm_max=512):
    B, T, _ = x.shape
    tm = min(T, tm_max)
    specs = [s(tm) if callable(s) else s for s in branch_specs]
    return pl.pallas_call(
        functools.partial(_mixer_mlp_kernel, branch=branch, ff_chunk=1024),
        grid=(B, T // tm),
        in_specs=[_row_spec(tm), _mod_spec(), _const_spec((1, D_MODEL)),
                  _const_spec((D_MODEL, D_FF)), _const_spec((D_FF, D_MODEL))] + specs,
        out_specs=_row_spec(tm),
        out_shape=jax.ShapeDtypeStruct(x.shape, F32),
        compiler_params=_params("parallel", "parallel"),
    )(x, mod, g, win_bf, wout_bf, *branch_args)


DECAY_SCALE = float(np.exp(-0.5))


def _rwkv_prep_kernel(x_ref, xp_ref, xn_ref, g_ref, mod_ref, mu_ref, wrkv_ref, w1_ref, w2_ref, w0_ref, a1_ref, a2_ref,
                      a0_ref, g1_ref, g2_ref, kk_ref, kac_ref, kah_ref,
                      r_o, v_o, kk_o, g_o, lw0_o, lw1_o, key0_o, key1_o, a0_o, a1_o, *, tm):
    i = pl.program_id(1)
    last = pl.num_programs(1) - 1
    g = g_ref[...]
    mod = mod_ref[0]
    h = _norm_mod(x_ref[0], g, mod, 0)
    hp = _norm_mod(xp_ref[0][SUBLANES - 1:SUBLANES], g, mod, 0) * jnp.where(i == 0, 0.0, 1.0)
    hn = _norm_mod(xn_ref[0][0:1], g, mod, 0) * jnp.where(i == last, 0.0, 1.0)
    row = lax.broadcasted_iota(jnp.int32, (tm, 1), 0)
    prev = jnp.where(row == 0, hp, pltpu.roll(h, 1, 0))
    nxt = jnp.where(row == tm - 1, hn, pltpu.roll(h, tm - 1, 0))
    xx = 0.5 * (prev + nxt) - h
    mix = lambda m: (h + xx * mu_ref[m:m + 1]).astype(BF16)
    dot = functools.partial(jnp.dot, preferred_element_type=F32)
    r = dot(mix(0), wrkv_ref[0])
    k = dot(mix(2), wrkv_ref[1])
    v = dot(mix(3), wrkv_ref[2])
    r_o[0] = r.astype(BF16)
    v_o[0] = v.astype(BF16)
    g_o[0] = dot(_sigmoid(dot(mix(5), g1_ref[...])).astype(BF16), g2_ref[...]).astype(BF16)
    kk = k * kk_ref[...]
    kk_o[0] = (kk * lax.rsqrt(jnp.maximum(_seg_sum_wide(kk * kk), 1e-24))).astype(BF16)
    tw = jnp.tanh(dot(mix(1), w1_ref[...])).astype(BF16)
    ta = dot(mix(4), a1_ref[...]).astype(BF16)
    for d, (lw_o, key_o, a_o) in enumerate(((lw0_o, key0_o, a0_o), (lw1_o, key1_o, a1_o))):
        th = jnp.tanh(w0_ref[d:d + 1] + dot(tw[:, d * DECAY_LORA:(d + 1) * DECAY_LORA], w2_ref[d]))
        lw_o[0] = (-0.5 * DECAY_SCALE) * th + (-0.5 * DECAY_SCALE)
        ah = jnp.tanh(a0_ref[d:d + 1] + dot(ta[:, d * ICLR_LORA:(d + 1) * ICLR_LORA], a2_ref[d]))
        a_o[0] = (0.5 * ah + 0.5).astype(BF16)
        key_o[0] = (k * (kac_ref[...] + kah_ref[...] * ah)).astype(BF16)


def _rwkv_prepare(x, g, mod, p):
    B, T, _ = x.shape
    tm = min(T, 512)
    nb = tm // SUBLANES
    n_blk8 = T // SUBLANES
    prev_spec = pl.BlockSpec((1, SUBLANES, D_MODEL), lambda b, i: (b, jnp.maximum(i * nb - 1, 0), 0))
    next_spec = pl.BlockSpec((1, SUBLANES, D_MODEL), lambda b, i: (b, jnp.minimum((i + 1) * nb, n_blk8 - 1), 0))
    o_bf, o_f32 = jax.ShapeDtypeStruct(x.shape, BF16), jax.ShapeDtypeStruct(x.shape, F32)
    return pl.pallas_call(
        functools.partial(_rwkv_prep_kernel, tm=tm),
        grid=(B, T // tm),
        in_specs=[_row_spec(tm), prev_spec, next_spec, _const_spec((1, D_MODEL)), _mod_spec(),
                  _const_spec((6, D_MODEL)), _const_spec((3, D_MODEL, D_MODEL)),
                  _const_spec((D_MODEL, 2 * DECAY_LORA)), _const_spec((2, DECAY_LORA, D_MODEL)),
                  _const_spec((2, D_MODEL)),
                  _const_spec((D_MODEL, 2 * ICLR_LORA)), _const_spec((2, ICLR_LORA, D_MODEL)),
                  _const_spec((2, D_MODEL)),
                  _const_spec((D_MODEL, GATE_LORA)), _const_spec((GATE_LORA, D_MODEL)),
                  _const_spec((1, D_MODEL)), _const_spec((1, D_MODEL)), _const_spec((1, D_MODEL))],
        out_specs=[_row_spec(tm)] * 10,
        out_shape=[o_bf, o_bf, o_bf, o_bf, o_f32, o_f32, o_bf, o_bf, o_bf, o_bf],
        compiler_params=_params("parallel", "parallel"),
    )(x, x, x, g, mod, p["mu"], p["w_rkv"], p["w1"], p["w2"], p["w0"], p["a1"], p["a2"], p["a0"], p["g1"], p["g2"],
      p["k_k"], p["k_ac"], p["k_ah"])


def _wkv_kernel(r_ref, lw_ref, k_ref, v_ref, kk_ref, a_ref, s0_ref, y_ref, s_ref, state, *, n_chunks, n_pairs,
                reverse):
    C = CHUNK

    def mm(dims):
        return lambda a, b: lax.dot_general(a.astype(BF16), b.astype(BF16), dims, preferred_element_type=F32)

    hp = mm((((1,), (0,)), ((), ())))
    nt = mm((((1,), (1,)), ((), ())))
    tn = mm((((0,), (0,)), ((), ())))

    @pl.when(pl.program_id(2) == 0)
    def _():
        state[...] = s0_ref[0]

    pi = lax.broadcasted_iota(jnp.int32, (C, LANES), 0)
    pj = lax.broadcasted_iota(jnp.int32, (C, LANES), 1) % C
    strict = (pi < pj) if reverse else (pi > pj)
    incl = (pi <= pj) if reverse else (pi >= pj)
    blk = (pi // INV_BLOCK) == (pj // INV_BLOCK)
    eye = pi == pj
    ident = jnp.where(eye, 1.0, 0.0)
    head0 = lax.broadcasted_iota(jnp.int32, (1, LANES), 1) < RWKV_HEAD

    def stack(z):
        zb = z.astype(BF16)
        zero = jnp.zeros_like(zb)
        return jnp.concatenate([jnp.where(head0, zb, zero), jnp.where(head0, zero, zb)], axis=0)

    side = lambda x, y: jnp.concatenate([x, y], axis=1)
    diag_blocks = lambda z: jnp.where(head0, z[:C], z[C:])

    each = lambda f, *ls: [f(*xs) for xs in zip(*ls)]
    order = range(n_chunks - 1, -1, -1) if reverse else range(n_chunks)

    def cumsum(x):
        step = 1
        while step < C:
            if reverse:
                x = x + jnp.where(pi < C - step, pltpu.roll(x, C - step, 0), 0.0)
            else:
                x = x + jnp.where(pi >= step, pltpu.roll(x, step, 0), 0.0)
            step *= 2
        return x

    def chain(pairs):
        units = [(slice(c * C, (c + 1) * C), slice(p * LANES, (p + 1) * LANES)) for p in pairs for c in range(n_chunks)]
        load = lambda ref: [ref[0, rows, cols].astype(F32) for rows, cols in units]
        r, lw, k, v, kk, a = (load(ref) for ref in (r_ref, lw_ref, k_ref, v_ref, kk_ref, a_ref))
        cum = each(cumsum, lw)
        total = each(lambda z: z[0:1] if reverse else z[C - 1:C], cum)
        g_in = each(lambda z: jnp.exp(-z), cum)
        g_to_end = each(lambda t, z: jnp.exp(t - z), total, cum)
        b = each(lambda x, y: x * y, kk, a)
        a_u = each(lambda x, z, l: x * jnp.exp(z - l), kk, cum, lw)
        r_u = each(lambda x, z: x * jnp.exp(z), r, cum)
        b_s = each(lambda x, y: stack(x * y), b, g_in)
        k_s = each(lambda x, y: stack(x * y), k, g_in)
        v_s = each(stack, v)
        bh = each(lambda x, y: x * y, b, g_to_end)
        kh = each(lambda x, y: x * y, k, g_to_end)
        big = each(lambda a_, r_, b_, k_: nt(jnp.concatenate([a_, r_], axis=0), jnp.concatenate([b_, k_], axis=0)),
                   a_u, r_u, b_s, k_s)
        l_ab = each(lambda z: jnp.where(strict, z[:C, :LANES], 0.0), big)
        l_ak = each(lambda z: jnp.where(strict, z[:C, LANES:], 0.0), big)
        p_rb = each(lambda z: jnp.where(incl, z[C:, :LANES], 0.0), big)
        p_rk = each(lambda z: jnp.where(incl, z[C:, LANES:], 0.0), big)
        l_d = each(lambda z: jnp.where(blk, z, 0.0), l_ab)
        l_o = each(lambda x, y: x - y, l_ab, l_d)
        l2 = each(lambda d: hp(d, stack(d)), l_d)
        lpv = each(lambda x, y, vs: hp(jnp.concatenate([x, y], axis=0), vs), l_ak, p_rk, v_s)
        lv = each(lambda z: z[:C], lpv)
        prv = each(lambda z: z[C:], lpv)
        khv = each(tn, kh, v)
        l34 = each(lambda d, s2: hp(jnp.concatenate([d, s2], axis=0), stack(s2)), l_d, l2)
        p1 = each(lambda d, s2, s34: ident - d + s2 - s34[:C], l_d, l2, l34)
        x48 = each(lambda p, s34: hp(jnp.concatenate([p, s34[C:]], axis=0), stack(s34[C:])), p1, l34)
        p2 = each(lambda p, x: p + x[:C], p1, x48)
        t_d = each(lambda p, x: p + hp(p, stack(x[C:])), p2, x48)
        m1 = each(lambda t, o: hp(t, stack(o)), t_d, l_o)
        mm_ = each(lambda x, t: hp(x, side(stack(x), stack(t))), m1, t_d)
        y_d = each(lambda t, x: t - x[:, LANES:], t_d, mm_)
        t_inv = each(lambda y, x: y + hp(x[:, :LANES], stack(y)), y_d, mm_)
        w = each(lambda t, x, y: hp(t, side(stack(x), stack(y))), t_inv, a_u, lv)
        pw = each(lambda p, x: hp(p, side(stack(x[:, :LANES]), stack(x[:, LANES:]))), p_rb, w)
        bw = each(tn, bh, w)
        q1 = each(lambda x, y: x - y[:, :LANES], r_u, pw)
        y2 = each(lambda x, y: x - y[:, LANES:], prv, pw)
        g_mat = each(lambda t, y: jnp.where(eye, jnp.exp(t), 0.0) - diag_blocks(y[:, :LANES]), total, bw)
        h_mat = each(lambda x, y: diag_blocks(x) - diag_blocks(y[:, LANES:]), khv, bw)
        s = [state[p] for p in pairs]
        for c in order:
            for i, p in enumerate(pairs):
                u = i * n_chunks + c
                ys = hp(jnp.concatenate([q1[u], g_mat[u]], axis=0), stack(s[i]))
                y_ref[0, c * C:(c + 1) * C, p * LANES:(p + 1) * LANES] = ys[:C] + y2[u]
                s[i] = ys[C:] + h_mat[u]
        for i, p in enumerate(pairs):
            state[p] = s[i]
        return s

    final_state = chain(list(range(n_pairs)))

    @pl.when(pl.program_id(2) == pl.num_programs(2) - 1)
    def _():
        for p in range(n_pairs):
            s_ref[0, p] = final_state[p]


WKV_PAIRS = 4
WKV_CHUNKS = 8


def _wkv(r, lw, k, v, kk, a, s0, reverse):
    B, T, _ = r.shape
    rows = min(T, WKV_CHUNKS * CHUNK)
    n_steps = T // rows
    width = WKV_PAIRS * LANES
    tmap = (lambda b, p, j: (b, n_steps - 1 - j, p)) if reverse else (lambda b, p, j: (b, j, p))
    seq_spec = pl.BlockSpec((1, rows, width), tmap)
    st_spec = pl.BlockSpec((1, WKV_PAIRS, RWKV_HEAD, LANES), lambda b, p, j: (b, p, 0, 0))
    return pl.pallas_call(
        functools.partial(_wkv_kernel, n_chunks=rows // CHUNK, n_pairs=WKV_PAIRS, reverse=reverse),
        grid=(B, PAIRS // WKV_PAIRS, n_steps),
        in_specs=[seq_spec] * 6 + [st_spec],
        out_specs=[seq_spec, st_spec],
        out_shape=[jax.ShapeDtypeStruct(r.shape, F32), jax.ShapeDtypeStruct((B, PAIRS, RWKV_HEAD, LANES), F32)],
        scratch_shapes=[pltpu.VMEM((WKV_PAIRS, RWKV_HEAD, LANES), F32)],
        compiler_params=_params("parallel", "parallel", "arbitrary"),
    )(r, lw, k, v, kk, a, s0)


def _rwkv_branch(y0_ref, y1_ref, r_ref, v_ref, k0_ref, k1_ref, g_ref, rk_ref, lng_ref, lnb_ref, wo_ref):
    y = y0_ref[0] + y1_ref[0]
    inv_n = 1.0 / RWKV_HEAD
    mean = _seg_sum_wide(y) * inv_n
    yc = y - mean
    var = _seg_sum_wide(yc * yc) * inv_n
    yn = yc * lax.rsqrt(var + GN_EPS) * lng_ref[...] + lnb_ref[...]
    f32 = lambda ref: ref[0].astype(F32)
    bonus = _seg_sum_wide(f32(r_ref) * (f32(k0_ref) + f32(k1_ref)) * rk_ref[...]) * f32(v_ref)
    return jnp.dot(((yn + bonus) * f32(g_ref)).astype(BF16), wo_ref[...], preferred_element_type=F32)


def _rwkv_branch_specs():
    vec = _const_spec((1, D_MODEL))
    return [_row_spec] * 7 + [vec, vec, vec, _const_spec((D_MODEL, D_MODEL))]


HALO = SUBLANES


def _pool_branch(x_ref, xp_ref, xn_ref, g_ref, mod_ref, w_ref, sc_ref, *, seq_len):
    tm = x_ref.shape[1]
    i = pl.program_id(1)
    last = pl.num_programs(1) - 1
    g = g_ref[...]
    mod = mod_ref[0]
    h = _norm_mod(x_ref[0], g, mod, 0)
    hp = _norm_mod(xp_ref[0], g, mod, 0) * jnp.where(i == 0, 0.0, 1.0)
    hn = _norm_mod(xn_ref[0], g, mod, 0) * jnp.where(i == last, 0.0, 1.0)
    ext = jnp.concatenate([hp, h, hn], axis=0)
    n_ext = tm + 2 * HALO
    t = i * tm + lax.broadcasted_iota(jnp.int32, (tm, 1), 0)
    outs = []
    for gi, win in enumerate(POOL_WINDOWS):
        e = ext[:, gi * POOL_GROUP:(gi + 1) * POOL_GROUP]
        acc = e + pltpu.roll(e, 1, 0)
        step = 1
        while 2 * step < win:
            acc = pltpu.roll(acc, step, 0) + pltpu.roll(acc, n_ext - step, 0)
            step *= 2
        half = win // 2
        cnt = (jnp.minimum(t + half, seq_len) - jnp.maximum(t - half, 0)).astype(F32)
        pooled = acc[HALO:HALO + tm] / cnt - e[HALO:HALO + tm]
        outs.append(jnp.dot(pooled.astype(BF16), w_ref[gi], preferred_element_type=F32))
    return jnp.concatenate(outs, axis=1) * sc_ref[...]


def _pool_branch_specs(seq_len):
    n_blk = seq_len // HALO
    prev_spec = lambda tm: pl.BlockSpec((1, HALO, D_MODEL), lambda b, i: (b, jnp.maximum(i * (tm // HALO) - 1, 0), 0))
    next_spec = lambda tm: pl.BlockSpec(
        (1, HALO, D_MODEL), lambda b, i: (b, jnp.minimum((i + 1) * (tm // HALO), n_blk - 1), 0))
    return [_row_spec, prev_spec, next_spec, _const_spec((1, D_MODEL)), lambda tm: _mod_spec(),
            _const_spec((len(POOL_WINDOWS), POOL_GROUP, POOL_GROUP)), _const_spec((1, D_MODEL))]


def _rope_tables(n_tokens):
    rows = n_tokens // GRID_W
    n_freq = HEAD_DIM // 4
    inv = ROPE_THETA ** (-jnp.arange(n_freq, dtype=F32) / n_freq)
    ang_r = jnp.arange(rows, dtype=F32)[:, None] * inv
    ang_c = jnp.arange(GRID_W, dtype=F32)[:, None] * inv
    ang = jnp.concatenate([
        jnp.broadcast_to(ang_r[:, None, :], (rows, GRID_W, n_freq)),
        jnp.broadcast_to(ang_c[None, :, :], (rows, GRID_W, n_freq))], axis=-1).reshape(rows * GRID_W, 2 * n_freq)
    cos, sin = jnp.cos(ang), jnp.sin(ang)
    return jnp.tile(cos, (1, 4)), jnp.tile(sin, (1, 4))


def kernel(x, c, ctx, c_ctx, w_mod, b_mod, norm1_g, norm2_g, mlp_w_in, mlp_w_out, attn_w_qkv, attn_q_gain, attn_k_gain, attn_w_o, rwkv_mu, rwkv_w_rkv, rwkv_w0, rwkv_w1, rwkv_w2, rwkv_a0, rwkv_a1, rwkv_a2, rwkv_g1, rwkv_g2, rwkv_k_k, rwkv_k_a, rwkv_r_k, rwkv_ln_g, rwkv_ln_b, rwkv_w_o, pool_w, pool_scale):
    B, S, _ = x.shape
    L = ctx.shape[1]
    depth = w_mod.shape[0]
    assert x.shape[2] == D_MODEL and S % (4 * CHUNK) == 0 and L % CHUNK == 0 and S % GRID_W == 0

    n_rows = -(-(B + 1) // SUBLANES) * SUBLANES
    cvec = jnp.concatenate([c, c_ctx[None], jnp.zeros((n_rows - B - 1, D_MODEL), F32)], axis=0)
    mod_all = _modulation(cvec, w_mod, b_mod).reshape(depth, n_rows, N_MOD, D_MODEL)
    cos_t, sin_t = _rope_tables(S)
    zero_tab = jnp.zeros((L, LANES), F32)
    row = lambda a: a.reshape(1, -1)

    for i in range(depth):
        last = i == depth - 1
        j = i // N_MIXERS
        mod_l = mod_all[i, :B]
        mod_c = jnp.broadcast_to(mod_all[i, B][None], (B, N_MOD, D_MODEL))
        g1 = row(norm1_g[i])
        kind = i % N_MIXERS
        if kind == 0:
            w_qkv = attn_w_qkv[j].astype(BF16)
            w_o = attn_w_o[j].astype(BF16)
            qg = jnp.tile(row(attn_q_gain[j]), (1, 2))
            kg = jnp.tile(row(attn_k_gain[j]), (1, 2))
            q_l, k_l, v_l = _qkv_project(x, g1, mod_l, w_qkv, qg, kg, cos_t, sin_t, True)
            q_c, k_c, v_c = _qkv_project(ctx, g1, mod_c, w_qkv, qg, kg, zero_tab, zero_tab, False)
            proj_specs = [_row_spec, _const_spec((D_MODEL, D_MODEL))]
            mix_l = (_proj_branch, (_attention(q_l, [(k_l, v_l), (k_c, v_c)]), w_o), proj_specs, 512)
            if not last:
                mix_c = (_proj_branch, (_attention(q_c, [(k_c, v_c)]), w_o), proj_specs, 512)
        elif kind == 1:
            p = {
                "mu": rwkv_mu[j], "w_rkv": rwkv_w_rkv[j].astype(BF16),
                "w1": jnp.concatenate([rwkv_w1[j, 0], rwkv_w1[j, 1]], axis=1).astype(BF16),
                "w2": (0.5 * rwkv_w2[j]).astype(BF16), "w0": 0.5 * rwkv_w0[j],
                "a1": jnp.concatenate([rwkv_a1[j, 0], rwkv_a1[j, 1]], axis=1).astype(BF16),
                "a2": (0.5 * rwkv_a2[j]).astype(BF16), "a0": 0.5 * rwkv_a0[j],
                "g1": rwkv_g1[j].astype(BF16), "g2": rwkv_g2[j].astype(BF16),
                "k_k": row(rwkv_k_k[j]), "k_ac": row(1.0 - 0.5 * rwkv_k_a[j]), "k_ah": row(0.5 * rwkv_k_a[j]), "r_k": row(rwkv_r_k[j]),
                "ln_g": row(rwkv_ln_g[j]), "ln_b": row(rwkv_ln_b[j]), "w_o": rwkv_w_o[j].astype(BF16),
            }
            r_l, v_l, kk_l, g_l, lw0_l, lw1_l, key0_l, key1_l, a0_l, a1_l = _rwkv_prepare(x, g1, mod_l, p)
            r_c, v_c, kk_c, g_c, lw0_c, lw1_c, key0_c, key1_c, a0_c, a1_c = _rwkv_prepare(ctx, g1, mod_c, p)
            zero_state = jnp.zeros((B, PAIRS, RWKV_HEAD, LANES), F32)
            ys_l, ys_c = [], []
            for rev, (lw_l, key_l, a_l, lw_c, key_c, a_c) in enumerate(
                    ((lw0_l, key0_l, a0_l, lw0_c, key0_c, a0_c), (lw1_l, key1_l, a1_l, lw1_c, key1_c, a1_c))):
                y_c, s_ctx = _wkv(r_c, lw_c, key_c, v_c, kk_c, a_c, zero_state, bool(rev))
                y_l, _ = _wkv(r_l, lw_l, key_l, v_l, kk_l, a_l, s_ctx, bool(rev))
                ys_l.append(y_l)
                ys_c.append(y_c)
            consts = (p["r_k"], p["ln_g"], p["ln_b"], p["w_o"])
            mix_l = (_rwkv_branch, (ys_l[0], ys_l[1], r_l, v_l, key0_l, key1_l, g_l) + consts, _rwkv_branch_specs(), 512)
            if not last:
                mix_c = (_rwkv_branch, (ys_c[0], ys_c[1], r_c, v_c, key0_c, key1_c, g_c) + consts,
                         _rwkv_branch_specs(), 256)
        else:
            w_p = pool_w[j].astype(BF16)
            sc = row(pool_scale[j])
            mix_l = (functools.partial(_pool_branch, seq_len=S), (x, x, x, g1, mod_l, w_p, sc), _pool_branch_specs(S), 512)
            if not last:
                mix_c = (functools.partial(_pool_branch, seq_len=L), (ctx, ctx, ctx, g1, mod_c, w_p, sc),
                         _pool_branch_specs(L), 256)
        g2 = row(norm2_g[i])
        w_in = mlp_w_in[i].astype(BF16)
        w_out = mlp_w_out[i].astype(BF16)
        x = _mixer_mlp(x, mod_l, g2, w_in, w_out, *mix_l)
        if not last:
            ctx = _mixer_mlp(ctx, mod_c, g2, w_in, w_out, *mix_c)
    return x
```

```python
import functools

import jax
import jax.numpy as jnp
import numpy as np
from jax import lax
from jax.experimental import pallas as pl
from jax.experimental.pallas import tpu as pltpu

F32 = jnp.float32
BF16 = jnp.bfloat16

D_MODEL = 1024
GRID_W = 64
N_MIXERS = 3
N_MOD = 6
EPS = 1e-6
N_HEADS = 16
N_KV_HEADS = 4
HEAD_DIM = 64
GQA_REP = N_HEADS // N_KV_HEADS
Q_WIDTH = N_HEADS * HEAD_DIM
KV_WIDTH = N_KV_HEADS * HEAD_DIM
QKV_WIDTH = Q_WIDTH + 2 * KV_WIDTH
ROPE_THETA = 10000.0
RWKV_HEAD = 64
DECAY_LORA = 64
ICLR_LORA = 64
GATE_LORA = 160
GN_EPS = RWKV_HEAD * 1e-5
POOL_WINDOWS = (2, 4, 8, 16)
POOL_GROUP = D_MODEL // len(POOL_WINDOWS)
D_FF = 4 * D_MODEL

LANES = 128
SUBLANES = 8
PAIRS = D_MODEL // LANES
CHUNK = 64
INV_BLOCK = 16
VMEM_LIMIT = 56 * 1024 * 1024

HIGHEST = lax.Precision.HIGHEST


def _params(*sem):
    return pltpu.CompilerParams(dimension_semantics=sem, vmem_limit_bytes=VMEM_LIMIT)


def _const_spec(shape):
    zeros = (0,) * len(shape)
    return pl.BlockSpec(shape, lambda *_: zeros, pipeline_mode=pl.Buffered(1))


def _row_spec(tm, width=D_MODEL):
    return pl.BlockSpec((1, tm, width), lambda b, i: (b, i, 0))


def _mod_spec():
    return pl.BlockSpec((1, N_MOD, D_MODEL), lambda b, i: (b, 0, 0))


def _norm_mod(x, g, mod, k):
    y = x * lax.rsqrt(jnp.mean(x * x, axis=-1, keepdims=True) + EPS) * g
    return y * (1.0 + mod[k + 1:k + 2]) + mod[k:k + 1]


def _seg_sum(x):
    i = lax.broadcasted_iota(jnp.int32, (LANES, LANES), 0) // HEAD_DIM
    j = lax.broadcasted_iota(jnp.int32, (LANES, LANES), 1) // HEAD_DIM
    return jnp.dot(x.astype(BF16), jnp.where(i == j, 1.0, 0.0).astype(BF16), preferred_element_type=F32)


def _sigmoid(z):
    return 0.5 * jnp.tanh(0.5 * z) + 0.5


def _seg_sum_wide(x):
    return jnp.concatenate([_seg_sum(x[:, p * LANES:(p + 1) * LANES]) for p in range(x.shape[1] // LANES)], axis=1)


def _mod_kernel(c_ref, w_ref, b_ref, o_ref):
    c = c_ref[...]
    s = c * jax.nn.sigmoid(c)
    o_ref[0] = jnp.dot(s, w_ref[0], precision=HIGHEST, preferred_element_type=F32) + b_ref[0]


def _modulation(cvec, w_mod, b_mod):
    depth = w_mod.shape[0]
    rows = cvec.shape[0]
    tn = 1536
    return pl.pallas_call(
        _mod_kernel,
        grid=(depth, N_MOD * D_MODEL // tn),
        in_specs=[pl.BlockSpec((rows, D_MODEL), lambda l, j: (0, 0)),
                  pl.BlockSpec((1, D_MODEL, tn), lambda l, j: (l, 0, j)),
                  pl.BlockSpec((1, 1, tn), lambda l, j: (l, 0, j))],
        out_specs=pl.BlockSpec((1, rows, tn), lambda l, j: (l, 0, j)),
        out_shape=jax.ShapeDtypeStruct((depth, rows, N_MOD * D_MODEL), F32),
        compiler_params=_params("parallel", "parallel"),
    )(cvec, w_mod, b_mod.reshape(depth, 1, N_MOD * D_MODEL))


VT_ROWS = HEAD_DIM + 16
Q_SCALE = float(HEAD_DIM ** -0.5 * np.log2(np.e))


def _qkv_kernel(x_ref, g_ref, mod_ref, w_ref, qg_ref, kg_ref, cos_ref, sin_ref, q_ref, k_ref, v_ref, *, use_rope):
    h = _norm_mod(x_ref[0], g_ref[...], mod_ref[0], 0).astype(BF16)
    acc = jnp.dot(h, w_ref[...], preferred_element_type=F32)
    src_lane = lax.broadcasted_iota(jnp.int32, (2 * LANES, LANES), 0) % LANES
    dst_lane = lax.broadcasted_iota(jnp.int32, (2 * LANES, LANES), 1)
    half = HEAD_DIM // 2
    first_half = (dst_lane % HEAD_DIM) < half
    rot_mat = jnp.where(first_half & (src_lane == dst_lane + half), -1.0,
                        jnp.where(jnp.logical_not(first_half) & (src_lane == dst_lane - half), 1.0, 0.0)).astype(BF16)

    def head_pair(xp, gain, scale):
        y = xp * lax.rsqrt(_seg_sum(xp * xp) * (1.0 / HEAD_DIM) + EPS) * gain
        if use_rope:
            y_hi = y.astype(BF16)
            y_lo = (y - y_hi.astype(F32)).astype(BF16)
            rot = jnp.dot(jnp.concatenate([y_hi, y_lo], axis=1), rot_mat, preferred_element_type=F32)
            y = y * cos_ref[...] + rot * sin_ref[...]
        return (y * scale).astype(BF16)

    for p in range(Q_WIDTH // LANES):
        y = head_pair(acc[:, p * LANES:(p + 1) * LANES], qg_ref[...], Q_SCALE)
        q_ref[0, 2 * p] = y[:, :HEAD_DIM]
        q_ref[0, 2 * p + 1] = y[:, HEAD_DIM:]
    for p in range(KV_WIDTH // LANES):
        y = head_pair(acc[:, Q_WIDTH + p * LANES:Q_WIDTH + (p + 1) * LANES], kg_ref[...], 1.0)
        k_ref[0, 2 * p] = y[:, :HEAD_DIM]
        k_ref[0, 2 * p + 1] = y[:, HEAD_DIM:]
    ones = jnp.ones((VT_ROWS - HEAD_DIM, acc.shape[0]), F32)
    for p in range(KV_WIDTH // LANES):
        c0 = Q_WIDTH + KV_WIDTH + p * LANES
        vt = acc[:, c0:c0 + LANES].T
        v_ref[0, 2 * p] = jnp.concatenate([vt[:HEAD_DIM], ones], axis=0).astype(BF16)
        v_ref[0, 2 * p + 1] = jnp.concatenate([vt[HEAD_DIM:], ones], axis=0).astype(BF16)


def _qkv_project(x, g, mod, w_bf, qg, kg, cos_t, sin_t, use_rope):
    B, T, _ = x.shape
    tm = min(T, 1024)
    tab_spec = pl.BlockSpec((tm, LANES), lambda b, i: (i, 0))
    head_spec = lambda n: pl.BlockSpec((1, n, tm, HEAD_DIM), lambda b, i: (b, 0, i, 0))
    return pl.pallas_call(
        functools.partial(_qkv_kernel, use_rope=use_rope),
        grid=(B, T // tm),
        in_specs=[_row_spec(tm), _const_spec((1, D_MODEL)), _mod_spec(), _const_spec((D_MODEL, QKV_WIDTH)),
                  _const_spec((1, LANES)), _const_spec((1, LANES)), tab_spec, tab_spec],
        out_specs=[head_spec(N_HEADS), head_spec(N_KV_HEADS),
                   pl.BlockSpec((1, N_KV_HEADS, VT_ROWS, tm), lambda b, i: (b, 0, 0, i))],
        out_shape=[jax.ShapeDtypeStruct((B, N_HEADS, T, HEAD_DIM), BF16),
                   jax.ShapeDtypeStruct((B, N_KV_HEADS, T, HEAD_DIM), BF16),
                   jax.ShapeDtypeStruct((B, N_KV_HEADS, VT_ROWS, T), BF16)],
        compiler_params=_params("parallel", "parallel"),
    )(x, g, mod, w_bf, qg, kg, cos_t, sin_t)


def _attn_kernel(*refs, n_src, tq):
    q_ref = refs[0]
    kv_refs = refs[1:1 + 2 * n_src]
    o_ref = refs[1 + 2 * n_src]
    s_scr, m_scr = refs[2 + 2 * n_src:]
    nq = GQA_REP * tq
    i = pl.program_id(0)
    pieces = []
    row = 0
    for s in range(n_src):
        keys_total = kv_refs[2 * s].shape[2]
        ck = min(ATTN_KV_CHUNK, keys_total)
        for r0 in range(0, keys_total, ck):
            pieces.append((kv_refs[2 * s], kv_refs[2 * s + 1], r0, ck, row))
            row += ck

    def run(with_pass2):
        q = q_ref[0].reshape(nq, HEAD_DIM)
        m8 = None
        if with_pass2:
            m_prev = m_scr[...]
            acc = jnp.zeros((VT_ROWS, nq), F32)
        for k_ref, vt_ref, r0, ck, c0 in pieces:
            if with_pass2:
                p = jnp.exp2(s_scr[c0:c0 + ck, :] - m_prev).astype(BF16)
                acc = acc + jnp.dot(vt_ref[0, 0, :, r0:r0 + ck], p, preferred_element_type=F32)
            s_blk = lax.dot_general(k_ref[0, 0, r0:r0 + ck, :], q, (((1,), (1,)), ((), ())),
                                    preferred_element_type=F32)
            s_scr[c0:c0 + ck, :] = s_blk
            bm = jnp.max(s_blk.reshape(ck // SUBLANES, SUBLANES, nq), axis=0)
            m8 = bm if m8 is None else jnp.maximum(m8, bm)
        m_scr[...] = jnp.max(m8, axis=0, keepdims=True)
        if with_pass2:
            o = (acc[:HEAD_DIM] / acc[HEAD_DIM:HEAD_DIM + 1]).T
            o_ref[0] = jnp.concatenate([o[h * tq:(h + 1) * tq] for h in range(GQA_REP)], axis=1).astype(BF16)

    pl.when(i == 0)(lambda: run(False))
    pl.when(i > 0)(lambda: run(True))


ATTN_KV_CHUNK = 256


def _attention(q, kv_sources):
    B, _, T, _ = q.shape
    tq = min(T, 512)
    n_tiles = T // tq
    n_all = B * N_KV_HEADS * n_tiles
    total = sum(k.shape[2] for k, _ in kv_sources)

    def tile(t):
        t = jnp.clip(t, 0, n_all - 1)
        return t // (N_KV_HEADS * n_tiles), (t // n_tiles) % N_KV_HEADS, t % n_tiles

    def q_map(t):
        b, g, i = tile(t)
        return b, g, i, 0

    def k_map(t):
        b, g, _ = tile(t)
        return b, g, 0, 0

    def v_map(t):
        b, g, _ = tile(t - 1)
        return b, g, 0, 0

    def o_map(t):
        b, g, i = tile(t - 1)
        return b, i, g

    in_specs = [pl.BlockSpec((1, GQA_REP, tq, HEAD_DIM), q_map)]
    args = [q]
    for k, vt in kv_sources:
        in_specs += [pl.BlockSpec((1, 1, k.shape[2], HEAD_DIM), k_map),
                     pl.BlockSpec((1, 1, VT_ROWS, k.shape[2]), v_map)]
        args += [k, vt]
    return pl.pallas_call(
        functools.partial(_attn_kernel, n_src=len(kv_sources), tq=tq),
        grid=(n_all + 1,),
        in_specs=in_specs,
        out_specs=pl.BlockSpec((1, tq, GQA_REP * HEAD_DIM), o_map),
        out_shape=jax.ShapeDtypeStruct((B, T, Q_WIDTH), BF16),
        scratch_shapes=[pltpu.VMEM((total, GQA_REP * tq), F32), pltpu.VMEM((1, GQA_REP * tq), F32)],
        compiler_params=_params("arbitrary"),
    )(*args)


def _proj_branch(y_ref, w_ref):
    return jnp.dot(y_ref[0], w_ref[...], preferred_element_type=F32)


def _mixer_mlp_kernel(*refs, branch, ff_chunk):
    x_ref, mod_ref, g_ref, win_ref, wout_ref = refs[:5]
    o_ref = refs[-1]
    x = x_ref[0]
    mod = mod_ref[0]
    if branch is not None:
        x = x + mod[2:3] * branch(*refs[5:-1])
    h = _norm_mod(x, g_ref[...], mod, 3).astype(BF16)
    acc = jnp.zeros(x.shape, F32)
    for f0 in range(0, D_FF, ff_chunk):
        u = jnp.maximum(jnp.dot(h, win_ref[:, f0:f0 + ff_chunk], preferred_element_type=F32), 0.0)
        acc = acc + jnp.dot((u * u).astype(BF16), wout_ref[f0:f0 + ff_chunk, :], preferred_element_type=F32)
    o_ref[0] = x + mod[5:6] * acc


def _mixer_mlp(x, mod, g, win_bf, wout_bf, branch=None, branch_args=(), branch_specs=(), tm_max=512):
    B, T, _ = x.shape
    tm = min(T, tm_max)
    specs = [s(tm) if callable(s) else s for s in branch_specs]
    return pl.pallas_call(
        functools.partial(_mixer_mlp_kernel, branch=branch, ff_chunk=1024),
        grid=(B, T // tm),
        in_specs=[_row_spec(tm), _mod_spec(), _const_spec((1, D_MODEL)),
                  _const_spec((D_MODEL, D_FF)), _const_spec((D_FF, D_MODEL))] + specs,
        out_specs=_row_spec(tm),
        out_shape=jax.ShapeDtypeStruct(x.shape, F32),
        compiler_params=_params("parallel", "parallel"),
    )(x, mod, g, win_bf, wout_bf, *branch_args)


DECAY_SCALE = float(np.exp(-0.5))


def _rwkv_prep_kernel(x_ref, xp_ref, xn_ref, g_ref, mod_ref, mu_ref, wrkv_ref, w1_ref, w2_ref, w0_ref, a1_ref, a2_ref,
                      a0_ref, g1_ref, g2_ref, kk_ref, kac_ref, kah_ref,
                      r_o, v_o, kk_o, g_o, lw0_o, lw1_o, key0_o, key1_o, a0_o, a1_o, *, tm):
    i = pl.program_id(1)
    last = pl.num_programs(1) - 1
    g = g_ref[...]
    mod = mod_ref[0]
    h = _norm_mod(x_ref[0], g, mod, 0)
    hp = _norm_mod(xp_ref[0][SUBLANES - 1:SUBLANES], g, mod, 0) * jnp.where(i == 0, 0.0, 1.0)
    hn = _norm_mod(xn_ref[0][0:1], g, mod, 0) * jnp.where(i == last, 0.0, 1.0)
    row = lax.broadcasted_iota(jnp.int32, (tm, 1), 0)
    prev = jnp.where(row == 0, hp, pltpu.roll(h, 1, 0))
    nxt = jnp.where(row == tm - 1, hn, pltpu.roll(h, tm - 1, 0))
    xx = 0.5 * (prev + nxt) - h
    mix = lambda m: (h + xx * mu_ref[m:m + 1]).astype(BF16)
    dot = functools.partial(jnp.dot, preferred_element_type=F32)
    r = dot(mix(0), wrkv_ref[0])
    k = dot(mix(2), wrkv_ref[1])
    v = dot(mix(3), wrkv_ref[2])
    r_o[0] = r.astype(BF16)
    v_o[0] = v.astype(BF16)
    g_o[0] = dot(_sigmoid(dot(mix(5), g1_ref[...])).astype(BF16), g2_ref[...]).astype(BF16)
    kk = k * kk_ref[...]
    kk_o[0] = (kk * lax.rsqrt(jnp.maximum(_seg_sum_wide(kk * kk), 1e-24))).astype(BF16)
    tw = jnp.tanh(dot(mix(1), w1_ref[...])).astype(BF16)
    ta = dot(mix(4), a1_ref[...]).astype(BF16)
    for d, (lw_o, key_o, a_o) in enumerate(((lw0_o, key0_o, a0_o), (lw1_o, key1_o, a1_o))):
        th = jnp.tanh(w0_ref[d:d + 1] + dot(tw[:, d * DECAY_LORA:(d + 1) * DECAY_LORA], w2_ref[d]))
        lw_o[0] = (-0.5 * DECAY_SCALE) * th + (-0.5 * DECAY_SCALE)
        ah = jnp.tanh(a0_ref[d:d + 1] + dot(ta[:, d * ICLR_LORA:(d + 1) * ICLR_LORA], a2_ref[d]))
        a_o[0] = (0.5 * ah + 0.5).astype(BF16)
        key_o[0] = (k * (kac_ref[...] + kah_ref[...] * ah)).astype(BF16)


def _rwkv_prepare(x, g, mod, p):
    B, T, _ = x.shape
    tm = min(T, 512)
    nb = tm // SUBLANES
    n_blk8 = T // SUBLANES
    prev_spec = pl.BlockSpec((1, SUBLANES, D_MODEL), lambda b, i: (b, jnp.maximum(i * nb - 1, 0), 0))
    next_spec = pl.BlockSpec((1, SUBLANES, D_MODEL), lambda b, i: (b, jnp.minimum((i + 1) * nb, n_blk8 - 1), 0))
    o_bf, o_f32 = jax.ShapeDtypeStruct(x.shape, BF16), jax.ShapeDtypeStruct(x.shape, F32)
    return pl.pallas_call(
        functools.partial(_rwkv_prep_kernel, tm=tm),
        grid=(B, T // tm),
        in_specs=[_row_spec(tm), prev_spec, next_spec, _const_spec((1, D_MODEL)), _mod_spec(),
                  _const_spec((6, D_MODEL)), _const_spec((3, D_MODEL, D_MODEL)),
                  _const_spec((D_MODEL, 2 * DECAY_LORA)), _const_spec((2, DECAY_LORA, D_MODEL)),
                  _const_spec((2, D_MODEL)),
                  _const_spec((D_MODEL, 2 * ICLR_LORA)), _const_spec((2, ICLR_LORA, D_MODEL)),
                  _const_spec((2, D_MODEL)),
                  _const_spec((D_MODEL, GATE_LORA)), _const_spec((GATE_LORA, D_MODEL)),
                  _const_spec((1, D_MODEL)), _const_spec((1, D_MODEL)), _const_spec((1, D_MODEL))],
        out_specs=[_row_spec(tm)] * 10,
        out_shape=[o_bf, o_bf, o_bf, o_bf, o_f32, o_f32, o_bf, o_bf, o_bf, o_bf],
        compiler_params=_params("parallel", "parallel"),
    )(x, x, x, g, mod, p["mu"], p["w_rkv"], p["w1"], p["w2"], p["w0"], p["a1"], p["a2"], p["a0"], p["g1"], p["g2"],
      p["k_k"], p["k_ac"], p["k_ah"])


def _wkv_kernel(r_ref, lw_ref, k_ref, v_ref, kk_ref, a_ref, s0_ref, y_ref, s_ref, state, *, n_chunks, n_pairs,
                reverse):
    C = CHUNK

    def mm(dims):
        return lambda a, b: lax.dot_general(a.astype(BF16), b.astype(BF16), dims, preferred_element_type=F32)

    hp = mm((((1,), (0,)), ((), ())))
    nt = mm((((1,), (1,)), ((), ())))
    tn = mm((((0,), (0,)), ((), ())))

    @pl.when(pl.program_id(2) == 0)
    def _():
        state[...] = s0_ref[0]

    pi = lax.broadcasted_iota(jnp.int32, (C, LANES), 0)
    pj = lax.broadcasted_iota(jnp.int32, (C, LANES), 1) % C
    strict = (pi < pj) if reverse else (pi > pj)
    incl = (pi <= pj) if reverse else (pi >= pj)
    blk = (pi // INV_BLOCK) == (pj // INV_BLOCK)
    eye = pi == pj
    ident = jnp.where(eye, 1.0, 0.0)
    head0 = lax.broadcasted_iota(jnp.int32, (1, LANES), 1) < RWKV_HEAD

    def stack(z):
        zb = z.astype(BF16)
        zero = jnp.zeros_like(zb)
        return jnp.concatenate([jnp.where(head0, zb, zero), jnp.where(head0, zero, zb)], axis=0)

    side = lambda x, y: jnp.concatenate([x, y], axis=1)
    diag_blocks = lambda z: jnp.where(head0, z[:C], z[C:])

    each = lambda f, *ls: [f(*xs) for xs in zip(*ls)]
    order = range(n_chunks - 1, -1, -1) if reverse else range(n_chunks)

    def cumsum(x):
        step = 1
        while step < C:
            if reverse:
                x = x + jnp.where(pi < C - step, pltpu.roll(x, C - step, 0), 0.0)
            else:
                x = x + jnp.where(pi >= step, pltpu.roll(x, step, 0), 0.0)
            step *= 2
        return x

    def chain(pairs):
        units = [(slice(c * C, (c + 1) * C), slice(p * LANES, (p + 1) * LANES)) for p in pairs for c in range(n_chunks)]
        load = lambda ref: [ref[0, rows, cols].astype(F32) for rows, cols in units]
        r, lw, k, v, kk, a = (load(ref) for ref in (r_ref, lw_ref, k_ref, v_ref, kk_ref, a_ref))
        cum = each(cumsum, lw)
        total = each(lambda z: z[0:1] if reverse else z[C - 1:C], cum)
        g_in = each(lambda z: jnp.exp(-z), cum)
        g_to_end = each(lambda t, z: jnp.exp(t - z), total, cum)
        b = each(lambda x, y: x * y, kk, a)
        a_u = each(lambda x, z, l: x * jnp.exp(z - l), kk, cum, lw)
        r_u = each(lambda x, z: x * jnp.exp(z), r, cum)
        b_s = each(lambda x, y: stack(x * y), b, g_in)
        k_s = each(lambda x, y: stack(x * y), k, g_in)
        v_s = each(stack, v)
        bh = each(lambda x, y: x * y, b, g_to_end)
        kh = each(lambda x, y: x * y, k, g_to_end)
        big = each(lambda a_, r_, b_, k_: nt(jnp.concatenate([a_, r_], axis=0), jnp.concatenate([b_, k_], axis=0)),
                   a_u, r_u, b_s, k_s)
        l_ab = each(lambda z: jnp.where(strict, z[:C, :LANES], 0.0), big)
        l_ak = each(lambda z: jnp.where(strict, z[:C, LANES:], 0.0), big)
        p_rb = each(lambda z: jnp.where(incl, z[C:, :LANES], 0.0), big)
        p_rk = each(lambda z: jnp.where(incl, z[C:, LANES:], 0.0), big)
        l_d = each(lambda z: jnp.where(blk, z, 0.0), l_ab)
        l_o = each(lambda x, y: x - y, l_ab, l_d)
        l2 = each(lambda d: hp(d, stack(d)), l_d)
        lpv = each(lambda x, y, vs: hp(jnp.concatenate([x, y], axis=0), vs), l_ak, p_rk, v_s)
        lv = each(lambda z: z[:C], lpv)
        prv = each(lambda z: z[C:], lpv)
        khv = each(tn, kh, v)
        l34 = each(lambda d, s2: hp(jnp.concatenate([d, s2], axis=0), stack(s2)), l_d, l2)
        p1 = each(lambda d, s2, s34: ident - d + s2 - s34[:C], l_d, l2, l34)
        x48 = each(lambda p, s34: hp(jnp.concatenate([p, s34[C:]], axis=0), stack(s34[C:])), p1, l34)
        p2 = each(lambda p, x: p + x[:C], p1, x48)
        t_d = each(lambda p, x: p + hp(p, stack(x[C:])), p2, x48)
        m1 = each(lambda t, o: hp(t, stack(o)), t_d, l_o)
        mm_ = each(lambda x, t: hp(x, side(stack(x), stack(t))), m1, t_d)
        y_d = each(lambda t, x: t - x[:, LANES:], t_d, mm_)
        t_inv = each(lambda y, x: y + hp(x[:, :LANES], stack(y)), y_d, mm_)
        w = each(lambda t, x, y: hp(t, side(stack(x), stack(y))), t_inv, a_u, lv)
        pw = each(lambda p, x: hp(p, side(stack(x[:, :LANES]), stack(x[:, LANES:]))), p_rb, w)
        bw = each(tn, bh, w)
        q1 = each(lambda x, y: x - y[:, :LANES], r_u, pw)
        y2 = each(lambda x, y: x - y[:, LANES:], prv, pw)
        g_mat = each(lambda t, y: jnp.where(eye, jnp.exp(t), 0.0) - diag_blocks(y[:, :LANES]), total, bw)
        h_mat = each(lambda x, y: diag_blocks(x) - diag_blocks(y[:, LANES:]), khv, bw)
        s = [state[p] for p in pairs]
        for c in order:
            for i, p in enumerate(pairs):
                u = i * n_chunks + c
                ys = hp(jnp.concatenate([q1[u], g_mat[u]], axis=0), stack(s[i]))
                y_ref[0, c * C:(c + 1) * C, p * LANES:(p + 1) * LANES] = ys[:C] + y2[u]
                s[i] = ys[C:] + h_mat[u]
        for i, p in enumerate(pairs):
            state[p] = s[i]
        return s

    final_state = chain(list(range(n_pairs)))

    @pl.when(pl.program_id(2) == pl.num_programs(2) - 1)
    def _():
        for p in range(n_pairs):
            s_ref[0, p] = final_state[p]


WKV_PAIRS = 4
WKV_CHUNKS = 8


def _wkv(r, lw, k, v, kk, a, s0, reverse):
    B, T, _ = r.shape
    rows = min(T, WKV_CHUNKS * CHUNK)
    n_steps = T // rows
    width = WKV_PAIRS * LANES
    tmap = (lambda b, p, j: (b, n_steps - 1 - j, p)) if reverse else (lambda b, p, j: (b, j, p))
    seq_spec = pl.BlockSpec((1, rows, width), tmap)
    st_spec = pl.BlockSpec((1, WKV_PAIRS, RWKV_HEAD, LANES), lambda b, p, j: (b, p, 0, 0))
    return pl.pallas_call(
        functools.partial(_wkv_kernel, n_chunks=rows // CHUNK, n_pairs=WKV_PAIRS, reverse=reverse),
        grid=(B, PAIRS // WKV_PAIRS, n_steps),
        in_specs=[seq_spec] * 6 + [st_spec],
        out_specs=[seq_spec, st_spec],
        out_shape=[jax.ShapeDtypeStruct(r.shape, F32), jax.ShapeDtypeStruct((B, PAIRS, RWKV_HEAD, LANES), F32)],
        scratch_shapes=[pltpu.VMEM((WKV_PAIRS, RWKV_HEAD, LANES), F32)],
        compiler_params=_params("parallel", "parallel", "arbitrary"),
    )(r, lw, k, v, kk, a, s0)


def _rwkv_branch(y0_ref, y1_ref, r_ref, v_ref, k0_ref, k1_ref, g_ref, rk_ref, lng_ref, lnb_ref, wo_ref):
    y = y0_ref[0] + y1_ref[0]
    inv_n = 1.0 / RWKV_HEAD
    mean = _seg_sum_wide(y) * inv_n
    yc = y - mean
    var = _seg_sum_wide(yc * yc) * inv_n
    yn = yc * lax.rsqrt(var + GN_EPS) * lng_ref[...] + lnb_ref[...]
    f32 = lambda ref: ref[0].astype(F32)
    bonus = _seg_sum_wide(f32(r_ref) * (f32(k0_ref) + f32(k1_ref)) * rk_ref[...]) * f32(v_ref)
    return jnp.dot(((yn + bonus) * f32(g_ref)).astype(BF16), wo_ref[...], preferred_element_type=F32)


def _rwkv_branch_specs():
    vec = _const_spec((1, D_MODEL))
    return [_row_spec] * 7 + [vec, vec, vec, _const_spec((D_MODEL, D_MODEL))]


HALO = SUBLANES


def _pool_branch(x_ref, xp_ref, xn_ref, g_ref, mod_ref, w_ref, sc_ref, *, seq_len):
    tm = x_ref.shape[1]
    i = pl.program_id(1)
    last = pl.num_programs(1) - 1
    g = g_ref[...]
    mod = mod_ref[0]
    h = _norm_mod(x_ref[0], g, mod, 0)
    hp = _norm_mod(xp_ref[0], g, mod, 0) * jnp.where(i == 0, 0.0, 1.0)
    hn = _norm_mod(xn_ref[0], g, mod, 0) * jnp.where(i == last, 0.0, 1.0)
    ext = jnp.concatenate([hp, h, hn], axis=0)
    n_ext = tm + 2 * HALO
    t = i * tm + lax.broadcasted_iota(jnp.int32, (tm, 1), 0)
    outs = []
    for gi, win in enumerate(POOL_WINDOWS):
        e = ext[:, gi * POOL_GROUP:(gi + 1) * POOL_GROUP]
        acc = e + pltpu.roll(e, 1, 0)
        step = 1
        while 2 * step < win:
            acc = pltpu.roll(acc, step, 0) + pltpu.roll(acc, n_ext - step, 0)
            step *= 2
        half = win // 2
        cnt = (jnp.minimum(t + half, seq_len) - jnp.maximum(t - half, 0)).astype(F32)
        pooled = acc[HALO:HALO + tm] / cnt - e[HALO:HALO + tm]
        outs.append(jnp.dot(pooled.astype(BF16), w_ref[gi], preferred_element_type=F32))
    return jnp.concatenate(outs, axis=1) * sc_ref[...]


def _pool_branch_specs(seq_len):
    n_blk = seq_len // HALO
    prev_spec = lambda tm: pl.BlockSpec((1, HALO, D_MODEL), lambda b, i: (b, jnp.maximum(i * (tm // HALO) - 1, 0), 0))
    next_spec = lambda tm: pl.BlockSpec(
        (1, HALO, D_MODEL), lambda b, i: (b, jnp.minimum((i + 1) * (tm // HALO), n_blk - 1), 0))
    return [_row_spec, prev_spec, next_spec, _const_spec((1, D_MODEL)), lambda tm: _mod_spec(),
            _const_spec((len(POOL_WINDOWS), POOL_GROUP, POOL_GROUP)), _const_spec((1, D_MODEL))]


def _rope_tables(n_tokens):
    rows = n_tokens // GRID_W
    n_freq = HEAD_DIM // 4
    inv = ROPE_THETA ** (-jnp.arange(n_freq, dtype=F32) / n_freq)
    ang_r = jnp.arange(rows, dtype=F32)[:, None] * inv
    ang_c = jnp.arange(GRID_W, dtype=F32)[:, None] * inv
    ang = jnp.concatenate([
        jnp.broadcast_to(ang_r[:, None, :], (rows, GRID_W, n_freq)),
        jnp.broadcast_to(ang_c[None, :, :], (rows, GRID_W, n_freq))], axis=-1).reshape(rows * GRID_W, 2 * n_freq)
    cos, sin = jnp.cos(ang), jnp.sin(ang)
    return jnp.tile(cos, (1, 4)), jnp.tile(sin, (1, 4))


def kernel(x, c, ctx, c_ctx, w_mod, b_mod, norm1_g, norm2_g, mlp_w_in, mlp_w_out, attn_w_qkv, attn_q_gain, attn_k_gain, attn_w_o, rwkv_mu, rwkv_w_rkv, rwkv_w0, rwkv_w1, rwkv_w2, rwkv_a0, rwkv_a1, rwkv_a2, rwkv_g1, rwkv_g2, rwkv_k_k, rwkv_k_a, rwkv_r_k, rwkv_ln_g, rwkv_ln_b, rwkv_w_o, pool_w, pool_scale):
    B, S, _ = x.shape
    L = ctx.shape[1]
    depth = w_mod.shape[0]
    assert x.shape[2] == D_MODEL and S % (4 * CHUNK) == 0 and L % CHUNK == 0 and S % GRID_W == 0

    n_rows = -(-(B + 1) // SUBLANES) * SUBLANES
    cvec = jnp.concatenate([c, c_ctx[None], jnp.zeros((n_rows - B - 1, D_MODEL), F32)], axis=0)
    mod_all = _modulation(cvec, w_mod, b_mod).reshape(depth, n_rows, N_MOD, D_MODEL)
    cos_t, sin_t = _rope_tables(S)
    zero_tab = jnp.zeros((L, LANES), F32)
    row = lambda a: a.reshape(1, -1)

    for i in range(depth):
        last = i == depth - 1
        j = i // N_MIXERS
        mod_l = mod_all[i, :B]
        mod_c = jnp.broadcast_to(mod_all[i, B][None], (B, N_MOD, D_MODEL))
        g1 = row(norm1_g[i])
        kind = i % N_MIXERS
        if kind == 0:
            w_qkv = attn_w_qkv[j].astype(BF16)
            w_o = attn_w_o[j].astype(BF16)
            qg = jnp.tile(row(attn_q_gain[j]), (1, 2))
            kg = jnp.tile(row(attn_k_gain[j]), (1, 2))
            q_l, k_l, v_l = _qkv_project(x, g1, mod_l, w_qkv, qg, kg, cos_t, sin_t, True)
            q_c, k_c, v_c = _qkv_project(ctx, g1, mod_c, w_qkv, qg, kg, zero_tab, zero_tab, False)
            proj_specs = [_row_spec, _const_spec((D_MODEL, D_MODEL))]
            mix_l = (_proj_branch, (_attention(q_l, [(k_l, v_l), (k_c, v_c)]), w_o), proj_specs, 512)
            if not last:
                mix_c = (_proj_branch, (_attention(q_c, [(k_c, v_c)]), w_o), proj_specs, 512)
        elif kind == 1:
            p = {
                "mu": rwkv_mu[j], "w_rkv": rwkv_w_rkv[j].astype(BF16),
                "w1": jnp.concatenate([rwkv_w1[j, 0], rwkv_w1[j, 1]], axis=1).astype(BF16),
                "w2": (0.5 * rwkv_w2[j]).astype(BF16), "w0": 0.5 * rwkv_w0[j],
                "a1": jnp.concatenate([rwkv_a1[j, 0], rwkv_a1[j, 1]], axis=1).astype(BF16),
                "a2": (0.5 * rwkv_a2[j]).astype(BF16), "a0": 0.5 * rwkv_a0[j],
                "g1": rwkv_g1[j].astype(BF16), "g2": rwkv_g2[j].astype(BF16),
                "k_k": row(rwkv_k_k[j]), "k_ac": row(1.0 - 0.5 * rwkv_k_a[j]), "k_ah": row(0.5 * rwkv_k_a[j]), "r_k": row(rwkv_r_k[j]),
                "ln_g": row(rwkv_ln_g[j]), "ln_b": row(rwkv_ln_b[j]), "w_o": rwkv_w_o[j].astype(BF16),
            }
            r_l, v_l, kk_l, g_l, lw0_l, lw1_l, key0_l, key1_l, a0_l, a1_l = _rwkv_prepare(x, g1, mod_l, p)
            r_c, v_c, kk_c, g_c, lw0_c, lw1_c, key0_c, key1_c, a0_c, a1_c = _rwkv_prepare(ctx, g1, mod_c, p)
            zero_state = jnp.zeros((B, PAIRS, RWKV_HEAD, LANES), F32)
            ys_l, ys_c = [], []
            for rev, (lw_l, key_l, a_l, lw_c, key_c, a_c) in enumerate(
                    ((lw0_l, key0_l, a0_l, lw0_c, key0_c, a0_c), (lw1_l, key1_l, a1_l, lw1_c, key1_c, a1_c))):
                y_c, s_ctx = _wkv(r_c, lw_c, key_c, v_c, kk_c, a_c, zero_state, bool(rev))
                y_l, _ = _wkv(r_l, lw_l, key_l, v_l, kk_l, a_l, s_ctx, bool(rev))
                ys_l.append(y_l)
                ys_c.append(y_c)
            consts = (p["r_k"], p["ln_g"], p["ln_b"], p["w_o"])
            mix_l = (_rwkv_branch, (ys_l[0], ys_l[1], r_l, v_l, key0_l, key1_l, g_l) + consts, _rwkv_branch_specs(), 512)
            if not last:
                mix_c = (_rwkv_branch, (ys_c[0], ys_c[1], r_c, v_c, key0_c, key1_c, g_c) + consts,
                         _rwkv_branch_specs(), 256)
        else:
            w_p = pool_w[j].astype(BF16)
            sc = row(pool_scale[j])
            mix_l = (functools.partial(_pool_branch, seq_len=S), (x, x, x, g1, mod_l, w_p, sc), _pool_branch_specs(S), 512)
            if not last:
                mix_c = (functools.partial(_pool_branch, seq_len=L), (ctx, ctx, ctx, g1, mod_c, w_p, sc),
                         _pool_branch_specs(L), 256)
        g2 = row(norm2_g[i])
        w_in = mlp_w_in[i].astype(BF16)
        w_out = mlp_w_out[i].astype(BF16)
        x = _mixer_mlp(x, mod_l, g2, w_in, w_out, *mix_l)
        if not last:
            ctx = _mixer_mlp(ctx, mod_c, g2, w_in, w_out, *mix_c)
    return x
```

```python
import functools

import jax
import jax.numpy as jnp
import numpy as np
from jax import lax
from jax.experimental import pallas as pl
from jax.experimental.pallas import tpu as pltpu

F32 = jnp.float32
BF16 = jnp.bfloat16

D_MODEL = 1024
GRID_W = 64
N_MIXERS = 3
N_MOD = 6
EPS = 1e-6
N_HEADS = 16
N_KV_HEADS = 4
HEAD_DIM = 64
GQA_REP = N_HEADS // N_KV_HEADS
Q_WIDTH = N_HEADS * HEAD_DIM
KV_WIDTH = N_KV_HEADS * HEAD_DIM
QKV_WIDTH = Q_WIDTH + 2 * KV_WIDTH
ROPE_THETA = 10000.0
RWKV_HEAD = 64
DECAY_LORA = 64
ICLR_LORA = 64
GATE_LORA = 160
GN_EPS = RWKV_HEAD * 1e-5
POOL_WINDOWS = (2, 4, 8, 16)
POOL_GROUP = D_MODEL // len(POOL_WINDOWS)
D_FF = 4 * D_MODEL

LANES = 128
SUBLANES = 8
PAIRS = D_MODEL // LANES
CHUNK = 64
INV_BLOCK = 8
VMEM_LIMIT = 56 * 1024 * 1024

HIGHEST = lax.Precision.HIGHEST


def _params(*sem):
    return pltpu.CompilerParams(dimension_semantics=sem, vmem_limit_bytes=VMEM_LIMIT)


def _const_spec(shape):
    zeros = (0,) * len(shape)
    return pl.BlockSpec(shape, lambda *_: zeros, pipeline_mode=pl.Buffered(1))


def _row_spec(tm, width=D_MODEL):
    return pl.BlockSpec((1, tm, width), lambda b, i: (b, i, 0))


def _mod_spec():
    return pl.BlockSpec((1, N_MOD, D_MODEL), lambda b, i: (b, 0, 0))


def _norm_mod(x, g, mod, k):
    y = x * lax.rsqrt(jnp.mean(x * x, axis=-1, keepdims=True) + EPS) * g
    return y * (1.0 + mod[k + 1:k + 2]) + mod[k:k + 1]


def _seg_sum(x):
    i = lax.broadcasted_iota(jnp.int32, (LANES, LANES), 0) // HEAD_DIM
    j = lax.broadcasted_iota(jnp.int32, (LANES, LANES), 1) // HEAD_DIM
    return jnp.dot(x.astype(BF16), jnp.where(i == j, 1.0, 0.0).astype(BF16), preferred_element_type=F32)


def _sigmoid(z):
    return 0.5 * jnp.tanh(0.5 * z) + 0.5


def _seg_sum_wide(x):
    return jnp.concatenate([_seg_sum(x[:, p * LANES:(p + 1) * LANES]) for p in range(x.shape[1] // LANES)], axis=1)


def _mod_kernel(c_ref, w_ref, b_ref, o_ref):
    c = c_ref[...]
    s = c * jax.nn.sigmoid(c)
    o_ref[0] = jnp.dot(s, w_ref[0], precision=HIGHEST, preferred_element_type=F32) + b_ref[0]


def _modulation(cvec, w_mod, b_mod):
    depth = w_mod.shape[0]
    rows = cvec.shape[0]
    tn = 1536
    return pl.pallas_call(
        _mod_kernel,
        grid=(depth, N_MOD * D_MODEL // tn),
        in_specs=[pl.BlockSpec((rows, D_MODEL), lambda l, j: (0, 0)),
                  pl.BlockSpec((1, D_MODEL, tn), lambda l, j: (l, 0, j)),
                  pl.BlockSpec((1, 1, tn), lambda l, j: (l, 0, j))],
        out_specs=pl.BlockSpec((1, rows, tn), lambda l, j: (l, 0, j)),
        out_shape=jax.ShapeDtypeStruct((depth, rows, N_MOD * D_MODEL), F32),
        compiler_params=_params("parallel", "parallel"),
    )(cvec, w_mod, b_mod.reshape(depth, 1, N_MOD * D_MODEL))


VT_ROWS = HEAD_DIM + 16
Q_SCALE = float(HEAD_DIM ** -0.5 * np.log2(np.e))


def _qkv_kernel(x_ref, g_ref, mod_ref, w_ref, qg_ref, kg_ref, cos_ref, sin_ref, q_ref, k_ref, v_ref, *, use_rope):
    h = _norm_mod(x_ref[0], g_ref[...], mod_ref[0], 0).astype(BF16)
    acc = jnp.dot(h, w_ref[...], preferred_element_type=F32)
    src_lane = lax.broadcasted_iota(jnp.int32, (2 * LANES, LANES), 0) % LANES
    dst_lane = lax.broadcasted_iota(jnp.int32, (2 * LANES, LANES), 1)
    half = HEAD_DIM // 2
    first_half = (dst_lane % HEAD_DIM) < half
    rot_mat = jnp.where(first_half & (src_lane == dst_lane + half), -1.0,
                        jnp.where(jnp.logical_not(first_half) & (src_lane == dst_lane - half), 1.0, 0.0)).astype(BF16)

    def head_pair(xp, gain, scale):
        y = xp * lax.rsqrt(_seg_sum(xp * xp) * (1.0 / HEAD_DIM) + EPS) * gain
        if use_rope:
            y_hi = y.astype(BF16)
            y_lo = (y - y_hi.astype(F32)).astype(BF16)
            rot = jnp.dot(jnp.concatenate([y_hi, y_lo], axis=1), rot_mat, preferred_element_type=F32)
            y = y * cos_ref[...] + rot * sin_ref[...]
        return (y * scale).astype(BF16)

    for p in range(Q_WIDTH // LANES):
        y = head_pair(acc[:, p * LANES:(p + 1) * LANES], qg_ref[...], Q_SCALE)
        q_ref[0, 2 * p] = y[:, :HEAD_DIM]
        q_ref[0, 2 * p + 1] = y[:, HEAD_DIM:]
    for p in range(KV_WIDTH // LANES):
        y = head_pair(acc[:, Q_WIDTH + p * LANES:Q_WIDTH + (p + 1) * LANES], kg_ref[...], 1.0)
        k_ref[0, 2 * p] = y[:, :HEAD_DIM]
        k_ref[0, 2 * p + 1] = y[:, HEAD_DIM:]
    ones = jnp.ones((VT_ROWS - HEAD_DIM, acc.shape[0]), F32)
    for p in range(KV_WIDTH // LANES):
        c0 = Q_WIDTH + KV_WIDTH + p * LANES
        vt = acc[:, c0:c0 + LANES].T
        v_ref[0, 2 * p] = jnp.concatenate([vt[:HEAD_DIM], ones], axis=0).astype(BF16)
        v_ref[0, 2 * p + 1] = jnp.concatenate([vt[HEAD_DIM:], ones], axis=0).astype(BF16)


def _qkv_project(x, g, mod, w_bf, qg, kg, cos_t, sin_t, use_rope):
    B, T, _ = x.shape
    tm = min(T, 1024)
    tab_spec = pl.BlockSpec((tm, LANES), lambda b, i: (i, 0))
    head_spec = lambda n: pl.BlockSpec((1, n, tm, HEAD_DIM), lambda b, i: (b, 0, i, 0))
    return pl.pallas_call(
        functools.partial(_qkv_kernel, use_rope=use_rope),
        grid=(B, T // tm),
        in_specs=[_row_spec(tm), _const_spec((1, D_MODEL)), _mod_spec(), _const_spec((D_MODEL, QKV_WIDTH)),
                  _const_spec((1, LANES)), _const_spec((1, LANES)), tab_spec, tab_spec],
        out_specs=[head_spec(N_HEADS), head_spec(N_KV_HEADS),
                   pl.BlockSpec((1, N_KV_HEADS, VT_ROWS, tm), lambda b, i: (b, 0, 0, i))],
        out_shape=[jax.ShapeDtypeStruct((B, N_HEADS, T, HEAD_DIM), BF16),
                   jax.ShapeDtypeStruct((B, N_KV_HEADS, T, HEAD_DIM), BF16),
                   jax.ShapeDtypeStruct((B, N_KV_HEADS, VT_ROWS, T), BF16)],
        compiler_params=_params("parallel", "parallel"),
    )(x, g, mod, w_bf, qg, kg, cos_t, sin_t)


def _attn_kernel(*refs, n_src, tq):
    q_ref = refs[0]
    kv_refs = refs[1:1 + 2 * n_src]
    o_ref = refs[1 + 2 * n_src]
    s_scr, m_scr = refs[2 + 2 * n_src:]
    nq = GQA_REP * tq
    i = pl.program_id(0)
    pieces = []
    row = 0
    for s in range(n_src):
        keys_total = kv_refs[2 * s].shape[2]
        ck = min(ATTN_KV_CHUNK, keys_total)
        for r0 in range(0, keys_total, ck):
            pieces.append((kv_refs[2 * s], kv_refs[2 * s + 1], r0, ck, row))
            row += ck

    def run(with_pass2):
        q = q_ref[0].reshape(nq, HEAD_DIM)
        m8 = None
        if with_pass2:
            m_prev = m_scr[...]
            acc = jnp.zeros((VT_ROWS, nq), F32)
        for k_ref, vt_ref, r0, ck, c0 in pieces:
            if with_pass2:
                p = jnp.exp2(s_scr[c0:c0 + ck, :] - m_prev).astype(BF16)
                acc = acc + jnp.dot(vt_ref[0, 0, :, r0:r0 + ck], p, preferred_element_type=F32)
            s_blk = lax.dot_general(k_ref[0, 0, r0:r0 + ck, :], q, (((1,), (1,)), ((), ())),
                                    preferred_element_type=F32)
            s_scr[c0:c0 + ck, :] = s_blk
            bm = jnp.max(s_blk.reshape(ck // SUBLANES, SUBLANES, nq), axis=0)
            m8 = bm if m8 is None else jnp.maximum(m8, bm)
        m_scr[...] = jnp.max(m8, axis=0, keepdims=True)
        if with_pass2:
            o = (acc[:HEAD_DIM] / acc[HEAD_DIM:HEAD_DIM + 1]).T
            o_ref[0] = jnp.concatenate([o[h * tq:(h + 1) * tq] for h in range(GQA_REP)], axis=1).astype(BF16)

    pl.when(i == 0)(lambda: run(False))
    pl.when(i > 0)(lambda: run(True))


ATTN_KV_CHUNK = 256


def _attention(q, kv_sources):
    B, _, T, _ = q.shape
    tq = min(T, 256)
    n_tiles = T // tq
    n_all = B * N_KV_HEADS * n_tiles
    total = sum(k.shape[2] for k, _ in kv_sources)

    def tile(t):
        t = jnp.clip(t, 0, n_all - 1)
        return t // (N_KV_HEADS * n_tiles), (t // n_tiles) % N_KV_HEADS, t % n_tiles

    def q_map(t):
        b, g, i = tile(t)
        return b, g, i, 0

    def k_map(t):
        b, g, _ = tile(t)
        return b, g, 0, 0

    def v_map(t):
        b, g, _ = tile(t - 1)
        return b, g, 0, 0

    def o_map(t):
        b, g, i = tile(t - 1)
        return b, i, g

    in_specs = [pl.BlockSpec((1, GQA_REP, tq, HEAD_DIM), q_map)]
    args = [q]
    for k, vt in kv_sources:
        in_specs += [pl.BlockSpec((1, 1, k.shape[2], HEAD_DIM), k_map),
                     pl.BlockSpec((1, 1, VT_ROWS, k.shape[2]), v_map)]
        args += [k, vt]
    return pl.pallas_call(
        functools.partial(_attn_kernel, n_src=len(kv_sources), tq=tq),
        grid=(n_all + 1,),
        in_specs=in_specs,
        out_specs=pl.BlockSpec((1, tq, GQA_REP * HEAD_DIM), o_map),
        out_shape=jax.ShapeDtypeStruct((B, T, Q_WIDTH), BF16),
        scratch_shapes=[pltpu.VMEM((total, GQA_REP * tq), F32), pltpu.VMEM((1, GQA_REP * tq), F32)],
        compiler_params=_params("arbitrary"),
    )(*args)


def _proj_branch(y_ref, w_ref):
    return jnp.dot(y_ref[0], w_ref[...], preferred_element_type=F32)


def _mixer_mlp_kernel(*refs, branch, ff_chunk):
    x_ref, mod_ref, g_ref, win_ref, wout_ref = refs[:5]
    o_ref = refs[-1]
    x = x_ref[0]
    mod = mod_ref[0]
    if branch is not None:
        x = x + mod[2:3] * branch(*refs[5:-1])
    h = _norm_mod(x, g_ref[...], mod, 3).astype(BF16)
    acc = jnp.zeros(x.shape, F32)
    for f0 in range(0, D_FF, ff_chunk):
        u = jnp.maximum(jnp.dot(h, win_ref[:, f0:f0 + ff_chunk], preferred_element_type=F32), 0.0)
        acc = acc + jnp.dot((u * u).astype(BF16), wout_ref[f0:f0 + ff_chunk, :], preferred_element_type=F32)
    o_ref[0] = x + mod[5:6] * acc


def _mixer_mlp(x, mod, g, win_bf, wout_bf, branch=None, branch_args=(), branch_specs=(), tm_max=512):
    B, T, _ = x.shape
    tm = min(T, tm_max)
    specs = [s(tm) if callable(s) else s for s in branch_specs]
    return pl.pallas_call(
        functools.partial(_mixer_mlp_kernel, branch=branch, ff_chunk=1024),
        grid=(B, T // tm),
        in_specs=[_row_spec(tm), _mod_spec(), _const_spec((1, D_MODEL)),
                  _const_spec((D_MODEL, D_FF)), _const_spec((D_FF, D_MODEL))] + specs,
        out_specs=_row_spec(tm),
        out_shape=jax.ShapeDtypeStruct(x.shape, F32),
        compiler_params=_params("parallel", "parallel"),
    )(x, mod, g, win_bf, wout_bf, *branch_args)


DECAY_SCALE = float(np.exp(-0.5))


def _rwkv_prep_kernel(x_ref, xp_ref, xn_ref, g_ref, mod_ref, mu_ref, wrkv_ref, w1_ref, w2_ref, w0_ref, a1_ref, a2_ref,
                      a0_ref, g1_ref, g2_ref, kk_ref, kac_ref, kah_ref,
                      r_o, v_o, kk_o, g_o, lw0_o, lw1_o, key0_o, key1_o, a0_o, a1_o, *, tm):
    i = pl.program_id(1)
    last = pl.num_programs(1) - 1
    g = g_ref[...]
    mod = mod_ref[0]
    h = _norm_mod(x_ref[0], g, mod, 0)
    hp = _norm_mod(xp_ref[0][SUBLANES - 1:SUBLANES], g, mod, 0) * jnp.where(i == 0, 0.0, 1.0)
    hn = _norm_mod(xn_ref[0][0:1], g, mod, 0) * jnp.where(i == last, 0.0, 1.0)
    row = lax.broadcasted_iota(jnp.int32, (tm, 1), 0)
    prev = jnp.where(row == 0, hp, pltpu.roll(h, 1, 0))
    nxt = jnp.where(row == tm - 1, hn, pltpu.roll(h, tm - 1, 0))
    xx = 0.5 * (prev + nxt) - h
    mix = lambda m: (h + xx * mu_ref[m:m + 1]).astype(BF16)
    dot = functools.partial(jnp.dot, preferred_element_type=F32)
    r = dot(mix(0), wrkv_ref[0])
    k = dot(mix(2), wrkv_ref[1])
    v = dot(mix(3), wrkv_ref[2])
    r_o[0] = r.astype(BF16)
    v_o[0] = v.astype(BF16)
    g_o[0] = dot(_sigmoid(dot(mix(5), g1_ref[...])).astype(BF16), g2_ref[...]).astype(BF16)
    kk = k * kk_ref[...]
    kk_o[0] = (kk * lax.rsqrt(jnp.maximum(_seg_sum_wide(kk * kk), 1e-24))).astype(BF16)
    tw = jnp.tanh(dot(mix(1), w1_ref[...])).astype(BF16)
    ta = dot(mix(4), a1_ref[...]).astype(BF16)
    for d, (lw_o, key_o, a_o) in enumerate(((lw0_o, key0_o, a0_o), (lw1_o, key1_o, a1_o))):
        th = jnp.tanh(w0_ref[d:d + 1] + dot(tw[:, d * DECAY_LORA:(d + 1) * DECAY_LORA], w2_ref[d]))
        lw_o[0] = (-0.5 * DECAY_SCALE) * th + (-0.5 * DECAY_SCALE)
        ah = jnp.tanh(a0_ref[d:d + 1] + dot(ta[:, d * ICLR_LORA:(d + 1) * ICLR_LORA], a2_ref[d]))
        a_o[0] = (0.5 * ah + 0.5).astype(BF16)
        key_o[0] = (k * (kac_ref[...] + kah_ref[...] * ah)).astype(BF16)


def _rwkv_prepare(x, g, mod, p):
    B, T, _ = x.shape
    tm = min(T, 512)
    nb = tm // SUBLANES
    n_blk8 = T // SUBLANES
    prev_spec = pl.BlockSpec((1, SUBLANES, D_MODEL), lambda b, i: (b, jnp.maximum(i * nb - 1, 0), 0))
    next_spec = pl.BlockSpec((1, SUBLANES, D_MODEL), lambda b, i: (b, jnp.minimum((i + 1) * nb, n_blk8 - 1), 0))
    o_bf, o_f32 = jax.ShapeDtypeStruct(x.shape, BF16), jax.ShapeDtypeStruct(x.shape, F32)
    return pl.pallas_call(
        functools.partial(_rwkv_prep_kernel, tm=tm),
        grid=(B, T // tm),
        in_specs=[_row_spec(tm), prev_spec, next_spec, _const_spec((1, D_MODEL)), _mod_spec(),
                  _const_spec((6, D_MODEL)), _const_spec((3, D_MODEL, D_MODEL)),
                  _const_spec((D_MODEL, 2 * DECAY_LORA)), _const_spec((2, DECAY_LORA, D_MODEL)),
                  _const_spec((2, D_MODEL)),
                  _const_spec((D_MODEL, 2 * ICLR_LORA)), _const_spec((2, ICLR_LORA, D_MODEL)),
                  _const_spec((2, D_MODEL)),
                  _const_spec((D_MODEL, GATE_LORA)), _const_spec((GATE_LORA, D_MODEL)),
                  _const_spec((1, D_MODEL)), _const_spec((1, D_MODEL)), _const_spec((1, D_MODEL))],
        out_specs=[_row_spec(tm)] * 10,
        out_shape=[o_bf, o_bf, o_bf, o_bf, o_f32, o_f32, o_bf, o_bf, o_bf, o_bf],
        compiler_params=_params("parallel", "parallel"),
    )(x, x, x, g, mod, p["mu"], p["w_rkv"], p["w1"], p["w2"], p["w0"], p["a1"], p["a2"], p["a0"], p["g1"], p["g2"],
      p["k_k"], p["k_ac"], p["k_ah"])


def _wkv_kernel(r_ref, lw_ref, k_ref, v_ref, kk_ref, a_ref, s0_ref, y_ref, s_ref, state, *, n_chunks, n_pairs,
                reverse):
    C = CHUNK

    def mm(dims):
        return lambda a, b: lax.dot_general(a.astype(BF16), b.astype(BF16), dims, preferred_element_type=F32)

    hp = mm((((1,), (0,)), ((), ())))
    nt = mm((((1,), (1,)), ((), ())))
    tn = mm((((0,), (0,)), ((), ())))

    @pl.when(pl.program_id(2) == 0)
    def _():
        state[...] = s0_ref[0]

    pi = lax.broadcasted_iota(jnp.int32, (C, LANES), 0)
    pj = lax.broadcasted_iota(jnp.int32, (C, LANES), 1) % C
    strict = (pi < pj) if reverse else (pi > pj)
    incl = (pi <= pj) if reverse else (pi >= pj)
    eye = pi == pj
    ident = jnp.where(eye, 1.0, 0.0)
    head0 = lax.broadcasted_iota(jnp.int32, (1, LANES), 1) < RWKV_HEAD

    def stack(z):
        zb = z.astype(BF16)
        zero = jnp.zeros_like(zb)
        return jnp.concatenate([jnp.where(head0, zb, zero), jnp.where(head0, zero, zb)], axis=0)

    side = lambda x, y: jnp.concatenate([x, y], axis=1)
    diag_blocks = lambda z: jnp.where(head0, z[:C], z[C:])

    each = lambda f, *ls: [f(*xs) for xs in zip(*ls)]
    order = range(n_chunks - 1, -1, -1) if reverse else range(n_chunks)

    def cumsum(x):
        step = 1
        while step < C:
            if reverse:
                x = x + jnp.where(pi < C - step, pltpu.roll(x, C - step, 0), 0.0)
            else:
                x = x + jnp.where(pi >= step, pltpu.roll(x, step, 0), 0.0)
            step *= 2
        return x

    def chain(pairs):
        units = [(slice(c * C, (c + 1) * C), slice(p * LANES, (p + 1) * LANES)) for p in pairs for c in range(n_chunks)]
        load = lambda ref: [ref[0, rows, cols].astype(F32) for rows, cols in units]
        r, lw, k, v, kk, a = (load(ref) for ref in (r_ref, lw_ref, k_ref, v_ref, kk_ref, a_ref))
        cum = each(cumsum, lw)
        total = each(lambda z: z[0:1] if reverse else z[C - 1:C], cum)
        g_in = each(lambda z: jnp.exp(-z), cum)
        g_to_end = each(lambda t, z: jnp.exp(t - z), total, cum)
        b = each(lambda x, y: x * y, kk, a)
        a_u = each(lambda x, z, l: x * jnp.exp(z - l), kk, cum, lw)
        r_u = each(lambda x, z: x * jnp.exp(z), r, cum)
        b_s = each(lambda x, y: stack(x * y), b, g_in)
        k_s = each(lambda x, y: stack(x * y), k, g_in)
        v_s = each(stack, v)
        bh = each(lambda x, y: x * y, b, g_to_end)
        kh = each(lambda x, y: x * y, k, g_to_end)
        big = each(lambda a_, r_, b_, k_: nt(jnp.concatenate([a_, r_], axis=0), jnp.concatenate([b_, k_], axis=0)),
                   a_u, r_u, b_s, k_s)
        l_ab = each(lambda z: jnp.where(strict, z[:C, :LANES], 0.0), big)
        l_ak = each(lambda z: jnp.where(strict, z[:C, LANES:], 0.0), big)
        p_rb = each(lambda z: jnp.where(incl, z[C:, :LANES], 0.0), big)
        p_rk = each(lambda z: jnp.where(incl, z[C:, LANES:], 0.0), big)
        same_blk = lambda n: (pi // n) == (pj // n)
        l_d = each(lambda z: jnp.where(same_blk(INV_BLOCK), z, 0.0), l_ab)
        l2 = each(lambda d: hp(d, stack(d)), l_d)
        lpv = each(lambda x, y, vs: hp(jnp.concatenate([x, y], axis=0), vs), l_ak, p_rk, v_s)
        lv = each(lambda z: z[:C], lpv)
        prv = each(lambda z: z[C:], lpv)
        khv = each(tn, kh, v)
        l34 = each(lambda d, s2: hp(jnp.concatenate([d, s2], axis=0), stack(s2)), l_d, l2)
        p1 = each(lambda d, s2, s34: ident - d + s2 - s34[:C], l_d, l2, l34)
        t_inv = each(lambda p, s34: p + hp(p, stack(s34[C:])), p1, l34)
        n = INV_BLOCK
        while n < C:
            off = jnp.logical_and(same_blk(2 * n), jnp.logical_not(same_blk(n)))
            c_n = each(lambda z: jnp.where(off, z, 0.0), l_ab)
            tc = each(lambda t, c_: hp(t, stack(c_)), t_inv, c_n)
            t_inv = each(lambda t, x: t - hp(x, stack(t)), t_inv, tc)
            n *= 2
        w = each(lambda t, x, y: hp(t, side(stack(x), stack(y))), t_inv, a_u, lv)
        pw = each(lambda p, x: hp(p, side(stack(x[:, :LANES]), stack(x[:, LANES:]))), p_rb, w)
        bw = each(tn, bh, w)
        q1 = each(lambda x, y: x - y[:, :LANES], r_u, pw)
        y2 = each(lambda x, y: x - y[:, LANES:], prv, pw)
        g_mat = each(lambda t, y: jnp.where(eye, jnp.exp(t), 0.0) - diag_blocks(y[:, :LANES]), total, bw)
        h_mat = each(lambda x, y: diag_blocks(x) - diag_blocks(y[:, LANES:]), khv, bw)
        s = [state[p] for p in pairs]
        for c in order:
            for i, p in enumerate(pairs):
                u = i * n_chunks + c
                ys = hp(jnp.concatenate([q1[u], g_mat[u]], axis=0), stack(s[i]))
                y_ref[0, c * C:(c + 1) * C, p * LANES:(p + 1) * LANES] = ys[:C] + y2[u]
                s[i] = ys[C:] + h_mat[u]
        for i, p in enumerate(pairs):
            state[p] = s[i]
        return s

    final_state = chain(list(range(n_pairs)))

    @pl.when(pl.program_id(2) == pl.num_programs(2) - 1)
    def _():
        for p in range(n_pairs):
            s_ref[0, p] = final_state[p]


WKV_PAIRS = 4
WKV_CHUNKS = 8


def _wkv(r, lw, k, v, kk, a, s0, reverse):
    B, T, _ = r.shape
    rows = min(T, WKV_CHUNKS * CHUNK)
    n_steps = T // rows
    width = WKV_PAIRS * LANES
    tmap = (lambda b, p, j: (b, n_steps - 1 - j, p)) if reverse else (lambda b, p, j: (b, j, p))
    seq_spec = pl.BlockSpec((1, rows, width), tmap)
    st_spec = pl.BlockSpec((1, WKV_PAIRS, RWKV_HEAD, LANES), lambda b, p, j: (b, p, 0, 0))
    return pl.pallas_call(
        functools.partial(_wkv_kernel, n_chunks=rows // CHUNK, n_pairs=WKV_PAIRS, reverse=reverse),
        grid=(B, PAIRS // WKV_PAIRS, n_steps),
        in_specs=[seq_spec] * 6 + [st_spec],
        out_specs=[seq_spec, st_spec],
        out_shape=[jax.ShapeDtypeStruct(r.shape, F32), jax.ShapeDtypeStruct((B, PAIRS, RWKV_HEAD, LANES), F32)],
        scratch_shapes=[pltpu.VMEM((WKV_PAIRS, RWKV_HEAD, LANES), F32)],
        compiler_params=_params("parallel", "parallel", "arbitrary"),
    )(r, lw, k, v, kk, a, s0)


def _rwkv_branch(y0_ref, y1_ref, r_ref, v_ref, k0_ref, k1_ref, g_ref, rk_ref, lng_ref, lnb_ref, wo_ref):
    y = y0_ref[0] + y1_ref[0]
    inv_n = 1.0 / RWKV_HEAD
    mean = _seg_sum_wide(y) * inv_n
    yc = y - mean
    var = _seg_sum_wide(yc * yc) * inv_n
    yn = yc * lax.rsqrt(var + GN_EPS) * lng_ref[...] + lnb_ref[...]
    f32 = lambda ref: ref[0].astype(F32)
    bonus = _seg_sum_wide(f32(r_ref) * (f32(k0_ref) + f32(k1_ref)) * rk_ref[...]) * f32(v_ref)
    return jnp.dot(((yn + bonus) * f32(g_ref)).astype(BF16), wo_ref[...], preferred_element_type=F32)


def _rwkv_branch_specs():
    vec = _const_spec((1, D_MODEL))
    return [_row_spec] * 7 + [vec, vec, vec, _const_spec((D_MODEL, D_MODEL))]


HALO = SUBLANES


def _pool_branch(x_ref, xp_ref, xn_ref, g_ref, mod_ref, w_ref, sc_ref, *, seq_len):
    tm = x_ref.shape[1]
    i = pl.program_id(1)
    last = pl.num_programs(1) - 1
    g = g_ref[...]
    mod = mod_ref[0]
    h = _norm_mod(x_ref[0], g, mod, 0)
    hp = _norm_mod(xp_ref[0], g, mod, 0) * jnp.where(i == 0, 0.0, 1.0)
    hn = _norm_mod(xn_ref[0], g, mod, 0) * jnp.where(i == last, 0.0, 1.0)
    ext = jnp.concatenate([hp, h, hn], axis=0)
    n_ext = tm + 2 * HALO
    t = i * tm + lax.broadcasted_iota(jnp.int32, (tm, 1), 0)
    outs = []
    for gi, win in enumerate(POOL_WINDOWS):
        e = ext[:, gi * POOL_GROUP:(gi + 1) * POOL_GROUP]
        acc = e + pltpu.roll(e, 1, 0)
        step = 1
        while 2 * step < win:
            acc = pltpu.roll(acc, step, 0) + pltpu.roll(acc, n_ext - step, 0)
            step *= 2
        half = win // 2
        cnt = (jnp.minimum(t + half, seq_len) - jnp.maximum(t - half, 0)).astype(F32)
        pooled = acc[HALO:HALO + tm] / cnt - e[HALO:HALO + tm]
        outs.append(jnp.dot(pooled.astype(BF16), w_ref[gi], preferred_element_type=F32))
    return jnp.concatenate(outs, axis=1) * sc_ref[...]


def _pool_branch_specs(seq_len):
    n_blk = seq_len // HALO
    prev_spec = lambda tm: pl.BlockSpec((1, HALO, D_MODEL), lambda b, i: (b, jnp.maximum(i * (tm // HALO) - 1, 0), 0))
    next_spec = lambda tm: pl.BlockSpec(
        (1, HALO, D_MODEL), lambda b, i: (b, jnp.minimum((i + 1) * (tm // HALO), n_blk - 1), 0))
    return [_row_spec, prev_spec, next_spec, _const_spec((1, D_MODEL)), lambda tm: _mod_spec(),
            _const_spec((len(POOL_WINDOWS), POOL_GROUP, POOL_GROUP)), _const_spec((1, D_MODEL))]


def _rope_tables(n_tokens):
    rows = n_tokens // GRID_W
    n_freq = HEAD_DIM // 4
    inv = ROPE_THETA ** (-jnp.arange(n_freq, dtype=F32) / n_freq)
    ang_r = jnp.arange(rows, dtype=F32)[:, None] * inv
    ang_c = jnp.arange(GRID_W, dtype=F32)[:, None] * inv
    ang = jnp.concatenate([
        jnp.broadcast_to(ang_r[:, None, :], (rows, GRID_W, n_freq)),
        jnp.broadcast_to(ang_c[None, :, :], (rows, GRID_W, n_freq))], axis=-1).reshape(rows * GRID_W, 2 * n_freq)
    cos, sin = jnp.cos(ang), jnp.sin(ang)
    return jnp.tile(cos, (1, 4)), jnp.tile(sin, (1, 4))


def kernel(x, c, ctx, c_ctx, w_mod, b_mod, norm1_g, norm2_g, mlp_w_in, mlp_w_out, attn_w_qkv, attn_q_gain, attn_k_gain, attn_w_o, rwkv_mu, rwkv_w_rkv, rwkv_w0, rwkv_w1, rwkv_w2, rwkv_a0, rwkv_a1, rwkv_a2, rwkv_g1, rwkv_g2, rwkv_k_k, rwkv_k_a, rwkv_r_k, rwkv_ln_g, rwkv_ln_b, rwkv_w_o, pool_w, pool_scale):
    B, S, _ = x.shape
    L = ctx.shape[1]
    depth = w_mod.shape[0]
    assert x.shape[2] == D_MODEL and S % (4 * CHUNK) == 0 and L % CHUNK == 0 and S % GRID_W == 0

    n_rows = -(-(B + 1) // SUBLANES) * SUBLANES
    cvec = jnp.concatenate([c, c_ctx[None], jnp.zeros((n_rows - B - 1, D_MODEL), F32)], axis=0)
    mod_all = _modulation(cvec, w_mod, b_mod).reshape(depth, n_rows, N_MOD, D_MODEL)
    cos_t, sin_t = _rope_tables(S)
    zero_tab = jnp.zeros((L, LANES), F32)
    row = lambda a: a.reshape(1, -1)

    for i in range(depth):
        last = i == depth - 1
        j = i // N_MIXERS
        mod_l = mod_all[i, :B]
        mod_c = jnp.broadcast_to(mod_all[i, B][None], (B, N_MOD, D_MODEL))
        g1 = row(norm1_g[i])
        kind = i % N_MIXERS
        if kind == 0:
            w_qkv = attn_w_qkv[j].astype(BF16)
            w_o = attn_w_o[j].astype(BF16)
            qg = jnp.tile(row(attn_q_gain[j]), (1, 2))
            kg = jnp.tile(row(attn_k_gain[j]), (1, 2))
            q_l, k_l, v_l = _qkv_project(x, g1, mod_l, w_qkv, qg, kg, cos_t, sin_t, True)
            q_c, k_c, v_c = _qkv_project(ctx, g1, mod_c, w_qkv, qg, kg, zero_tab, zero_tab, False)
            proj_specs = [_row_spec, _const_spec((D_MODEL, D_MODEL))]
            mix_l = (_proj_branch, (_attention(q_l, [(k_l, v_l), (k_c, v_c)]), w_o), proj_specs, 512)
            if not last:
                mix_c = (_proj_branch, (_attention(q_c, [(k_c, v_c)]), w_o), proj_specs, 512)
        elif kind == 1:
            p = {
                "mu": rwkv_mu[j], "w_rkv": rwkv_w_rkv[j].astype(BF16),
                "w1": jnp.concatenate([rwkv_w1[j, 0], rwkv_w1[j, 1]], axis=1).astype(BF16),
                "w2": (0.5 * rwkv_w2[j]).astype(BF16), "w0": 0.5 * rwkv_w0[j],
                "a1": jnp.concatenate([rwkv_a1[j, 0], rwkv_a1[j, 1]], axis=1).astype(BF16),
                "a2": (0.5 * rwkv_a2[j]).astype(BF16), "a0": 0.5 * rwkv_a0[j],
                "g1": rwkv_g1[j].astype(BF16), "g2": rwkv_g2[j].astype(BF16),
                "k_k": row(rwkv_k_k[j]), "k_ac": row(1.0 - 0.5 * rwkv_k_a[j]), "k_ah": row(0.5 * rwkv_k_a[j]), "r_k": row(rwkv_r_k[j]),
                "ln_g": row(rwkv_ln_g[j]), "ln_b": row(rwkv_ln_b[j]), "w_o": rwkv_w_o[j].astype(BF16),
            }
            r_l, v_l, kk_l, g_l, lw0_l, lw1_l, key0_l, key1_l, a0_l, a1_l = _rwkv_prepare(x, g1, mod_l, p)
            r_c, v_c, kk_c, g_c, lw0_c, lw1_c, key0_c, key1_c, a0_c, a1_c = _rwkv_prepare(ctx, g1, mod_c, p)
            zero_state = jnp.zeros((B, PAIRS, RWKV_HEAD, LANES), F32)
            ys_l, ys_c = [], []
            for rev, (lw_l, key_l, a_l, lw_c, key_c, a_c) in enumerate(
                    ((lw0_l, key0_l, a0_l, lw0_c, key0_c, a0_c), (lw1_l, key1_l, a1_l, lw1_c, key1_c, a1_c))):
                y_c, s_ctx = _wkv(r_c, lw_c, key_c, v_c, kk_c, a_c, zero_state, bool(rev))
                y_l, _ = _wkv(r_l, lw_l, key_l, v_l, kk_l, a_l, s_ctx, bool(rev))
                ys_l.append(y_l)
                ys_c.append(y_c)
            consts = (p["r_k"], p["ln_g"], p["ln_b"], p["w_o"])
            mix_l = (_rwkv_branch, (ys_l[0], ys_l[1], r_l, v_l, key0_l, key1_l, g_l) + consts, _rwkv_branch_specs(), 512)
            if not last:
                mix_c = (_rwkv_branch, (ys_c[0], ys_c[1], r_c, v_c, key0_c, key1_c, g_c) + consts,
                         _rwkv_branch_specs(), 256)
        else:
            w_p = pool_w[j].astype(BF16)
            sc = row(pool_scale[j])
            mix_l = (functools.partial(_pool_branch, seq_len=S), (x, x, x, g1, mod_l, w_p, sc), _pool_branch_specs(S), 512)
            if not last:
                mix_c = (functools.partial(_pool_branch, seq_len=L), (ctx, ctx, ctx, g1, mod_c, w_p, sc),
                         _pool_branch_specs(L), 256)
        g2 = row(norm2_g[i])
        w_in = mlp_w_in[i].astype(BF16)
        w_out = mlp_w_out[i].astype(BF16)
        x = _mixer_mlp(x, mod_l, g2, w_in, w_out, *mix_l)
        if not last:
            ctx = _mixer_mlp(ctx, mod_c, g2, w_in, w_out, *mix_c)
    return x
```

```python
import functools

import jax
import jax.numpy as jnp
import numpy as np
from jax import lax
from jax.experimental import pallas as pl
from jax.experimental.pallas import tpu as pltpu

F32 = jnp.float32
BF16 = jnp.bfloat16

D_MODEL = 1024
GRID_W = 64
N_MIXERS = 3
N_MOD = 6
EPS = 1e-6
N_HEADS = 16
N_KV_HEADS = 4
HEAD_DIM = 64
GQA_REP = N_HEADS // N_KV_HEADS
Q_WIDTH = N_HEADS * HEAD_DIM
KV_WIDTH = N_KV_HEADS * HEAD_DIM
QKV_WIDTH = Q_WIDTH + 2 * KV_WIDTH
ROPE_THETA = 10000.0
RWKV_HEAD = 64
DECAY_LORA = 64
ICLR_LORA = 64
GATE_LORA = 160
GN_EPS = RWKV_HEAD * 1e-5
POOL_WINDOWS = (2, 4, 8, 16)
POOL_GROUP = D_MODEL // len(POOL_WINDOWS)
D_FF = 4 * D_MODEL

LANES = 128
SUBLANES = 8
PAIRS = D_MODEL // LANES
CHUNK = 64
INV_BLOCK = 8
VMEM_LIMIT = 56 * 1024 * 1024

HIGHEST = lax.Precision.HIGHEST


def _params(*sem):
    return pltpu.CompilerParams(dimension_semantics=sem, vmem_limit_bytes=VMEM_LIMIT)


def _const_spec(shape):
    zeros = (0,) * len(shape)
    return pl.BlockSpec(shape, lambda *_: zeros, pipeline_mode=pl.Buffered(1))


def _row_spec(tm, width=D_MODEL):
    return pl.BlockSpec((1, tm, width), lambda b, i: (b, i, 0))


def _mod_spec():
    return pl.BlockSpec((1, N_MOD, D_MODEL), lambda b, i: (b, 0, 0))


def _norm_mod(x, g, mod, k):
    y = x * lax.rsqrt(jnp.mean(x * x, axis=-1, keepdims=True) + EPS) * g
    return y * (1.0 + mod[k + 1:k + 2]) + mod[k:k + 1]


def _seg_sum(x):
    i = lax.broadcasted_iota(jnp.int32, (LANES, LANES), 0) // HEAD_DIM
    j = lax.broadcasted_iota(jnp.int32, (LANES, LANES), 1) // HEAD_DIM
    return jnp.dot(x.astype(BF16), jnp.where(i == j, 1.0, 0.0).astype(BF16), preferred_element_type=F32)


def _sigmoid(z):
    return 0.5 * jnp.tanh(0.5 * z) + 0.5


def _seg_sum_wide(x):
    return jnp.concatenate([_seg_sum(x[:, p * LANES:(p + 1) * LANES]) for p in range(x.shape[1] // LANES)], axis=1)


def _mod_kernel(c_ref, w_ref, b_ref, o_ref):
    c = c_ref[...]
    s = c * jax.nn.sigmoid(c)
    o_ref[0] = jnp.dot(s, w_ref[0], precision=HIGHEST, preferred_element_type=F32) + b_ref[0]


def _modulation(cvec, w_mod, b_mod):
    depth = w_mod.shape[0]
    rows = cvec.shape[0]
    tn = 1536
    return pl.pallas_call(
        _mod_kernel,
        grid=(depth, N_MOD * D_MODEL // tn),
        in_specs=[pl.BlockSpec((rows, D_MODEL), lambda l, j: (0, 0)),
                  pl.BlockSpec((1, D_MODEL, tn), lambda l, j: (l, 0, j)),
                  pl.BlockSpec((1, 1, tn), lambda l, j: (l, 0, j))],
        out_specs=pl.BlockSpec((1, rows, tn), lambda l, j: (l, 0, j)),
        out_shape=jax.ShapeDtypeStruct((depth, rows, N_MOD * D_MODEL), F32),
        compiler_params=_params("parallel", "parallel"),
    )(cvec, w_mod, b_mod.reshape(depth, 1, N_MOD * D_MODEL))


VT_ROWS = HEAD_DIM + 16
Q_SCALE = float(HEAD_DIM ** -0.5 * np.log2(np.e))


def _qkv_kernel(x_ref, g_ref, mod_ref, w_ref, qg_ref, kg_ref, cos_ref, sin_ref, q_ref, k_ref, v_ref, *, use_rope):
    h = _norm_mod(x_ref[0], g_ref[...], mod_ref[0], 0).astype(BF16)
    acc = jnp.dot(h, w_ref[...], preferred_element_type=F32)
    src_lane = lax.broadcasted_iota(jnp.int32, (2 * LANES, LANES), 0) % LANES
    dst_lane = lax.broadcasted_iota(jnp.int32, (2 * LANES, LANES), 1)
    half = HEAD_DIM // 2
    first_half = (dst_lane % HEAD_DIM) < half
    rot_mat = jnp.where(first_half & (src_lane == dst_lane + half), -1.0,
                        jnp.where(jnp.logical_not(first_half) & (src_lane == dst_lane - half), 1.0, 0.0)).astype(BF16)

    def head_pair(xp, gain, scale):
        y = xp * lax.rsqrt(_seg_sum(xp * xp) * (1.0 / HEAD_DIM) + EPS) * gain
        if use_rope:
            y_hi = y.astype(BF16)
            y_lo = (y - y_hi.astype(F32)).astype(BF16)
            rot = jnp.dot(jnp.concatenate([y_hi, y_lo], axis=1), rot_mat, preferred_element_type=F32)
            y = y * cos_ref[...] + rot * sin_ref[...]
        return (y * scale).astype(BF16)

    for p in range(Q_WIDTH // LANES):
        y = head_pair(acc[:, p * LANES:(p + 1) * LANES], qg_ref[...], Q_SCALE)
        q_ref[0, 2 * p] = y[:, :HEAD_DIM]
        q_ref[0, 2 * p + 1] = y[:, HEAD_DIM:]
    for p in range(KV_WIDTH // LANES):
        y = head_pair(acc[:, Q_WIDTH + p * LANES:Q_WIDTH + (p + 1) * LANES], kg_ref[...], 1.0)
        k_ref[0, 2 * p] = y[:, :HEAD_DIM]
        k_ref[0, 2 * p + 1] = y[:, HEAD_DIM:]
    ones = jnp.ones((VT_ROWS - HEAD_DIM, acc.shape[0]), F32)
    for p in range(KV_WIDTH // LANES):
        c0 = Q_WIDTH + KV_WIDTH + p * LANES
        vt = acc[:, c0:c0 + LANES].T
        v_ref[0, 2 * p] = jnp.concatenate([vt[:HEAD_DIM], ones], axis=0).astype(BF16)
        v_ref[0, 2 * p + 1] = jnp.concatenate([vt[HEAD_DIM:], ones], axis=0).astype(BF16)


def _qkv_project(x, g, mod, w_bf, qg, kg, cos_t, sin_t, use_rope):
    B, T, _ = x.shape
    tm = min(T, 1024)
    tab_spec = pl.BlockSpec((tm, LANES), lambda b, i: (i, 0))
    head_spec = lambda n: pl.BlockSpec((1, n, tm, HEAD_DIM), lambda b, i: (b, 0, i, 0))
    return pl.pallas_call(
        functools.partial(_qkv_kernel, use_rope=use_rope),
        grid=(B, T // tm),
        in_specs=[_row_spec(tm), _const_spec((1, D_MODEL)), _mod_spec(), _const_spec((D_MODEL, QKV_WIDTH)),
                  _const_spec((1, LANES)), _const_spec((1, LANES)), tab_spec, tab_spec],
        out_specs=[head_spec(N_HEADS), head_spec(N_KV_HEADS),
                   pl.BlockSpec((1, N_KV_HEADS, VT_ROWS, tm), lambda b, i: (b, 0, 0, i))],
        out_shape=[jax.ShapeDtypeStruct((B, N_HEADS, T, HEAD_DIM), BF16),
                   jax.ShapeDtypeStruct((B, N_KV_HEADS, T, HEAD_DIM), BF16),
                   jax.ShapeDtypeStruct((B, N_KV_HEADS, VT_ROWS, T), BF16)],
        compiler_params=_params("parallel", "parallel"),
    )(x, g, mod, w_bf, qg, kg, cos_t, sin_t)


def _attn_kernel(*refs, n_src, tq):
    q_ref = refs[0]
    kv_refs = refs[1:1 + 2 * n_src]
    o_ref = refs[1 + 2 * n_src]
    s_scr, m_scr = refs[2 + 2 * n_src:]
    nq = GQA_REP * tq
    i = pl.program_id(0)
    pieces = []
    row = 0
    for s in range(n_src):
        keys_total = kv_refs[2 * s].shape[2]
        ck = min(ATTN_KV_CHUNK, keys_total)
        for r0 in range(0, keys_total, ck):
            pieces.append((kv_refs[2 * s], kv_refs[2 * s + 1], r0, ck, row))
            row += ck

    def run(with_pass2):
        q = q_ref[0].reshape(nq, HEAD_DIM)
        m8 = None
        if with_pass2:
            m_prev = m_scr[...]
            acc = jnp.zeros((VT_ROWS, nq), F32)
        for k_ref, vt_ref, r0, ck, c0 in pieces:
            if with_pass2:
                p = jnp.exp2(s_scr[c0:c0 + ck, :] - m_prev).astype(BF16)
                acc = acc + jnp.dot(vt_ref[0, 0, :, r0:r0 + ck], p, preferred_element_type=F32)
            s_blk = lax.dot_general(k_ref[0, 0, r0:r0 + ck, :], q, (((1,), (1,)), ((), ())),
                                    preferred_element_type=F32)
            s_scr[c0:c0 + ck, :] = s_blk
            bm = jnp.max(s_blk.reshape(ck // SUBLANES, SUBLANES, nq), axis=0)
            m8 = bm if m8 is None else jnp.maximum(m8, bm)
        m_scr[...] = jnp.max(m8, axis=0, keepdims=True)
        if with_pass2:
            o = (acc[:HEAD_DIM] / acc[HEAD_DIM:HEAD_DIM + 1]).T
            o_ref[0] = jnp.concatenate([o[h * tq:(h + 1) * tq] for h in range(GQA_REP)], axis=1).astype(BF16)

    pl.when(i == 0)(lambda: run(False))
    pl.when(i > 0)(lambda: run(True))


ATTN_KV_CHUNK = 256


def _attention(q, kv_sources):
    B, _, T, _ = q.shape
    tq = min(T, 256)
    n_tiles = T // tq
    n_all = B * N_KV_HEADS * n_tiles
    total = sum(k.shape[2] for k, _ in kv_sources)

    def tile(t):
        t = jnp.clip(t, 0, n_all - 1)
        return t // (N_KV_HEADS * n_tiles), (t // n_tiles) % N_KV_HEADS, t % n_tiles

    def q_map(t):
        b, g, i = tile(t)
        return b, g, i, 0

    def k_map(t):
        b, g, _ = tile(t)
        return b, g, 0, 0

    def v_map(t):
        b, g, _ = tile(t - 1)
        return b, g, 0, 0

    def o_map(t):
        b, g, i = tile(t - 1)
        return b, i, g

    in_specs = [pl.BlockSpec((1, GQA_REP, tq, HEAD_DIM), q_map)]
    args = [q]
    for k, vt in kv_sources:
        in_specs += [pl.BlockSpec((1, 1, k.shape[2], HEAD_DIM), k_map),
                     pl.BlockSpec((1, 1, VT_ROWS, k.shape[2]), v_map)]
        args += [k, vt]
    return pl.pallas_call(
        functools.partial(_attn_kernel, n_src=len(kv_sources), tq=tq),
        grid=(n_all + 1,),
        in_specs=in_specs,
        out_specs=pl.BlockSpec((1, tq, GQA_REP * HEAD_DIM), o_map),
        out_shape=jax.ShapeDtypeStruct((B, T, Q_WIDTH), BF16),
        scratch_shapes=[pltpu.VMEM((total, GQA_REP * tq), F32), pltpu.VMEM((1, GQA_REP * tq), F32)],
        compiler_params=_params("arbitrary"),
    )(*args)


def _proj_branch(y_ref, w_ref):
    return jnp.dot(y_ref[0], w_ref[...], preferred_element_type=F32)


def _mixer_mlp_kernel(*refs, branch, ff_chunk):
    x_ref, mod_ref, g_ref, win_ref, wout_ref = refs[:5]
    o_ref = refs[-1]
    x = x_ref[0]
    mod = mod_ref[0]
    if branch is not None:
        x = x + mod[2:3] * branch(*refs[5:-1])
    h = _norm_mod(x, g_ref[...], mod, 3).astype(BF16)
    acc = jnp.zeros(x.shape, F32)
    for f0 in range(0, D_FF, ff_chunk):
        u = jnp.maximum(jnp.dot(h, win_ref[:, f0:f0 + ff_chunk], preferred_element_type=F32), 0.0)
        acc = acc + jnp.dot((u * u).astype(BF16), wout_ref[f0:f0 + ff_chunk, :], preferred_element_type=F32)
    o_ref[0] = x + mod[5:6] * acc


def _mixer_mlp(x, mod, g, win_bf, wout_bf, branch=None, branch_args=(), branch_specs=(), tm_max=512):
    B, T, _ = x.shape
    tm = min(T, tm_max)
    specs = [s(tm) if callable(s) else s for s in branch_specs]
    return pl.pallas_call(
        functools.partial(_mixer_mlp_kernel, branch=branch, ff_chunk=1024),
        grid=(B, T // tm),
        in_specs=[_row_spec(tm), _mod_spec(), _const_spec((1, D_MODEL)),
                  _const_spec((D_MODEL, D_FF)), _const_spec((D_FF, D_MODEL))] + specs,
        out_specs=_row_spec(tm),
        out_shape=jax.ShapeDtypeStruct(x.shape, F32),
        compiler_params=_params("parallel", "parallel"),
    )(x, mod, g, win_bf, wout_bf, *branch_args)


DECAY_SCALE = float(np.exp(-0.5))


def _rwkv_prep_kernel(x_ref, xp_ref, xn_ref, g_ref, mod_ref, mu_ref, wrkv_ref, w1_ref, w2_ref, w0_ref, a1_ref, a2_ref,
                      a0_ref, g1_ref, g2_ref, kk_ref, kac_ref, kah_ref,
                      r_o, v_o, kk_o, g_o, lw0_o, lw1_o, key0_o, key1_o, a0_o, a1_o, *, tm):
    i = pl.program_id(1)
    last = pl.num_programs(1) - 1
    g = g_ref[...]
    mod = mod_ref[0]
    h = _norm_mod(x_ref[0], g, mod, 0)
    hp = _norm_mod(xp_ref[0][SUBLANES - 1:SUBLANES], g, mod, 0) * jnp.where(i == 0, 0.0, 1.0)
    hn = _norm_mod(xn_ref[0][0:1], g, mod, 0) * jnp.where(i == last, 0.0, 1.0)
    row = lax.broadcasted_iota(jnp.int32, (tm, 1), 0)
    prev = jnp.where(row == 0, hp, pltpu.roll(h, 1, 0))
    nxt = jnp.where(row == tm - 1, hn, pltpu.roll(h, tm - 1, 0))
    xx = 0.5 * (prev + nxt) - h
    mix = lambda m: (h + xx * mu_ref[m:m + 1]).astype(BF16)
    dot = functools.partial(jnp.dot, preferred_element_type=F32)
    r = dot(mix(0), wrkv_ref[0])
    k = dot(mix(2), wrkv_ref[1])
    v = dot(mix(3), wrkv_ref[2])
    r_o[0] = r.astype(BF16)
    v_o[0] = v.astype(BF16)
    g_o[0] = dot(_sigmoid(dot(mix(5), g1_ref[...])).astype(BF16), g2_ref[...]).astype(BF16)
    kk = k * kk_ref[...]
    kk_o[0] = kk * lax.rsqrt(jnp.maximum(_seg_sum_wide(kk * kk), 1e-24))
    tw = jnp.tanh(dot(mix(1), w1_ref[...])).astype(BF16)
    ta = dot(mix(4), a1_ref[...]).astype(BF16)
    for d, (lw_o, key_o, a_o) in enumerate(((lw0_o, key0_o, a0_o), (lw1_o, key1_o, a1_o))):
        th = jnp.tanh(w0_ref[d:d + 1] + dot(tw[:, d * DECAY_LORA:(d + 1) * DECAY_LORA], w2_ref[d]))
        lw_o[0] = (-0.5 * DECAY_SCALE) * th + (-0.5 * DECAY_SCALE)
        ah = jnp.tanh(a0_ref[d:d + 1] + dot(ta[:, d * ICLR_LORA:(d + 1) * ICLR_LORA], a2_ref[d]))
        a_o[0] = 0.5 * ah + 0.5
        key_o[0] = (k * (kac_ref[...] + kah_ref[...] * ah)).astype(BF16)


def _rwkv_prepare(x, g, mod, p):
    B, T, _ = x.shape
    tm = min(T, 512)
    nb = tm // SUBLANES
    n_blk8 = T // SUBLANES
    prev_spec = pl.BlockSpec((1, SUBLANES, D_MODEL), lambda b, i: (b, jnp.maximum(i * nb - 1, 0), 0))
    next_spec = pl.BlockSpec((1, SUBLANES, D_MODEL), lambda b, i: (b, jnp.minimum((i + 1) * nb, n_blk8 - 1), 0))
    o_bf, o_f32 = jax.ShapeDtypeStruct(x.shape, BF16), jax.ShapeDtypeStruct(x.shape, F32)
    return pl.pallas_call(
        functools.partial(_rwkv_prep_kernel, tm=tm),
        grid=(B, T // tm),
        in_specs=[_row_spec(tm), prev_spec, next_spec, _const_spec((1, D_MODEL)), _mod_spec(),
                  _const_spec((6, D_MODEL)), _const_spec((3, D_MODEL, D_MODEL)),
                  _const_spec((D_MODEL, 2 * DECAY_LORA)), _const_spec((2, DECAY_LORA, D_MODEL)),
                  _const_spec((2, D_MODEL)),
                  _const_spec((D_MODEL, 2 * ICLR_LORA)), _const_spec((2, ICLR_LORA, D_MODEL)),
                  _const_spec((2, D_MODEL)),
                  _const_spec((D_MODEL, GATE_LORA)), _const_spec((GATE_LORA, D_MODEL)),
                  _const_spec((1, D_MODEL)), _const_spec((1, D_MODEL)), _const_spec((1, D_MODEL))],
        out_specs=[_row_spec(tm)] * 10,
        out_shape=[o_bf, o_bf, o_f32, o_bf, o_f32, o_f32, o_bf, o_bf, o_f32, o_f32],
        compiler_params=_params("parallel", "parallel"),
    )(x, x, x, g, mod, p["mu"], p["w_rkv"], p["w1"], p["w2"], p["w0"], p["a1"], p["a2"], p["a0"], p["g1"], p["g2"],
      p["k_k"], p["k_ac"], p["k_ah"])


def _wkv_kernel(r_ref, lw_ref, k_ref, v_ref, kk_ref, a_ref, s0_ref, y_ref, s_ref, state, *, n_chunks, n_pairs,
                reverse):
    C = CHUNK

    def mm(dims):
        return lambda a, b: lax.dot_general(a.astype(BF16), b.astype(BF16), dims, preferred_element_type=F32)

    hp = mm((((1,), (0,)), ((), ())))
    nt = mm((((1,), (1,)), ((), ())))
    tn = mm((((0,), (0,)), ((), ())))

    @pl.when(pl.program_id(2) == 0)
    def _():
        state[...] = s0_ref[0]

    pi = lax.broadcasted_iota(jnp.int32, (C, LANES), 0)
    pj = lax.broadcasted_iota(jnp.int32, (C, LANES), 1) % C
    strict = (pi < pj) if reverse else (pi > pj)
    incl = (pi <= pj) if reverse else (pi >= pj)
    eye = pi == pj
    ident = jnp.where(eye, 1.0, 0.0)
    head0 = lax.broadcasted_iota(jnp.int32, (1, LANES), 1) < RWKV_HEAD

    def stack(z):
        zb = z.astype(BF16)
        zero = jnp.zeros_like(zb)
        return jnp.concatenate([jnp.where(head0, zb, zero), jnp.where(head0, zero, zb)], axis=0)

    side = lambda x, y: jnp.concatenate([x, y], axis=1)
    diag_blocks = lambda z: jnp.where(head0, z[:C], z[C:])

    each = lambda f, *ls: [f(*xs) for xs in zip(*ls)]
    order = range(n_chunks - 1, -1, -1) if reverse else range(n_chunks)

    def cumsum(x):
        step = 1
        while step < C:
            if reverse:
                x = x + jnp.where(pi < C - step, pltpu.roll(x, C - step, 0), 0.0)
            else:
                x = x + jnp.where(pi >= step, pltpu.roll(x, step, 0), 0.0)
            step *= 2
        return x

    def chain(pairs):
        units = [(slice(c * C, (c + 1) * C), slice(p * LANES, (p + 1) * LANES)) for p in pairs for c in range(n_chunks)]
        load = lambda ref: [ref[0, rows, cols].astype(F32) for rows, cols in units]
        r, lw, k, v, kk, a = (load(ref) for ref in (r_ref, lw_ref, k_ref, v_ref, kk_ref, a_ref))
        cum = each(cumsum, lw)
        total = each(lambda z: z[0:1] if reverse else z[C - 1:C], cum)
        g_in = each(lambda z: jnp.exp(-z), cum)
        g_to_end = each(lambda t, z: jnp.exp(t - z), total, cum)
        b = each(lambda x, y: x * y, kk, a)
        a_u = each(lambda x, z, l: x * jnp.exp(z - l), kk, cum, lw)
        r_u = each(lambda x, z: x * jnp.exp(z), r, cum)
        b_s = each(lambda x, y: stack(x * y), b, g_in)
        k_s = each(lambda x, y: stack(x * y), k, g_in)
        v_s = each(stack, v)
        bh = each(lambda x, y: x * y, b, g_to_end)
        kh = each(lambda x, y: x * y, k, g_to_end)
        big = each(lambda a_, r_, b_, k_: nt(jnp.concatenate([a_, r_], axis=0), jnp.concatenate([b_, k_], axis=0)),
                   a_u, r_u, b_s, k_s)
        l_ab = each(lambda z: jnp.where(strict, z[:C, :LANES], 0.0), big)
        l_ak = each(lambda z: jnp.where(strict, z[:C, LANES:], 0.0), big)
        p_rb = each(lambda z: jnp.where(incl, z[C:, :LANES], 0.0), big)
        p_rk = each(lambda z: jnp.where(incl, z[C:, LANES:], 0.0), big)
        same_blk = lambda n: (pi // n) == (pj // n)
        l_d = each(lambda z: jnp.where(same_blk(INV_BLOCK), z, 0.0), l_ab)
        l2 = each(lambda d: hp(d, stack(d)), l_d)
        lpv = each(lambda x, y, vs: hp(jnp.concatenate([x, y], axis=0), vs), l_ak, p_rk, v_s)
        lv = each(lambda z: z[:C], lpv)
        prv = each(lambda z: z[C:], lpv)
        khv = each(tn, kh, v)
        l34 = each(lambda d, s2: hp(jnp.concatenate([d, s2], axis=0), stack(s2)), l_d, l2)
        p1 = each(lambda d, s2, s34: ident - d + s2 - s34[:C], l_d, l2, l34)
        t_inv = each(lambda p, s34: p + hp(p, stack(s34[C:])), p1, l34)
        n = INV_BLOCK
        while n < C:
            off = jnp.logical_and(same_blk(2 * n), jnp.logical_not(same_blk(n)))
            c_n = each(lambda z: jnp.where(off, z, 0.0), l_ab)
            tc = each(lambda t, c_: hp(t, stack(c_)), t_inv, c_n)
            t_inv = each(lambda t, x: t - hp(x, stack(t)), t_inv, tc)
            n *= 2
        w = each(lambda t, x, y: hp(t, side(stack(x), stack(y))), t_inv, a_u, lv)
        pw = each(lambda p, x: hp(p, side(stack(x[:, :LANES]), stack(x[:, LANES:]))), p_rb, w)
        bw = each(tn, bh, w)
        q1 = each(lambda x, y: x - y[:, :LANES], r_u, pw)
        y2 = each(lambda x, y: x - y[:, LANES:], prv, pw)
        g_mat = each(lambda t, y: jnp.where(eye, jnp.exp(t), 0.0) - diag_blocks(y[:, :LANES]), total, bw)
        h_mat = each(lambda x, y: diag_blocks(x) - diag_blocks(y[:, LANES:]), khv, bw)
        s = [state[p] for p in pairs]
        for c in order:
            for i, p in enumerate(pairs):
                u = i * n_chunks + c
                ys = hp(jnp.concatenate([q1[u], g_mat[u]], axis=0), stack(s[i]))
                y_ref[0, c * C:(c + 1) * C, p * LANES:(p + 1) * LANES] = ys[:C] + y2[u]
                s[i] = ys[C:] + h_mat[u]
        for i, p in enumerate(pairs):
            state[p] = s[i]
        return s

    final_state = chain(list(range(n_pairs)))

    @pl.when(pl.program_id(2) == pl.num_programs(2) - 1)
    def _():
        for p in range(n_pairs):
            s_ref[0, p] = final_state[p]


WKV_PAIRS = 4
WKV_CHUNKS = 8


def _wkv(r, lw, k, v, kk, a, s0, reverse):
    B, T, _ = r.shape
    rows = min(T, WKV_CHUNKS * CHUNK)
    n_steps = T // rows
    width = WKV_PAIRS * LANES
    tmap = (lambda b, p, j: (b, n_steps - 1 - j, p)) if reverse else (lambda b, p, j: (b, j, p))
    seq_spec = pl.BlockSpec((1, rows, width), tmap)
    st_spec = pl.BlockSpec((1, WKV_PAIRS, RWKV_HEAD, LANES), lambda b, p, j: (b, p, 0, 0))
    return pl.pallas_call(
        functools.partial(_wkv_kernel, n_chunks=rows // CHUNK, n_pairs=WKV_PAIRS, reverse=reverse),
        grid=(B, PAIRS // WKV_PAIRS, n_steps),
        in_specs=[seq_spec] * 6 + [st_spec],
        out_specs=[seq_spec, st_spec],
        out_shape=[jax.ShapeDtypeStruct(r.shape, F32), jax.ShapeDtypeStruct((B, PAIRS, RWKV_HEAD, LANES), F32)],
        scratch_shapes=[pltpu.VMEM((WKV_PAIRS, RWKV_HEAD, LANES), F32)],
        compiler_params=_params("parallel", "parallel", "arbitrary"),
    )(r, lw, k, v, kk, a, s0)


def _rwkv_branch(y0_ref, y1_ref, r_ref, v_ref, k0_ref, k1_ref, g_ref, rk_ref, lng_ref, lnb_ref, wo_ref):
    y = y0_ref[0] + y1_ref[0]
    inv_n = 1.0 / RWKV_HEAD
    mean = _seg_sum_wide(y) * inv_n
    yc = y - mean
    var = _seg_sum_wide(yc * yc) * inv_n
    yn = yc * lax.rsqrt(var + GN_EPS) * lng_ref[...] + lnb_ref[...]
    f32 = lambda ref: ref[0].astype(F32)
    bonus = _seg_sum_wide(f32(r_ref) * (f32(k0_ref) + f32(k1_ref)) * rk_ref[...]) * f32(v_ref)
    return jnp.dot(((yn + bonus) * f32(g_ref)).astype(BF16), wo_ref[...], preferred_element_type=F32)


def _rwkv_branch_specs():
    vec = _const_spec((1, D_MODEL))
    return [_row_spec] * 7 + [vec, vec, vec, _const_spec((D_MODEL, D_MODEL))]


HALO = SUBLANES


def _pool_branch(x_ref, xp_ref, xn_ref, g_ref, mod_ref, w_ref, sc_ref, *, seq_len):
    tm = x_ref.shape[1]
    i = pl.program_id(1)
    last = pl.num_programs(1) - 1
    g = g_ref[...]
    mod = mod_ref[0]
    h = _norm_mod(x_ref[0], g, mod, 0)
    hp = _norm_mod(xp_ref[0], g, mod, 0) * jnp.where(i == 0, 0.0, 1.0)
    hn = _norm_mod(xn_ref[0], g, mod, 0) * jnp.where(i == last, 0.0, 1.0)
    ext = jnp.concatenate([hp, h, hn], axis=0)
    n_ext = tm + 2 * HALO
    t = i * tm + lax.broadcasted_iota(jnp.int32, (tm, 1), 0)
    outs = []
    for gi, win in enumerate(POOL_WINDOWS):
        e = ext[:, gi * POOL_GROUP:(gi + 1) * POOL_GROUP]
        acc = e + pltpu.roll(e, 1, 0)
        step = 1
        while 2 * step < win:
            acc = pltpu.roll(acc, step, 0) + pltpu.roll(acc, n_ext - step, 0)
            step *= 2
        half = win // 2
        cnt = (jnp.minimum(t + half, seq_len) - jnp.maximum(t - half, 0)).astype(F32)
        pooled = acc[HALO:HALO + tm] / cnt - e[HALO:HALO + tm]
        outs.append(jnp.dot(pooled.astype(BF16), w_ref[gi], preferred_element_type=F32))
    return jnp.concatenate(outs, axis=1) * sc_ref[...]


def _pool_branch_specs(seq_len):
    n_blk = seq_len // HALO
    prev_spec = lambda tm: pl.BlockSpec((1, HALO, D_MODEL), lambda b, i: (b, jnp.maximum(i * (tm // HALO) - 1, 0), 0))
    next_spec = lambda tm: pl.BlockSpec(
        (1, HALO, D_MODEL), lambda b, i: (b, jnp.minimum((i + 1) * (tm // HALO), n_blk - 1), 0))
    return [_row_spec, prev_spec, next_spec, _const_spec((1, D_MODEL)), lambda tm: _mod_spec(),
            _const_spec((len(POOL_WINDOWS), POOL_GROUP, POOL_GROUP)), _const_spec((1, D_MODEL))]


def _rope_tables(n_tokens):
    rows = n_tokens // GRID_W
    n_freq = HEAD_DIM // 4
    inv = ROPE_THETA ** (-jnp.arange(n_freq, dtype=F32) / n_freq)
    ang_r = jnp.arange(rows, dtype=F32)[:, None] * inv
    ang_c = jnp.arange(GRID_W, dtype=F32)[:, None] * inv
    ang = jnp.concatenate([
        jnp.broadcast_to(ang_r[:, None, :], (rows, GRID_W, n_freq)),
        jnp.broadcast_to(ang_c[None, :, :], (rows, GRID_W, n_freq))], axis=-1).reshape(rows * GRID_W, 2 * n_freq)
    cos, sin = jnp.cos(ang), jnp.sin(ang)
    return jnp.tile(cos, (1, 4)), jnp.tile(sin, (1, 4))


def kernel(x, c, ctx, c_ctx, w_mod, b_mod, norm1_g, norm2_g, mlp_w_in, mlp_w_out, attn_w_qkv, attn_q_gain, attn_k_gain, attn_w_o, rwkv_mu, rwkv_w_rkv, rwkv_w0, rwkv_w1, rwkv_w2, rwkv_a0, rwkv_a1, rwkv_a2, rwkv_g1, rwkv_g2, rwkv_k_k, rwkv_k_a, rwkv_r_k, rwkv_ln_g, rwkv_ln_b, rwkv_w_o, pool_w, pool_scale):
    B, S, _ = x.shape
    L = ctx.shape[1]
    depth = w_mod.shape[0]
    assert x.shape[2] == D_MODEL and S % (4 * CHUNK) == 0 and L % CHUNK == 0 and S % GRID_W == 0

    n_rows = -(-(B + 1) // SUBLANES) * SUBLANES
    cvec = jnp.concatenate([c, c_ctx[None], jnp.zeros((n_rows - B - 1, D_MODEL), F32)], axis=0)
    mod_all = _modulation(cvec, w_mod, b_mod).reshape(depth, n_rows, N_MOD, D_MODEL)
    cos_t, sin_t = _rope_tables(S)
    zero_tab = jnp.zeros((L, LANES), F32)
    row = lambda a: a.reshape(1, -1)

    for i in range(depth):
        last = i == depth - 1
        j = i // N_MIXERS
        mod_l = mod_all[i, :B]
        mod_c = jnp.broadcast_to(mod_all[i, B][None], (B, N_MOD, D_MODEL))
        g1 = row(norm1_g[i])
        kind = i % N_MIXERS
        if kind == 0:
            w_qkv = attn_w_qkv[j].astype(BF16)
            w_o = attn_w_o[j].astype(BF16)
            qg = jnp.tile(row(attn_q_gain[j]), (1, 2))
            kg = jnp.tile(row(attn_k_gain[j]), (1, 2))
            q_l, k_l, v_l = _qkv_project(x, g1, mod_l, w_qkv, qg, kg, cos_t, sin_t, True)
            q_c, k_c, v_c = _qkv_project(ctx, g1, mod_c, w_qkv, qg, kg, zero_tab, zero_tab, False)
            proj_specs = [_row_spec, _const_spec((D_MODEL, D_MODEL))]
            mix_l = (_proj_branch, (_attention(q_l, [(k_l, v_l), (k_c, v_c)]), w_o), proj_specs, 512)
            if not last:
                mix_c = (_proj_branch, (_attention(q_c, [(k_c, v_c)]), w_o), proj_specs, 512)
        elif kind == 1:
            p = {
                "mu": rwkv_mu[j], "w_rkv": rwkv_w_rkv[j].astype(BF16),
                "w1": jnp.concatenate([rwkv_w1[j, 0], rwkv_w1[j, 1]], axis=1).astype(BF16),
                "w2": (0.5 * rwkv_w2[j]).astype(BF16), "w0": 0.5 * rwkv_w0[j],
                "a1": jnp.concatenate([rwkv_a1[j, 0], rwkv_a1[j, 1]], axis=1).astype(BF16),
                "a2": (0.5 * rwkv_a2[j]).astype(BF16), "a0": 0.5 * rwkv_a0[j],
                "g1": rwkv_g1[j].astype(BF16), "g2": rwkv_g2[j].astype(BF16),
                "k_k": row(rwkv_k_k[j]), "k_ac": row(1.0 - 0.5 * rwkv_k_a[j]), "k_ah": row(0.5 * rwkv_k_a[j]), "r_k": row(rwkv_r_k[j]),
                "ln_g": row(rwkv_ln_g[j]), "ln_b": row(rwkv_ln_b[j]), "w_o": rwkv_w_o[j].astype(BF16),
            }
            r_l, v_l, kk_l, g_l, lw0_l, lw1_l, key0_l, key1_l, a0_l, a1_l = _rwkv_prepare(x, g1, mod_l, p)
            r_c, v_c, kk_c, g_c, lw0_c, lw1_c, key0_c, key1_c, a0_c, a1_c = _rwkv_prepare(ctx, g1, mod_c, p)
            zero_state = jnp.zeros((B, PAIRS, RWKV_HEAD, LANES), F32)
            ys_l, ys_c = [], []
            for rev, (lw_l, key_l, a_l, lw_c, key_c, a_c) in enumerate(
                    ((lw0_l, key0_l, a0_l, lw0_c, key0_c, a0_c), (lw1_l, key1_l, a1_l, lw1_c, key1_c, a1_c))):
                y_c, s_ctx = _wkv(r_c, lw_c, key_c, v_c, kk_c, a_c, zero_state, bool(rev))
                y_l, _ = _wkv(r_l, lw_l, key_l, v_l, kk_l, a_l, s_ctx, bool(rev))
                ys_l.append(y_l)
                ys_c.append(y_c)
            consts = (p["r_k"], p["ln_g"], p["ln_b"], p["w_o"])
            mix_l = (_rwkv_branch, (ys_l[0], ys_l[1], r_l, v_l, key0_l, key1_l, g_l) + consts, _rwkv_branch_specs(), 512)
            if not last:
                mix_c = (_rwkv_branch, (ys_c[0], ys_c[1], r_c, v_c, key0_c, key1_c, g_c) + consts,
                         _rwkv_branch_specs(), 256)
        else:
            w_p = pool_w[j].astype(BF16)
            sc = row(pool_scale[j])
            mix_l = (functools.partial(_pool_branch, seq_len=S), (x, x, x, g1, mod_l, w_p, sc), _pool_branch_specs(S), 512)
            if not last:
                mix_c = (functools.partial(_pool_branch, seq_len=L), (ctx, ctx, ctx, g1, mod_c, w_p, sc),
                         _pool_branch_specs(L), 256)
        g2 = row(norm2_g[i])
        w_in = mlp_w_in[i].astype(BF16)
        w_out = mlp_w_out[i].astype(BF16)
        x = _mixer_mlp(x, mod_l, g2, w_in, w_out, *mix_l)
        if not last:
            ctx = _mixer_mlp(ctx, mod_c, g2, w_in, w_out, *mix_c)
    return x
```

```python
import functools

import jax
import jax.numpy as jnp
import numpy as np
from jax import lax
from jax.experimental import pallas as pl
from jax.experimental.pallas import tpu as pltpu

F32 = jnp.float32
BF16 = jnp.bfloat16

D_MODEL = 1024
GRID_W = 64
N_MIXERS = 3
N_MOD = 6
EPS = 1e-6
N_HEADS = 16
N_KV_HEADS = 4
HEAD_DIM = 64
GQA_REP = N_HEADS // N_KV_HEADS
Q_WIDTH = N_HEADS * HEAD_DIM
KV_WIDTH = N_KV_HEADS * HEAD_DIM
QKV_WIDTH = Q_WIDTH + 2 * KV_WIDTH
ROPE_THETA = 10000.0
RWKV_HEAD = 64
DECAY_LORA = 64
ICLR_LORA = 64
GATE_LORA = 160
GN_EPS = RWKV_HEAD * 1e-5
POOL_WINDOWS = (2, 4, 8, 16)
POOL_GROUP = D_MODEL // len(POOL_WINDOWS)
D_FF = 4 * D_MODEL

LANES = 128
SUBLANES = 8
PAIRS = D_MODEL // LANES
CHUNK = 64
INV_BLOCK = 8
VMEM_LIMIT = 56 * 1024 * 1024

HIGHEST = lax.Precision.HIGHEST


def _params(*sem):
    return pltpu.CompilerParams(dimension_semantics=sem, vmem_limit_bytes=VMEM_LIMIT)


def _const_spec(shape):
    zeros = (0,) * len(shape)
    return pl.BlockSpec(shape, lambda *_: zeros, pipeline_mode=pl.Buffered(1))


def _row_spec(tm, width=D_MODEL):
    return pl.BlockSpec((1, tm, width), lambda b, i: (b, i, 0))


def _mod_spec():
    return pl.BlockSpec((1, N_MOD, D_MODEL), lambda b, i: (b, 0, 0))


def _norm_mod(x, g, mod, k):
    y = x * lax.rsqrt(jnp.mean(x * x, axis=-1, keepdims=True) + EPS) * g
    return y * (1.0 + mod[k + 1:k + 2]) + mod[k:k + 1]


def _seg_sum(x):
    i = lax.broadcasted_iota(jnp.int32, (LANES, LANES), 0) // HEAD_DIM
    j = lax.broadcasted_iota(jnp.int32, (LANES, LANES), 1) // HEAD_DIM
    return jnp.dot(x.astype(BF16), jnp.where(i == j, 1.0, 0.0).astype(BF16), preferred_element_type=F32)


def _sigmoid(z):
    return 0.5 * jnp.tanh(0.5 * z) + 0.5


def _seg_sum_wide(x):
    return jnp.concatenate([_seg_sum(x[:, p * LANES:(p + 1) * LANES]) for p in range(x.shape[1] // LANES)], axis=1)


def _mod_kernel(c_ref, w_ref, b_ref, o_ref):
    c = c_ref[...]
    s = c * jax.nn.sigmoid(c)
    o_ref[0] = jnp.dot(s, w_ref[0], precision=HIGHEST, preferred_element_type=F32) + b_ref[0]


def _modulation(cvec, w_mod, b_mod):
    depth = w_mod.shape[0]
    rows = cvec.shape[0]
    tn = 1536
    return pl.pallas_call(
        _mod_kernel,
        grid=(depth, N_MOD * D_MODEL // tn),
        in_specs=[pl.BlockSpec((rows, D_MODEL), lambda l, j: (0, 0)),
                  pl.BlockSpec((1, D_MODEL, tn), lambda l, j: (l, 0, j)),
                  pl.BlockSpec((1, 1, tn), lambda l, j: (l, 0, j))],
        out_specs=pl.BlockSpec((1, rows, tn), lambda l, j: (l, 0, j)),
        out_shape=jax.ShapeDtypeStruct((depth, rows, N_MOD * D_MODEL), F32),
        compiler_params=_params("parallel", "parallel"),
    )(cvec, w_mod, b_mod.reshape(depth, 1, N_MOD * D_MODEL))


VT_ROWS = HEAD_DIM + 16
Q_SCALE = float(HEAD_DIM ** -0.5 * np.log2(np.e))


def _qkv_kernel(x_ref, g_ref, mod_ref, w_ref, qg_ref, kg_ref, cos_ref, sin_ref, q_ref, k_ref, v_ref, *, use_rope):
    h = _norm_mod(x_ref[0], g_ref[...], mod_ref[0], 0).astype(BF16)
    acc = jnp.dot(h, w_ref[...], preferred_element_type=F32)
    src_lane = lax.broadcasted_iota(jnp.int32, (2 * LANES, LANES), 0) % LANES
    dst_lane = lax.broadcasted_iota(jnp.int32, (2 * LANES, LANES), 1)
    half = HEAD_DIM // 2
    first_half = (dst_lane % HEAD_DIM) < half
    rot_mat = jnp.where(first_half & (src_lane == dst_lane + half), -1.0,
                        jnp.where(jnp.logical_not(first_half) & (src_lane == dst_lane - half), 1.0, 0.0)).astype(BF16)

    def head_pair(xp, gain, scale):
        y = xp * lax.rsqrt(_seg_sum(xp * xp) * (1.0 / HEAD_DIM) + EPS) * gain
        if use_rope:
            y_hi = y.astype(BF16)
            y_lo = (y - y_hi.astype(F32)).astype(BF16)
            rot = jnp.dot(jnp.concatenate([y_hi, y_lo], axis=1), rot_mat, preferred_element_type=F32)
            y = y * cos_ref[...] + rot * sin_ref[...]
        return (y * scale).astype(BF16)

    for p in range(Q_WIDTH // LANES):
        y = head_pair(acc[:, p * LANES:(p + 1) * LANES], qg_ref[...], Q_SCALE)
        q_ref[0, 2 * p] = y[:, :HEAD_DIM]
        q_ref[0, 2 * p + 1] = y[:, HEAD_DIM:]
    for p in range(KV_WIDTH // LANES):
        y = head_pair(acc[:, Q_WIDTH + p * LANES:Q_WIDTH + (p + 1) * LANES], kg_ref[...], 1.0)
        k_ref[0, 2 * p] = y[:, :HEAD_DIM]
        k_ref[0, 2 * p + 1] = y[:, HEAD_DIM:]
    ones = jnp.ones((VT_ROWS - HEAD_DIM, acc.shape[0]), F32)
    for p in range(KV_WIDTH // LANES):
        c0 = Q_WIDTH + KV_WIDTH + p * LANES
        vt = acc[:, c0:c0 + LANES].T
        v_ref[0, 2 * p] = jnp.concatenate([vt[:HEAD_DIM], ones], axis=0).astype(BF16)
        v_ref[0, 2 * p + 1] = jnp.concatenate([vt[HEAD_DIM:], ones], axis=0).astype(BF16)


def _qkv_project(x, g, mod, w_bf, qg, kg, cos_t, sin_t, use_rope):
    B, T, _ = x.shape
    tm = min(T, 1024)
    tab_spec = pl.BlockSpec((tm, LANES), lambda b, i: (i, 0))
    head_spec = lambda n: pl.BlockSpec((1, n, tm, HEAD_DIM), lambda b, i: (b, 0, i, 0))
    return pl.pallas_call(
        functools.partial(_qkv_kernel, use_rope=use_rope),
        grid=(B, T // tm),
        in_specs=[_row_spec(tm), _const_spec((1, D_MODEL)), _mod_spec(), _const_spec((D_MODEL, QKV_WIDTH)),
                  _const_spec((1, LANES)), _const_spec((1, LANES)), tab_spec, tab_spec],
        out_specs=[head_spec(N_HEADS), head_spec(N_KV_HEADS),
                   pl.BlockSpec((1, N_KV_HEADS, VT_ROWS, tm), lambda b, i: (b, 0, 0, i))],
        out_shape=[jax.ShapeDtypeStruct((B, N_HEADS, T, HEAD_DIM), BF16),
                   jax.ShapeDtypeStruct((B, N_KV_HEADS, T, HEAD_DIM), BF16),
                   jax.ShapeDtypeStruct((B, N_KV_HEADS, VT_ROWS, T), BF16)],
        compiler_params=_params("parallel", "parallel"),
    )(x, g, mod, w_bf, qg, kg, cos_t, sin_t)


def _attn_kernel(*refs, n_src, tq):
    q_ref = refs[0]
    kv_refs = refs[1:1 + 2 * n_src]
    o_ref = refs[1 + 2 * n_src]
    s_scr, m_scr = refs[2 + 2 * n_src:]
    nq = GQA_REP * tq
    i = pl.program_id(0)
    pieces = []
    row = 0
    for s in range(n_src):
        keys_total = kv_refs[2 * s].shape[2]
        ck = min(ATTN_KV_CHUNK, keys_total)
        for r0 in range(0, keys_total, ck):
            pieces.append((kv_refs[2 * s], kv_refs[2 * s + 1], r0, ck, row))
            row += ck

    def run(with_pass2):
        q = q_ref[0].reshape(nq, HEAD_DIM)
        m8 = None
        if with_pass2:
            m_prev = m_scr[...]
            acc = jnp.zeros((VT_ROWS, nq), F32)
        for k_ref, vt_ref, r0, ck, c0 in pieces:
            if with_pass2:
                p = jnp.exp2(s_scr[c0:c0 + ck, :] - m_prev).astype(BF16)
                acc = acc + jnp.dot(vt_ref[0, 0, :, r0:r0 + ck], p, preferred_element_type=F32)
            s_blk = lax.dot_general(k_ref[0, 0, r0:r0 + ck, :], q, (((1,), (1,)), ((), ())),
                                    preferred_element_type=F32)
            s_scr[c0:c0 + ck, :] = s_blk
            bm = jnp.max(s_blk.reshape(ck // SUBLANES, SUBLANES, nq), axis=0)
            m8 = bm if m8 is None else jnp.maximum(m8, bm)
        m_scr[...] = jnp.max(m8, axis=0, keepdims=True)
        if with_pass2:
            o = (acc[:HEAD_DIM] / acc[HEAD_DIM:HEAD_DIM + 1]).T
            o_ref[0] = jnp.concatenate([o[h * tq:(h + 1) * tq] for h in range(GQA_REP)], axis=1).astype(BF16)

    pl.when(i == 0)(lambda: run(False))
    pl.when(i > 0)(lambda: run(True))


ATTN_KV_CHUNK = 256


def _attention(q, kv_sources):
    B, _, T, _ = q.shape
    tq = min(T, 256)
    n_tiles = T // tq
    n_all = B * N_KV_HEADS * n_tiles
    total = sum(k.shape[2] for k, _ in kv_sources)

    def tile(t):
        t = jnp.clip(t, 0, n_all - 1)
        return t // (N_KV_HEADS * n_tiles), (t // n_tiles) % N_KV_HEADS, t % n_tiles

    def q_map(t):
        b, g, i = tile(t)
        return b, g, i, 0

    def k_map(t):
        b, g, _ = tile(t)
        return b, g, 0, 0

    def v_map(t):
        b, g, _ = tile(t - 1)
        return b, g, 0, 0

    def o_map(t):
        b, g, i = tile(t - 1)
        return b, i, g

    in_specs = [pl.BlockSpec((1, GQA_REP, tq, HEAD_DIM), q_map)]
    args = [q]
    for k, vt in kv_sources:
        in_specs += [pl.BlockSpec((1, 1, k.shape[2], HEAD_DIM), k_map),
                     pl.BlockSpec((1, 1, VT_ROWS, k.shape[2]), v_map)]
        args += [k, vt]
    return pl.pallas_call(
        functools.partial(_attn_kernel, n_src=len(kv_sources), tq=tq),
        grid=(n_all + 1,),
        in_specs=in_specs,
        out_specs=pl.BlockSpec((1, tq, GQA_REP * HEAD_DIM), o_map),
        out_shape=jax.ShapeDtypeStruct((B, T, Q_WIDTH), BF16),
        scratch_shapes=[pltpu.VMEM((total, GQA_REP * tq), F32), pltpu.VMEM((1, GQA_REP * tq), F32)],
        compiler_params=_params("arbitrary"),
    )(*args)


def _proj_branch(y_ref, w_ref):
    return jnp.dot(y_ref[0], w_ref[...], preferred_element_type=F32)


def _mixer_mlp_kernel(*refs, branch, ff_chunk):
    x_ref, mod_ref, g_ref, win_ref, wout_ref = refs[:5]
    o_ref = refs[-1]
    x = x_ref[0]
    mod = mod_ref[0]
    if branch is not None:
        x = x + mod[2:3] * branch(*refs[5:-1])
    h = _norm_mod(x, g_ref[...], mod, 3).astype(BF16)
    acc = jnp.zeros(x.shape, F32)
    for f0 in range(0, D_FF, ff_chunk):
        u = jnp.maximum(jnp.dot(h, win_ref[:, f0:f0 + ff_chunk], preferred_element_type=F32), 0.0)
        acc = acc + jnp.dot((u * u).astype(BF16), wout_ref[f0:f0 + ff_chunk, :], preferred_element_type=F32)
    o_ref[0] = x + mod[5:6] * acc


def _mixer_mlp(x, mod, g, win_bf, wout_bf, branch=None, branch_args=(), branch_specs=(), tm_max=512):
    B, T, _ = x.shape
    tm = min(T, tm_max)
    specs = [s(tm) if callable(s) else s for s in branch_specs]
    return pl.pallas_call(
        functools.partial(_mixer_mlp_kernel, branch=branch, ff_chunk=1024),
        grid=(B, T // tm),
        in_specs=[_row_spec(tm), _mod_spec(), _const_spec((1, D_MODEL)),
                  _const_spec((D_MODEL, D_FF)), _const_spec((D_FF, D_MODEL))] + specs,
        out_specs=_row_spec(tm),
        out_shape=jax.ShapeDtypeStruct(x.shape, F32),
        compiler_params=_params("parallel", "parallel"),
    )(x, mod, g, win_bf, wout_bf, *branch_args)


DECAY_SCALE = float(np.exp(-0.5))


def _rwkv_prep_kernel(x_ref, xp_ref, xn_ref, g_ref, mod_ref, mu_ref, wrkv_ref, w1_ref, w2_ref, w0_ref, a1_ref, a2_ref,
                      a0_ref, g1_ref, g2_ref, kk_ref, kac_ref, kah_ref,
                      r_o, v_o, kk_o, g_o, lw0_o, lw1_o, key0_o, key1_o, a0_o, a1_o, *, tm):
    i = pl.program_id(1)
    last = pl.num_programs(1) - 1
    g = g_ref[...]
    mod = mod_ref[0]
    h = _norm_mod(x_ref[0], g, mod, 0)
    hp = _norm_mod(xp_ref[0][SUBLANES - 1:SUBLANES], g, mod, 0) * jnp.where(i == 0, 0.0, 1.0)
    hn = _norm_mod(xn_ref[0][0:1], g, mod, 0) * jnp.where(i == last, 0.0, 1.0)
    row = lax.broadcasted_iota(jnp.int32, (tm, 1), 0)
    prev = jnp.where(row == 0, hp, pltpu.roll(h, 1, 0))
    nxt = jnp.where(row == tm - 1, hn, pltpu.roll(h, tm - 1, 0))
    xx = 0.5 * (prev + nxt) - h
    mix = lambda m: (h + xx * mu_ref[m:m + 1]).astype(BF16)
    dot = functools.partial(jnp.dot, preferred_element_type=F32)
    r = dot(mix(0), wrkv_ref[0])
    k = dot(mix(2), wrkv_ref[1])
    v = dot(mix(3), wrkv_ref[2])
    r_o[0] = r.astype(BF16)
    v_o[0] = v.astype(BF16)
    g_o[0] = dot(_sigmoid(dot(mix(5), g1_ref[...])).astype(BF16), g2_ref[...]).astype(BF16)
    kk = k * kk_ref[...]
    kk_o[0] = kk * lax.rsqrt(jnp.maximum(_seg_sum_wide(kk * kk), 1e-24))
    tw = jnp.tanh(dot(mix(1), w1_ref[...])).astype(BF16)
    ta = dot(mix(4), a1_ref[...]).astype(BF16)
    for d, (lw_o, key_o, a_o) in enumerate(((lw0_o, key0_o, a0_o), (lw1_o, key1_o, a1_o))):
        th = jnp.tanh(w0_ref[d:d + 1] + dot(tw[:, d * DECAY_LORA:(d + 1) * DECAY_LORA], w2_ref[d]))
        lw_o[0] = (-0.5 * DECAY_SCALE) * th + (-0.5 * DECAY_SCALE)
        ah = jnp.tanh(a0_ref[d:d + 1] + dot(ta[:, d * ICLR_LORA:(d + 1) * ICLR_LORA], a2_ref[d]))
        a_o[0] = 0.5 * ah + 0.5
        key_o[0] = (k * (kac_ref[...] + kah_ref[...] * ah)).astype(BF16)


def _rwkv_prepare(x, g, mod, p):
    B, T, _ = x.shape
    tm = min(T, 512)
    nb = tm // SUBLANES
    n_blk8 = T // SUBLANES
    prev_spec = pl.BlockSpec((1, SUBLANES, D_MODEL), lambda b, i: (b, jnp.maximum(i * nb - 1, 0), 0))
    next_spec = pl.BlockSpec((1, SUBLANES, D_MODEL), lambda b, i: (b, jnp.minimum((i + 1) * nb, n_blk8 - 1), 0))
    o_bf, o_f32 = jax.ShapeDtypeStruct(x.shape, BF16), jax.ShapeDtypeStruct(x.shape, F32)
    return pl.pallas_call(
        functools.partial(_rwkv_prep_kernel, tm=tm),
        grid=(B, T // tm),
        in_specs=[_row_spec(tm), prev_spec, next_spec, _const_spec((1, D_MODEL)), _mod_spec(),
                  _const_spec((6, D_MODEL)), _const_spec((3, D_MODEL, D_MODEL)),
                  _const_spec((D_MODEL, 2 * DECAY_LORA)), _const_spec((2, DECAY_LORA, D_MODEL)),
                  _const_spec((2, D_MODEL)),
                  _const_spec((D_MODEL, 2 * ICLR_LORA)), _const_spec((2, ICLR_LORA, D_MODEL)),
                  _const_spec((2, D_MODEL)),
                  _const_spec((D_MODEL, GATE_LORA)), _const_spec((GATE_LORA, D_MODEL)),
                  _const_spec((1, D_MODEL)), _const_spec((1, D_MODEL)), _const_spec((1, D_MODEL))],
        out_specs=[_row_spec(tm)] * 10,
        out_shape=[o_bf, o_bf, o_f32, o_bf, o_f32, o_f32, o_bf, o_bf, o_f32, o_f32],
        compiler_params=_params("parallel", "parallel"),
    )(x, x, x, g, mod, p["mu"], p["w_rkv"], p["w1"], p["w2"], p["w0"], p["a1"], p["a2"], p["a0"], p["g1"], p["g2"],
      p["k_k"], p["k_ac"], p["k_ah"])


def _wkv_kernel(r_ref, lw_ref, k_ref, v_ref, kk_ref, a_ref, s0_ref, y_ref, s_ref, state, *, n_chunks, n_pairs,
                reverse):
    C = CHUNK

    def mm(dims):
        return lambda a, b: lax.dot_general(a.astype(BF16), b.astype(BF16), dims, preferred_element_type=F32)

    hp = mm((((1,), (0,)), ((), ())))
    nt = mm((((1,), (1,)), ((), ())))
    tn = mm((((0,), (0,)), ((), ())))

    @pl.when(pl.program_id(2) == 0)
    def _():
        state[...] = s0_ref[0]

    pi = lax.broadcasted_iota(jnp.int32, (C, LANES), 0)
    pj = lax.broadcasted_iota(jnp.int32, (C, LANES), 1) % C
    strict = (pi < pj) if reverse else (pi > pj)
    incl = (pi <= pj) if reverse else (pi >= pj)
    eye = pi == pj
    ident = jnp.where(eye, 1.0, 0.0)
    head0 = lax.broadcasted_iota(jnp.int32, (1, LANES), 1) < RWKV_HEAD

    def stack(z):
        zb = z.astype(BF16)
        zero = jnp.zeros_like(zb)
        return jnp.concatenate([jnp.where(head0, zb, zero), jnp.where(head0, zero, zb)], axis=0)

    side = lambda x, y: jnp.concatenate([x, y], axis=1)
    diag_blocks = lambda z: jnp.where(head0, z[:C], z[C:])

    each = lambda f, *ls: [f(*xs) for xs in zip(*ls)]
    order = range(n_chunks - 1, -1, -1) if reverse else range(n_chunks)

    def cumsum(x):
        step = 1
        while step < C:
            if reverse:
                x = x + jnp.where(pi < C - step, pltpu.roll(x, C - step, 0), 0.0)
            else:
                x = x + jnp.where(pi >= step, pltpu.roll(x, step, 0), 0.0)
            step *= 2
        return x

    def chain(pairs):
        units = [(slice(c * C, (c + 1) * C), slice(p * LANES, (p + 1) * LANES)) for p in pairs for c in range(n_chunks)]
        load = lambda ref: [ref[0, rows, cols].astype(F32) for rows, cols in units]
        r, lw, k, v, kk, a = (load(ref) for ref in (r_ref, lw_ref, k_ref, v_ref, kk_ref, a_ref))
        cum = each(cumsum, lw)
        total = each(lambda z: z[0:1] if reverse else z[C - 1:C], cum)
        g_in = each(lambda z: jnp.exp(-z), cum)
        g_to_end = each(lambda t, z: jnp.exp(t - z), total, cum)
        b = each(lambda x, y: x * y, kk, a)
        a_u = each(lambda x, z, l: x * jnp.exp(z - l), kk, cum, lw)
        r_u = each(lambda x, z: x * jnp.exp(z), r, cum)
        b_s = each(lambda x, y: stack(x * y), b, g_in)
        k_s = each(lambda x, y: stack(x * y), k, g_in)
        v_s = each(stack, v)
        bh = each(lambda x, y: x * y, b, g_to_end)
        kh = each(lambda x, y: x * y, k, g_to_end)
        big = each(lambda a_, r_, b_, k_: nt(jnp.concatenate([a_, r_], axis=0), jnp.concatenate([b_, k_], axis=0)),
                   a_u, r_u, b_s, k_s)
        l_ab = each(lambda z: jnp.where(strict, z[:C, :LANES], 0.0), big)
        l_ak = each(lambda z: jnp.where(strict, z[:C, LANES:], 0.0), big)
        p_rb = each(lambda z: jnp.where(incl, z[C:, :LANES], 0.0), big)
        p_rk = each(lambda z: jnp.where(incl, z[C:, LANES:], 0.0), big)
        same_blk = lambda n: (pi // n) == (pj // n)
        l_d = each(lambda z: jnp.where(same_blk(INV_BLOCK), z, 0.0), l_ab)
        l2 = each(lambda d: hp(d, stack(d)), l_d)
        lpv = each(lambda x, y, vs: hp(jnp.concatenate([x, y], axis=0), vs), l_ak, p_rk, v_s)
        lv = each(lambda z: z[:C], lpv)
        prv = each(lambda z: z[C:], lpv)
        khv = each(tn, kh, v)
        l34 = each(lambda d, s2: hp(jnp.concatenate([d, s2], axis=0), stack(s2)), l_d, l2)
        p1 = each(lambda d, s2, s34: ident - d + s2 - s34[:C], l_d, l2, l34)
        t_inv = each(lambda p, s34: p + hp(p, stack(s34[C:])), p1, l34)
        n = INV_BLOCK
        while n < C:
            off = jnp.logical_and(same_blk(2 * n), jnp.logical_not(same_blk(n)))
            c_n = each(lambda z: jnp.where(off, z, 0.0), l_ab)
            tc = each(lambda t, c_: hp(t, stack(c_)), t_inv, c_n)
            t_inv = each(lambda t, x: t - hp(x, stack(t)), t_inv, tc)
            n *= 2
        w = each(lambda t, x, y: hp(t, side(stack(x), stack(y))), t_inv, a_u, lv)
        pw = each(lambda p, x: hp(p, side(stack(x[:, :LANES]), stack(x[:, LANES:]))), p_rb, w)
        bw = each(tn, bh, w)
        q1 = each(lambda x, y: x - y[:, :LANES], r_u, pw)
        y2 = each(lambda x, y: x - y[:, LANES:], prv, pw)
        g_mat = each(lambda t, y: jnp.where(eye, jnp.exp(t), 0.0) - diag_blocks(y[:, :LANES]), total, bw)
        h_mat = each(lambda x, y: diag_blocks(x) - diag_blocks(y[:, LANES:]), khv, bw)
        s = [state[p] for p in pairs]
        for c in order:
            for i, p in enumerate(pairs):
                u = i * n_chunks + c
                ys = hp(jnp.concatenate([q1[u], g_mat[u]], axis=0), stack(s[i]))
                y_ref[0, c * C:(c + 1) * C, p * LANES:(p + 1) * LANES] = ys[:C] + y2[u]
                s[i] = ys[C:] + h_mat[u]
        for i, p in enumerate(pairs):
            state[p] = s[i]
        return s

    final_state = chain(list(range(n_pairs)))

    @pl.when(pl.program_id(2) == pl.num_programs(2) - 1)
    def _():
        for p in range(n_pairs):
            s_ref[0, p] = final_state[p]


WKV_PAIRS = 8
WKV_CHUNKS = 4


def _wkv(r, lw, k, v, kk, a, s0, reverse):
    B, T, _ = r.shape
    rows = min(T, WKV_CHUNKS * CHUNK)
    n_steps = T // rows
    width = WKV_PAIRS * LANES
    tmap = (lambda b, p, j: (b, n_steps - 1 - j, p)) if reverse else (lambda b, p, j: (b, j, p))
    seq_spec = pl.BlockSpec((1, rows, width), tmap)
    st_spec = pl.BlockSpec((1, WKV_PAIRS, RWKV_HEAD, LANES), lambda b, p, j: (b, p, 0, 0))
    return pl.pallas_call(
        functools.partial(_wkv_kernel, n_chunks=rows // CHUNK, n_pairs=WKV_PAIRS, reverse=reverse),
        grid=(B, PAIRS // WKV_PAIRS, n_steps),
        in_specs=[seq_spec] * 6 + [st_spec],
        out_specs=[seq_spec, st_spec],
        out_shape=[jax.ShapeDtypeStruct(r.shape, F32), jax.ShapeDtypeStruct((B, PAIRS, RWKV_HEAD, LANES), F32)],
        scratch_shapes=[pltpu.VMEM((WKV_PAIRS, RWKV_HEAD, LANES), F32)],
        compiler_params=_params("parallel", "parallel", "arbitrary"),
    )(r, lw, k, v, kk, a, s0)


def _rwkv_branch(y0_ref, y1_ref, r_ref, v_ref, k0_ref, k1_ref, g_ref, rk_ref, lng_ref, lnb_ref, wo_ref):
    y = y0_ref[0] + y1_ref[0]
    inv_n = 1.0 / RWKV_HEAD
    mean = _seg_sum_wide(y) * inv_n
    yc = y - mean
    var = _seg_sum_wide(yc * yc) * inv_n
    yn = yc * lax.rsqrt(var + GN_EPS) * lng_ref[...] + lnb_ref[...]
    f32 = lambda ref: ref[0].astype(F32)
    bonus = _seg_sum_wide(f32(r_ref) * (f32(k0_ref) + f32(k1_ref)) * rk_ref[...]) * f32(v_ref)
    return jnp.dot(((yn + bonus) * f32(g_ref)).astype(BF16), wo_ref[...], preferred_element_type=F32)


def _rwkv_branch_specs():
    vec = _const_spec((1, D_MODEL))
    return [_row_spec] * 7 + [vec, vec, vec, _const_spec((D_MODEL, D_MODEL))]


HALO = SUBLANES


def _pool_branch(x_ref, xp_ref, xn_ref, g_ref, mod_ref, w_ref, sc_ref, *, seq_len):
    tm = x_ref.shape[1]
    i = pl.program_id(1)
    last = pl.num_programs(1) - 1
    g = g_ref[...]
    mod = mod_ref[0]
    h = _norm_mod(x_ref[0], g, mod, 0)
    hp = _norm_mod(xp_ref[0], g, mod, 0) * jnp.where(i == 0, 0.0, 1.0)
    hn = _norm_mod(xn_ref[0], g, mod, 0) * jnp.where(i == last, 0.0, 1.0)
    ext = jnp.concatenate([hp, h, hn], axis=0)
    n_ext = tm + 2 * HALO
    t = i * tm + lax.broadcasted_iota(jnp.int32, (tm, 1), 0)
    outs = []
    for gi, win in enumerate(POOL_WINDOWS):
        e = ext[:, gi * POOL_GROUP:(gi + 1) * POOL_GROUP]
        acc = e + pltpu.roll(e, 1, 0)
        step = 1
        while 2 * step < win:
            acc = pltpu.roll(acc, step, 0) + pltpu.roll(acc, n_ext - step, 0)
            step *= 2
        half = win // 2
        cnt = (jnp.minimum(t + half, seq_len) - jnp.maximum(t - half, 0)).astype(F32)
        pooled = acc[HALO:HALO + tm] / cnt - e[HALO:HALO + tm]
        outs.append(jnp.dot(pooled.astype(BF16), w_ref[gi], preferred_element_type=F32))
    return jnp.concatenate(outs, axis=1) * sc_ref[...]


def _pool_branch_specs(seq_len):
    n_blk = seq_len // HALO
    prev_spec = lambda tm: pl.BlockSpec((1, HALO, D_MODEL), lambda b, i: (b, jnp.maximum(i * (tm // HALO) - 1, 0), 0))
    next_spec = lambda tm: pl.BlockSpec(
        (1, HALO, D_MODEL), lambda b, i: (b, jnp.minimum((i + 1) * (tm // HALO), n_blk - 1), 0))
    return [_row_spec, prev_spec, next_spec, _const_spec((1, D_MODEL)), lambda tm: _mod_spec(),
            _const_spec((len(POOL_WINDOWS), POOL_GROUP, POOL_GROUP)), _const_spec((1, D_MODEL))]


def _rope_tables(n_tokens):
    rows = n_tokens // GRID_W
    n_freq = HEAD_DIM // 4
    inv = ROPE_THETA ** (-jnp.arange(n_freq, dtype=F32) / n_freq)
    ang_r = jnp.arange(rows, dtype=F32)[:, None] * inv
    ang_c = jnp.arange(GRID_W, dtype=F32)[:, None] * inv
    ang = jnp.concatenate([
        jnp.broadcast_to(ang_r[:, None, :], (rows, GRID_W, n_freq)),
        jnp.broadcast_to(ang_c[None, :, :], (rows, GRID_W, n_freq))], axis=-1).reshape(rows * GRID_W, 2 * n_freq)
    cos, sin = jnp.cos(ang), jnp.sin(ang)
    return jnp.tile(cos, (1, 4)), jnp.tile(sin, (1, 4))


def kernel(x, c, ctx, c_ctx, w_mod, b_mod, norm1_g, norm2_g, mlp_w_in, mlp_w_out, attn_w_qkv, attn_q_gain, attn_k_gain, attn_w_o, rwkv_mu, rwkv_w_rkv, rwkv_w0, rwkv_w1, rwkv_w2, rwkv_a0, rwkv_a1, rwkv_a2, rwkv_g1, rwkv_g2, rwkv_k_k, rwkv_k_a, rwkv_r_k, rwkv_ln_g, rwkv_ln_b, rwkv_w_o, pool_w, pool_scale):
    B, S, _ = x.shape
    L = ctx.shape[1]
    depth = w_mod.shape[0]
    assert x.shape[2] == D_MODEL and S % (4 * CHUNK) == 0 and L % CHUNK == 0 and S % GRID_W == 0

    n_rows = -(-(B + 1) // SUBLANES) * SUBLANES
    cvec = jnp.concatenate([c, c_ctx[None], jnp.zeros((n_rows - B - 1, D_MODEL), F32)], axis=0)
    mod_all = _modulation(cvec, w_mod, b_mod).reshape(depth, n_rows, N_MOD, D_MODEL)
    cos_t, sin_t = _rope_tables(S)
    zero_tab = jnp.zeros((L, LANES), F32)
    row = lambda a: a.reshape(1, -1)

    for i in range(depth):
        last = i == depth - 1
        j = i // N_MIXERS
        mod_l = mod_all[i, :B]
        mod_c = jnp.broadcast_to(mod_all[i, B][None], (B, N_MOD, D_MODEL))
        g1 = row(norm1_g[i])
        kind = i % N_MIXERS
        if kind == 0:
            w_qkv = attn_w_qkv[j].astype(BF16)
            w_o = attn_w_o[j].astype(BF16)
            qg = jnp.tile(row(attn_q_gain[j]), (1, 2))
            kg = jnp.tile(row(attn_k_gain[j]), (1, 2))
            q_l, k_l, v_l = _qkv_project(x, g1, mod_l, w_qkv, qg, kg, cos_t, sin_t, True)
            q_c, k_c, v_c = _qkv_project(ctx, g1, mod_c, w_qkv, qg, kg, zero_tab, zero_tab, False)
            proj_specs = [_row_spec, _const_spec((D_MODEL, D_MODEL))]
            mix_l = (_proj_branch, (_attention(q_l, [(k_l, v_l), (k_c, v_c)]), w_o), proj_specs, 512)
            if not last:
                mix_c = (_proj_branch, (_attention(q_c, [(k_c, v_c)]), w_o), proj_specs, 512)
        elif kind == 1:
            p = {
                "mu": rwkv_mu[j], "w_rkv": rwkv_w_rkv[j].astype(BF16),
                "w1": jnp.concatenate([rwkv_w1[j, 0], rwkv_w1[j, 1]], axis=1).astype(BF16),
                "w2": (0.5 * rwkv_w2[j]).astype(BF16), "w0": 0.5 * rwkv_w0[j],
                "a1": jnp.concatenate([rwkv_a1[j, 0], rwkv_a1[j, 1]], axis=1).astype(BF16),
                "a2": (0.5 * rwkv_a2[j]).astype(BF16), "a0": 0.5 * rwkv_a0[j],
                "g1": rwkv_g1[j].astype(BF16), "g2": rwkv_g2[j].astype(BF16),
                "k_k": row(rwkv_k_k[j]), "k_ac": row(1.0 - 0.5 * rwkv_k_a[j]), "k_ah": row(0.5 * rwkv_k_a[j]), "r_k": row(rwkv_r_k[j]),
                "ln_g": row(rwkv_ln_g[j]), "ln_b": row(rwkv_ln_b[j]), "w_o": rwkv_w_o[j].astype(BF16),
            }
            r_l, v_l, kk_l, g_l, lw0_l, lw1_l, key0_l, key1_l, a0_l, a1_l = _rwkv_prepare(x, g1, mod_l, p)
            r_c, v_c, kk_c, g_c, lw0_c, lw1_c, key0_c, key1_c, a0_c, a1_c = _rwkv_prepare(ctx, g1, mod_c, p)
            zero_state = jnp.zeros((B, PAIRS, RWKV_HEAD, LANES), F32)
            ys_l, ys_c = [], []
            for rev, (lw_l, key_l, a_l, lw_c, key_c, a_c) in enumerate(
                    ((lw0_l, key0_l, a0_l, lw0_c, key0_c, a0_c), (lw1_l, key1_l, a1_l, lw1_c, key1_c, a1_c))):
                y_c, s_ctx = _wkv(r_c, lw_c, key_c, v_c, kk_c, a_c, zero_state, bool(rev))
                y_l, _ = _wkv(r_l, lw_l, key_l, v_l, kk_l, a_l, s_ctx, bool(rev))
                ys_l.append(y_l)
                ys_c.append(y_c)
            consts = (p["r_k"], p["ln_g"], p["ln_b"], p["w_o"])
            mix_l = (_rwkv_branch, (ys_l[0], ys_l[1], r_l, v_l, key0_l, key1_l, g_l) + consts, _rwkv_branch_specs(), 512)
            if not last:
                mix_c = (_rwkv_branch, (ys_c[0], ys_c[1], r_c, v_c, key0_c, key1_c, g_c) + consts,
                         _rwkv_branch_specs(), 256)
        else:
            w_p = pool_w[j].astype(BF16)
            sc = row(pool_scale[j])
            mix_l = (functools.partial(_pool_branch, seq_len=S), (x, x, x, g1, mod_l, w_p, sc), _pool_branch_specs(S), 512)
            if not last:
                mix_c = (functools.partial(_pool_branch, seq_len=L), (ctx, ctx, ctx, g1, mod_c, w_p, sc),
                         _pool_branch_specs(L), 256)
        g2 = row(norm2_g[i])
        w_in = mlp_w_in[i].astype(BF16)
        w_out = mlp_w_out[i].astype(BF16)
        x = _mixer_mlp(x, mod_l, g2, w_in, w_out, *mix_l)
        if not last:
            ctx = _mixer_mlp(ctx, mod_c, g2, w_in, w_out, *mix_c)
    return x
```
